```python
import jax, jax.numpy as jnp
from jax import lax
import numpy as np

D_MODEL = 1024
BATCH = 8
SEQ = 2048
DEPTH = 2

PLE_DIM = 256
D_FF = ((8 * D_MODEL // 3 + 255) // 256) * 256
D_MIX = D_MODEL
W_A = D_MIX // 2
GM_GROUPS = 4
GM_DH = W_A // GM_GROUPS
CHUNK = 128
W_B = D_MIX - W_A
SC_HEADS = 8
SC_WIDTH = 3
W_C = D_MIX // 2
POOL_WINDOWS = (2, 4, 8, 16)
POOL_GW = W_C // len(POOL_WINDOWS)
W_D = D_MIX - W_C
CV_HEADS = 8
CV_WIDTH = 31
N_EVEN = (DEPTH + 1) // 2
N_ODD = DEPTH // 2
IN_EVEN = 2 * W_A + 3 * W_B
IN_ODD = W_C + 2 * W_D
RMS_EPS = 1e-6
LN_EPS = 1e-5

kernel_name = "hybrid_gmlp_shortconv_pool_conformer_macaron"


def rmsnorm(x, g):
    xf = x.astype(jnp.float32)
    y = xf * lax.rsqrt(jnp.mean(xf * xf, axis=-1, keepdims=True) + RMS_EPS)
    return (y * g.astype(jnp.float32)).astype(x.dtype)


def layernorm(x, g, b):
    xf = x.astype(jnp.float32)
    mu = jnp.mean(xf, axis=-1, keepdims=True)
    xc = xf - mu
    var = jnp.mean(xc * xc, axis=-1, keepdims=True)
    y = xc * lax.rsqrt(var + LN_EPS) * g.astype(jnp.float32) + b.astype(jnp.float32)
    return y.astype(x.dtype)


def swiglu(x, w_gu, w_down):
    g, u = jnp.split(x @ w_gu, 2, axis=-1)
    return (jax.nn.silu(g) * u) @ w_down


def causal_dwconv(x, w):
    k, c = w.shape
    return lax.conv_general_dilated(
        x, w[:, None, :].astype(x.dtype), window_strides=(1,), padding=[(k - 1, 0)],
        dimension_numbers=("NWC", "WIO", "NWC"), feature_group_count=c)


def gmlp_spatial_gating(uv, ln_g, ln_b, w_s, b_s):
    u, v = jnp.split(jax.nn.gelu(uv), 2, axis=-1)
    bsz, s, _ = v.shape
    vg = layernorm(v.reshape(bsz, s, GM_GROUPS, GM_DH),
                   ln_g.reshape(GM_GROUPS, GM_DH), ln_b.reshape(GM_GROUPS, GM_DH))
    vc = vg.reshape(bsz, s // CHUNK, CHUNK, GM_GROUPS, GM_DH)
    w = jnp.tril(w_s)
    mixed = jnp.einsum("gts,bcsgd->bctgd", w, vc) + b_s.T[:, :, None]
    return u * mixed.reshape(bsz, s, W_A)


def short_conv_mixer(bg, cg, xv, w):
    return bg * causal_dwconv(cg * xv, w)


def multiscale_pool(x):
    s = x.shape[1]
    xf = x.astype(jnp.float32)
    cs = jnp.cumsum(xf, axis=1)
    cnt_base = jnp.arange(1, s + 1, dtype=jnp.float32)[None, :, None]
    outs = []
    for gi, win in enumerate(POOL_WINDOWS):
        sl = slice(gi * POOL_GW, (gi + 1) * POOL_GW)
        c = cs[:, :, sl]
        prev = jnp.pad(c[:, : s - win], ((0, 0), (win, 0), (0, 0)))
        count = jnp.minimum(cnt_base, float(win))
        outs.append((c - prev) / count - xf[:, :, sl])
    return jnp.concatenate(outs, axis=-1).astype(x.dtype)


def pool_mixer(zc, pool_w, pool_scale):
    bsz, s, _ = zc.shape
    pooled = multiscale_pool(zc).reshape(bsz, s, len(POOL_WINDOWS), POOL_GW)
    y = jnp.einsum("bsgc,gcd->bsgd", pooled, pool_w).reshape(bsz, s, W_C)
    return y * pool_scale


def conformer_conv(zd, cv_w, cv_b, ln_g, ln_b):
    a, g = jnp.split(zd, 2, axis=-1)
    h = causal_dwconv(a * jax.nn.sigmoid(g), cv_w) + cv_b
    return jax.nn.silu(layernorm(h, ln_g, ln_b))


def setup_inputs(seed: int = 0) -> dict:
    key = jax.random.key(seed)
    ks = iter(jax.random.split(key, 40))
    f32 = jnp.float32

    def nrm(shape, scale):
        return jax.random.normal(next(ks), shape, f32) * scale

    def gain(shape):
        return 1.0 + nrm(shape, 0.02)

    return {
        "x": nrm((BATCH, SEQ, D_MODEL), 1.0),
        "p": nrm((DEPTH, BATCH, SEQ, PLE_DIM), 1.0),
        "ffn1_norm": gain((DEPTH, D_MODEL)),
        "ffn1_w_gu": nrm((DEPTH, D_MODEL, 2 * D_FF), D_MODEL ** -0.5),
        "ffn1_w_down": nrm((DEPTH, D_FF, D_MODEL), D_FF ** -0.5),
        "mix_norm": gain((DEPTH, D_MODEL)),
        "ffn2_norm": gain((DEPTH, D_MODEL)),
        "ffn2_w_gu": nrm((DEPTH, D_MODEL, 2 * D_FF), D_MODEL ** -0.5),
        "ffn2_w_down": nrm((DEPTH, D_FF, D_MODEL), D_FF ** -0.5),
        "ple_norm": gain((DEPTH, D_MODEL)),
        "ple_w_gate": nrm((DEPTH, D_MODEL, D_MODEL), D_MODEL ** -0.5),
        "ple_w_up": nrm((DEPTH, PLE_DIM, D_MODEL), PLE_DIM ** -0.5),
        "ab_w_in": nrm((N_EVEN, D_MODEL, IN_EVEN), D_MODEL ** -0.5),
        "gm_ln_g": gain((N_EVEN, W_A)),
        "gm_ln_b": nrm((N_EVEN, W_A), 0.02),
        "gm_w_s": nrm((N_EVEN, GM_GROUPS, CHUNK, CHUNK), CHUNK ** -0.5),
        "gm_b_s": gain((N_EVEN, GM_GROUPS, CHUNK)),
        "sc_w": nrm((N_EVEN, SC_WIDTH, W_B), SC_WIDTH ** -0.5),
        "ab_w_out": nrm((N_EVEN, W_A + W_B, D_MODEL), (W_A + W_B) ** -0.5),
        "cd_w_in": nrm((N_ODD, D_MODEL, IN_ODD), D_MODEL ** -0.5),
        "pool_w": nrm((N_ODD, len(POOL_WINDOWS), POOL_GW, POOL_GW), POOL_GW ** -0.5),
        "pool_scale": gain((N_ODD, W_C)),
        "cv_w": nrm((N_ODD, CV_WIDTH, W_D), CV_WIDTH ** -0.5),
        "cv_b": nrm((N_ODD, W_D), 0.02),
        "cv_ln_g": gain((N_ODD, W_D)),
        "cv_ln_b": nrm((N_ODD, W_D), 0.02),
        "cd_w_out": nrm((N_ODD, W_C + W_D, D_MODEL), (W_C + W_D) ** -0.5),
        "final_norm": gain((D_MODEL,)),
    }


def reference(x, p, ffn1_norm, ffn1_w_gu, ffn1_w_down, mix_norm, ffn2_norm, ffn2_w_gu,
              ffn2_w_down, ple_norm, ple_w_gate, ple_w_up, ab_w_in, gm_ln_g, gm_ln_b,
              gm_w_s, gm_b_s, sc_w, ab_w_out, cd_w_in, pool_w, pool_scale, cv_w, cv_b,
              cv_ln_g, cv_ln_b, cd_w_out, final_norm):
    h = x
    for i in range(DEPTH):
        h = h + 0.5 * swiglu(rmsnorm(h, ffn1_norm[i]), ffn1_w_gu[i], ffn1_w_down[i])
        hn = rmsnorm(h, mix_norm[i])
        j = i // 2
        if i % 2 == 0:
            z = hn @ ab_w_in[j]
            uv = z[..., : 2 * W_A]
            bg, cg, xv = jnp.split(z[..., 2 * W_A:], 3, axis=-1)
            ya = gmlp_spatial_gating(uv, gm_ln_g[j], gm_ln_b[j], gm_w_s[j], gm_b_s[j])
            yb = short_conv_mixer(bg, cg, xv, sc_w[j])
            y = jnp.concatenate([ya, yb], axis=-1) @ ab_w_out[j]
        else:
            z = hn @ cd_w_in[j]
            yc = pool_mixer(z[..., :W_C], pool_w[j], pool_scale[j])
            yd = conformer_conv(z[..., W_C:], cv_w[j], cv_b[j], cv_ln_g[j], cv_ln_b[j])
            y = jnp.concatenate([yc, yd], axis=-1) @ cd_w_out[j]
        h = h + y
        h = h + 0.5 * swiglu(rmsnorm(h, ffn2_norm[i]), ffn2_w_gu[i], ffn2_w_down[i])
        gate = jax.nn.sigmoid(rmsnorm(h, ple_norm[i]) @ ple_w_gate[i])
        h = h + gate * (p[i] @ ple_w_up[i])
    return rmsnorm(h, final_norm)
```

```python
import functools

import jax
import jax.numpy as jnp
from jax import lax
from jax.experimental import pallas as pl
from jax.experimental.pallas import tpu as pltpu

F32 = jnp.float32
BF16 = jnp.bfloat16

D_MODEL = 1024
PLE_DIM = 256
D_FF = 2816
HALF = 512
GROUP_W = 128
N_GROUPS = 4
CHUNK = 128
POOL_WINDOWS = (2, 4, 8, 16)
SC_WIDTH = 3
CV_WIDTH = 31
RMS_EPS = 1e-6
LN_EPS = 1e-5

FFN_TM = 512
FFN_FC = 256
MIX_TM = 512
SC_HALO = 8
POOL_HALO = 16
CV_HALO = 32
CV_ROWS = 32
VMEM_LIMIT = 52 * 1024 * 1024


def _rmsnorm(x, g):
    ms = jnp.mean(x * x, axis=-1, keepdims=True)
    return x * lax.rsqrt(ms + RMS_EPS) * g


def _layernorm(x, g, b):
    mu = jnp.mean(x, axis=-1, keepdims=True)
    xc = x - mu
    var = jnp.mean(xc * xc, axis=-1, keepdims=True)
    return xc * lax.rsqrt(var + LN_EPS) * g + b


def _sigmoid(x):
    return 1.0 / (1.0 + jnp.exp(-x))


def _gelu_tanh(x):
    c = 0.7978845608028654
    return 0.5 * x * (1.0 + jnp.tanh(c * (x + 0.044715 * (x * x * x))))


def _dot(a, b):
    return jnp.dot(a, b, preferred_element_type=F32)


def _ffn_kernel(*refs, with_ple, with_final):
    h_ref, nw_ref, wg_ref, wu_ref, wd_ref = refs[:5]
    k = 5
    if with_ple:
        p_ref, pn_ref, wgate_ref, wup_ref = refs[k:k + 4]
        k += 4
    if with_final:
        fn_ref = refs[k]
        k += 1
    o_ref, xn_ref, a_ref = refs[k:k + 3]

    x = h_ref[...]
    xn_ref[...] = _rmsnorm(x, nw_ref[...]).astype(BF16)
    for c in range(D_FF // FFN_FC):
        sl = slice(c * FFN_FC, (c + 1) * FFN_FC)
        g = _dot(xn_ref[...], wg_ref[:, sl])
        u = _dot(xn_ref[...], wu_ref[:, sl])
        a_ref[:, sl] = (g * _sigmoid(g) * u).astype(BF16)
    h1 = x + 0.5 * _dot(a_ref[...], wd_ref[...])
    if with_ple:
        xn_ref[...] = _rmsnorm(h1, pn_ref[...]).astype(BF16)
        gate = _sigmoid(_dot(xn_ref[...], wgate_ref[...]))
        up = _dot(p_ref[...].astype(BF16), wup_ref[...])
        h1 = h1 + gate * up
    if with_final:
        h1 = _rmsnorm(h1, fn_ref[...])
    o_ref[...] = h1


def _const_spec(shape):
    nd = len(shape)
    return pl.BlockSpec(shape, lambda *_: (0,) * nd, pipeline_mode=pl.Buffered(1))


def _ffn_call(h, norm_w, w_g, w_u, w_d, ple=None, final_norm=None):
    t = h.shape[0]
    tm = FFN_TM
    row_spec = pl.BlockSpec((tm, D_MODEL), lambda i: (i, 0))
    in_specs = [row_spec, _const_spec((1, D_MODEL)), _const_spec((D_MODEL, D_FF)),
                _const_spec((D_MODEL, D_FF)), _const_spec((D_FF, D_MODEL))]
    args = [h, norm_w, w_g, w_u, w_d]
    if ple is not None:
        p, pn, wgate, wup = ple
        in_specs += [pl.BlockSpec((tm, PLE_DIM), lambda i: (i, 0)), _const_spec((1, D_MODEL)),
                     _const_spec((D_MODEL, D_MODEL)), _const_spec((PLE_DIM, D_MODEL))]
        args += [p, pn, wgate, wup]
    if final_norm is not None:
        in_specs.append(_const_spec((1, D_MODEL)))
        args.append(final_norm)
    return pl.pallas_call(
        functools.partial(_ffn_kernel, with_ple=ple is not None, with_final=final_norm is not None),
        grid=(t // tm,),
        in_specs=in_specs,
        out_specs=row_spec,
        out_shape=jax.ShapeDtypeStruct((t, D_MODEL), F32),
        scratch_shapes=[pltpu.VMEM((tm, D_MODEL), BF16), pltpu.VMEM((tm, D_FF), BF16)],
        compiler_params=pltpu.CompilerParams(dimension_semantics=("arbitrary",),
                                             vmem_limit_bytes=VMEM_LIMIT),
        name="ffn_ple" if ple is not None else "ffn",
    )(*args)


def _mix_even_kernel(h_ref, nw_ref, win_ref, lng_ref, lnb_ref, ws_ref, bs_ref, scw_ref, wout_ref,
                     o_ref, hn_ref, z_ref, vn_ref, q_ref, y_ref):
    tm = MIX_TM
    j = pl.program_id(1)
    x = h_ref[0]
    hn_ref[...] = _rmsnorm(x, nw_ref[...]).astype(BF16)
    z_ref[...] = _dot(hn_ref[...], win_ref[...])

    row = lax.broadcasted_iota(jnp.int32, (CHUNK, CHUNK), 0)
    col = lax.broadcasted_iota(jnp.int32, (CHUNK, CHUNK), 1)
    for g in range(N_GROUPS):
        gsl = slice(g * GROUP_W, (g + 1) * GROUP_W)
        v = _gelu_tanh(z_ref[:, HALF + g * GROUP_W:HALF + (g + 1) * GROUP_W])
        vn_ref[:, gsl] = _layernorm(v, lng_ref[:, gsl], lnb_ref[:, gsl]).astype(BF16)
    for g in range(N_GROUPS):
        gsl = slice(g * GROUP_W, (g + 1) * GROUP_W)
        wt = jnp.where(col <= row, ws_ref[g], 0.0).astype(BF16)
        for c in range(tm // CHUNK):
            rsl = slice(c * CHUNK, (c + 1) * CHUNK)
            mixed = _dot(wt, vn_ref[rsl, gsl]) + bs_ref[:, gsl]
            u = _gelu_tanh(z_ref[rsl, gsl])
            y_ref[rsl, gsl] = (u * mixed).astype(BF16)

    @pl.when(j == 0)
    def _():
        q_ref[0:SC_HALO, :] = jnp.zeros((SC_HALO, HALF), F32)

    q = z_ref[:, 3 * HALF:4 * HALF] * z_ref[:, 4 * HALF:5 * HALF]
    q_ref[SC_HALO:SC_HALO + tm, :] = q
    conv = scw_ref[2:3, :] * q
    conv = conv + scw_ref[1:2, :] * q_ref[SC_HALO - 1:SC_HALO - 1 + tm, :]
    conv = conv + scw_ref[0:1, :] * q_ref[SC_HALO - 2:SC_HALO - 2 + tm, :]
    y_ref[:, HALF:2 * HALF] = (z_ref[:, 2 * HALF:3 * HALF] * conv).astype(BF16)
    q_ref[0:SC_HALO, :] = q_ref[tm:tm + SC_HALO, :]

    o_ref[0] = x + _dot(y_ref[...], wout_ref[...])


def _mix_even_call(h3, norm_w, w_in, ln_g, ln_b, w_s, b_full, sc_w, w_out):
    b, s, _ = h3.shape
    tm = MIX_TM
    row_spec = pl.BlockSpec((1, tm, D_MODEL), lambda i, j: (i, j, 0))
    in_specs = [row_spec, _const_spec((1, D_MODEL)), _const_spec((D_MODEL, 5 * HALF)),
                _const_spec((1, HALF)), _const_spec((1, HALF)),
                _const_spec((N_GROUPS, CHUNK, CHUNK)), _const_spec((CHUNK, HALF)),
                _const_spec((SC_WIDTH, HALF)), _const_spec((D_MODEL, D_MODEL))]
    return pl.pallas_call(
        _mix_even_kernel,
        grid=(b, s // tm),
        in_specs=in_specs,
        out_specs=row_spec,
        out_shape=jax.ShapeDtypeStruct(h3.shape, F32),
        scratch_shapes=[pltpu.VMEM((tm, D_MODEL), BF16),
                        pltpu.VMEM((tm, 5 * HALF), F32),
                        pltpu.VMEM((tm, HALF), BF16),
                        pltpu.VMEM((SC_HALO + tm, HALF), F32),
                        pltpu.VMEM((tm, D_MODEL), BF16)],
        compiler_params=pltpu.CompilerParams(dimension_semantics=("arbitrary", "arbitrary"),
                                             vmem_limit_bytes=VMEM_LIMIT),
        name="mix_even",
    )(h3, norm_w, w_in, ln_g, ln_b, w_s, b_full, sc_w, w_out)


def _mix_odd_kernel(h_ref, nw_ref, win_ref, pw_ref, ps_ref, cvw_ref, cvb_ref, lng_ref, lnb_ref,
                    wout_ref, o_ref, hn_ref, z_ref, p_ref, c_ref, hc_ref, y_ref):
    tm = MIX_TM
    j = pl.program_id(1)
    x = h_ref[0]
    hn_ref[...] = _rmsnorm(x, nw_ref[...]).astype(BF16)
    z_ref[...] = _dot(hn_ref[...], win_ref[...])

    @pl.when(j == 0)
    def _():
        p_ref[0:POOL_HALO, :] = jnp.zeros((POOL_HALO, HALF), F32)
        c_ref[0:CV_HALO, :] = jnp.zeros((CV_HALO, HALF), F32)

    p_ref[POOL_HALO:POOL_HALO + tm, :] = z_ref[:, 0:HALF]
    pos = j * tm + lax.broadcasted_iota(jnp.int32, (tm, GROUP_W), 0)
    for gi, win in enumerate(POOL_WINDOWS):
        gsl = slice(gi * GROUP_W, (gi + 1) * GROUP_W)
        xg = p_ref[POOL_HALO:POOL_HALO + tm, gsl]
        s = xg
        for d in range(1, win):
            s = s + p_ref[POOL_HALO - d:POOL_HALO - d + tm, gsl]
        count = jnp.minimum(pos + 1, win).astype(F32)
        pooled = s / count - xg
        yc = _dot(pooled.astype(BF16), pw_ref[gi]) * ps_ref[:, gsl]
        y_ref[:, gsl] = yc.astype(BF16)
    p_ref[0:POOL_HALO, :] = p_ref[tm:tm + POOL_HALO, :]

    c_ref[CV_HALO:CV_HALO + tm, :] = z_ref[:, HALF:2 * HALF] * _sigmoid(z_ref[:, 2 * HALF:3 * HALF])
    for r in range(tm // CV_ROWS):
        base = CV_HALO + r * CV_ROWS - (CV_WIDTH - 1)
        acc = cvw_ref[0:1, :] * c_ref[base:base + CV_ROWS, :]
        for t in range(1, CV_WIDTH):
            acc = acc + cvw_ref[t:t + 1, :] * c_ref[base + t:base + t + CV_ROWS, :]
        hc_ref[r * CV_ROWS:(r + 1) * CV_ROWS, :] = acc + cvb_ref[...]
    c_ref[0:CV_HALO, :] = c_ref[tm:tm + CV_HALO, :]
    ln = _layernorm(hc_ref[...], lng_ref[...], lnb_ref[...])
    y_ref[:, HALF:2 * HALF] = (ln * _sigmoid(ln)).astype(BF16)

    o_ref[0] = x + _dot(y_ref[...], wout_ref[...])


def _mix_odd_call(h3, norm_w, w_in, pool_w, pool_scale, cv_w, cv_b, ln_g, ln_b, w_out):
    b, s, _ = h3.shape
    tm = MIX_TM
    row_spec = pl.BlockSpec((1, tm, D_MODEL), lambda i, j: (i, j, 0))
    in_specs = [row_spec, _const_spec((1, D_MODEL)), _const_spec((D_MODEL, 3 * HALF)),
                _const_spec((N_GROUPS, GROUP_W, GROUP_W)), _const_spec((1, HALF)),
                _const_spec((CV_WIDTH, HALF)), _const_spec((1, HALF)),
                _const_spec((1, HALF)), _const_spec((1, HALF)), _const_spec((D_MODEL, D_MODEL))]
    return pl.pallas_call(
        _mix_odd_kernel,
        grid=(b, s // tm),
        in_specs=in_specs,
        out_specs=row_spec,
        out_shape=jax.ShapeDtypeStruct(h3.shape, F32),
        scratch_shapes=[pltpu.VMEM((tm, D_MODEL), BF16),
                        pltpu.VMEM((tm, 3 * HALF), F32),
                        pltpu.VMEM((POOL_HALO + tm, HALF), F32),
                        pltpu.VMEM((CV_HALO + tm, HALF), F32),
                        pltpu.VMEM((tm, HALF), F32),
                        pltpu.VMEM((tm, D_MODEL), BF16)],
        compiler_params=pltpu.CompilerParams(dimension_semantics=("arbitrary", "arbitrary"),
                                             vmem_limit_bytes=VMEM_LIMIT),
        name="mix_odd",
    )(h3, norm_w, w_in, pool_w, pool_scale, cv_w, cv_b, ln_g, ln_b, w_out)


def kernel(x, p, ffn1_norm, ffn1_w_gu, ffn1_w_down, mix_norm, ffn2_norm, ffn2_w_gu, ffn2_w_down,
           ple_norm, ple_w_gate, ple_w_up, ab_w_in, gm_ln_g, gm_ln_b, gm_w_s, gm_b_s, sc_w,
           ab_w_out, cd_w_in, pool_w, pool_scale, cv_w, cv_b, cv_ln_g, cv_ln_b, cd_w_out,
           final_norm):
    bsz, seq, d = x.shape
    depth = p.shape[0]
    t = bsz * seq
    row = lambda v: v.reshape(1, -1)
    h = x.reshape(t, d)
    for i in range(depth):
        j = i // 2
        h = _ffn_call(h, row(ffn1_norm[i]), ffn1_w_gu[i, :, :D_FF].astype(BF16),
                      ffn1_w_gu[i, :, D_FF:].astype(BF16), ffn1_w_down[i].astype(BF16))
        h3 = h.reshape(bsz, seq, d)
        if i % 2 == 0:
            b_full = jnp.repeat(gm_b_s[j].T, GROUP_W, axis=1)
            h3 = _mix_even_call(h3, row(mix_norm[i]), ab_w_in[j].astype(BF16), row(gm_ln_g[j]),
                                row(gm_ln_b[j]), gm_w_s[j], b_full, sc_w[j],
                                ab_w_out[j].astype(BF16))
        else:
            h3 = _mix_odd_call(h3, row(mix_norm[i]), cd_w_in[j].astype(BF16),
                               pool_w[j].astype(BF16), row(pool_scale[j]), cv_w[j], row(cv_b[j]),
                               row(cv_ln_g[j]), row(cv_ln_b[j]), cd_w_out[j].astype(BF16))
        h = h3.reshape(t, d)
        h = _ffn_call(h, row(ffn2_norm[i]), ffn2_w_gu[i, :, :D_FF].astype(BF16),
                      ffn2_w_gu[i, :, D_FF:].astype(BF16), ffn2_w_down[i].astype(BF16),
                      ple=(p[i].reshape(t, PLE_DIM), row(ple_norm[i]), ple_w_gate[i].astype(BF16),
                           ple_w_up[i].astype(BF16)),
                      final_norm=row(final_norm) if i == depth - 1 else None)
    return h.reshape(bsz, seq, d)
```

```python
import functools

import jax
import jax.numpy as jnp
from jax import lax
from jax.experimental import pallas as pl
from jax.experimental.pallas import tpu as pltpu

F32 = jnp.float32
BF16 = jnp.bfloat16

D_MODEL = 1024
PLE_DIM = 256
D_FF = 2816
HALF = 512
LANES = 128
SUBLANES = 8
GROUP_W = 128
N_GROUPS = 4
CHUNK = 128
POOL_WINDOWS = (2, 4, 8, 16)
SC_WIDTH = 3
CV_WIDTH = 31
RMS_EPS = 1e-6
LN_EPS = 1e-5

FFN_TM = 512
FFN_FC = 256
MIX_TM = 512
SC_HALO = 8
HALO = 32
CV_ROWS = 64
VMEM_LIMIT = 52 * 1024 * 1024


def _rmsnorm(x, g):
    ms = jnp.mean(x * x, axis=-1, keepdims=True)
    return x * lax.rsqrt(ms + RMS_EPS) * g


def _layernorm(x, g, b):
    mu = jnp.mean(x, axis=-1, keepdims=True)
    xc = x - mu
    var = jnp.mean(xc * xc, axis=-1, keepdims=True)
    return xc * lax.rsqrt(var + LN_EPS) * g + b


def _sigmoid(x):
    return 1.0 / (1.0 + jnp.exp(-x))


def _gelu_tanh(x):
    c = 0.7978845608028654
    return 0.5 * x * (1.0 + jnp.tanh(c * (x + 0.044715 * (x * x * x))))


def _dot(a, b):
    return jnp.dot(a, b, preferred_element_type=F32)


def _ffn_kernel(*refs, with_ple, with_final):
    h_ref, nw_ref, wgu_ref, wd_ref = refs[:4]
    k = 4
    if with_ple:
        p_ref, pn_ref, wgate_ref, wup_ref = refs[k:k + 4]
        k += 4
    if with_final:
        fn_ref = refs[k]
        k += 1
    o_ref, xn_ref, a_ref = refs[k:k + 3]

    x = h_ref[...]
    xn_ref[...] = _rmsnorm(x, nw_ref[...]).astype(BF16)
    for c in range(D_FF // FFN_FC):
        sl = slice(c * FFN_FC, (c + 1) * FFN_FC)
        g = _dot(xn_ref[...], wgu_ref[:, sl])
        u = _dot(xn_ref[...], wgu_ref[:, D_FF + c * FFN_FC:D_FF + (c + 1) * FFN_FC])
        a_ref[:, sl] = (g * _sigmoid(g) * u).astype(BF16)
    h1 = x + 0.5 * _dot(a_ref[...], wd_ref[...])
    if with_ple:
        xn_ref[...] = _rmsnorm(h1, pn_ref[...]).astype(BF16)
        gate = _sigmoid(_dot(xn_ref[...], wgate_ref[...]))
        up = _dot(p_ref[...].astype(BF16), wup_ref[...])
        h1 = h1 + gate * up
    if with_final:
        h1 = _rmsnorm(h1, fn_ref[...])
    o_ref[...] = h1


def _const_spec(shape):
    nd = len(shape)
    return pl.BlockSpec(shape, lambda *_: (0,) * nd, pipeline_mode=pl.Buffered(1))


def _ffn_call(h, norm_w, w_gu, w_d, ple=None, final_norm=None):
    t = h.shape[0]
    tm = FFN_TM
    row_spec = pl.BlockSpec((tm, D_MODEL), lambda i: (i, 0))
    in_specs = [row_spec, _const_spec((1, D_MODEL)), _const_spec((D_MODEL, 2 * D_FF)),
                _const_spec((D_FF, D_MODEL))]
    args = [h, norm_w, w_gu, w_d]
    if ple is not None:
        p, pn, wgate, wup = ple
        in_specs += [pl.BlockSpec((tm, PLE_DIM), lambda i: (i, 0)), _const_spec((1, D_MODEL)),
                     _const_spec((D_MODEL, D_MODEL)), _const_spec((PLE_DIM, D_MODEL))]
        args += [p, pn, wgate, wup]
    if final_norm is not None:
        in_specs.append(_const_spec((1, D_MODEL)))
        args.append(final_norm)
    return pl.pallas_call(
        functools.partial(_ffn_kernel, with_ple=ple is not None, with_final=final_norm is not None),
        grid=(t // tm,),
        in_specs=in_specs,
        out_specs=row_spec,
        out_shape=jax.ShapeDtypeStruct((t, D_MODEL), F32),
        scratch_shapes=[pltpu.VMEM((tm, D_MODEL), BF16), pltpu.VMEM((tm, D_FF), BF16)],
        compiler_params=pltpu.CompilerParams(dimension_semantics=("arbitrary",),
                                             vmem_limit_bytes=VMEM_LIMIT),
        name="ffn_ple" if ple is not None else "ffn",
    )(*args)


def _mix_even_kernel(h_ref, nw_ref, win_ref, lng_ref, lnb_ref, ws_ref, bs_ref, scw_ref, wout_ref,
                     o_ref, hn_ref, z_ref, vn_ref, q_ref, y_ref):
    tm = MIX_TM
    j = pl.program_id(1)
    x = h_ref[0]
    hn_ref[...] = _rmsnorm(x, nw_ref[...]).astype(BF16)
    z_ref[...] = _dot(hn_ref[...], win_ref[...])

    row = lax.broadcasted_iota(jnp.int32, (CHUNK, CHUNK), 0)
    col = lax.broadcasted_iota(jnp.int32, (CHUNK, CHUNK), 1)
    for g in range(N_GROUPS):
        gsl = slice(g * GROUP_W, (g + 1) * GROUP_W)
        v = _gelu_tanh(z_ref[:, HALF + g * GROUP_W:HALF + (g + 1) * GROUP_W])
        vn_ref[:, gsl] = _layernorm(v, lng_ref[:, gsl], lnb_ref[:, gsl]).astype(BF16)
    for g in range(N_GROUPS):
        gsl = slice(g * GROUP_W, (g + 1) * GROUP_W)
        wt = jnp.where(col <= row, ws_ref[g], 0.0).astype(BF16)
        for c in range(tm // CHUNK):
            rsl = slice(c * CHUNK, (c + 1) * CHUNK)
            mixed = _dot(wt, vn_ref[rsl, gsl]) + bs_ref[:, gsl]
            u = _gelu_tanh(z_ref[rsl, gsl])
            y_ref[rsl, gsl] = (u * mixed).astype(BF16)

    @pl.when(j == 0)
    def _():
        q_ref[0:SC_HALO, :] = jnp.zeros((SC_HALO, HALF), F32)

    q = z_ref[:, 3 * HALF:4 * HALF] * z_ref[:, 4 * HALF:5 * HALF]
    q_ref[SC_HALO:SC_HALO + tm, :] = q
    conv = scw_ref[2:3, :] * q
    conv = conv + scw_ref[1:2, :] * q_ref[SC_HALO - 1:SC_HALO - 1 + tm, :]
    conv = conv + scw_ref[0:1, :] * q_ref[SC_HALO - 2:SC_HALO - 2 + tm, :]
    y_ref[:, HALF:2 * HALF] = (z_ref[:, 2 * HALF:3 * HALF] * conv).astype(BF16)
    q_ref[0:SC_HALO, :] = q_ref[tm:tm + SC_HALO, :]

    o_ref[0] = x + _dot(y_ref[...], wout_ref[...])


def _mix_even_call(h3, norm_w, w_in, ln_g, ln_b, w_s, b_full, sc_w, w_out):
    b, s, _ = h3.shape
    tm = MIX_TM
    row_spec = pl.BlockSpec((1, tm, D_MODEL), lambda i, j: (i, j, 0))
    in_specs = [row_spec, _const_spec((1, D_MODEL)), _const_spec((D_MODEL, 5 * HALF)),
                _const_spec((1, HALF)), _const_spec((1, HALF)),
                _const_spec((N_GROUPS, CHUNK, CHUNK)), _const_spec((CHUNK, HALF)),
                _const_spec((SC_WIDTH, HALF)), _const_spec((D_MODEL, D_MODEL))]
    return pl.pallas_call(
        _mix_even_kernel,
        grid=(b, s // tm),
        in_specs=in_specs,
        out_specs=row_spec,
        out_shape=jax.ShapeDtypeStruct(h3.shape, F32),
        scratch_shapes=[pltpu.VMEM((tm, D_MODEL), BF16),
                        pltpu.VMEM((tm, 5 * HALF), F32),
                        pltpu.VMEM((tm, HALF), BF16),
                        pltpu.VMEM((SC_HALO + tm, HALF), F32),
                        pltpu.VMEM((tm, D_MODEL), BF16)],
        compiler_params=pltpu.CompilerParams(dimension_semantics=("arbitrary", "arbitrary"),
                                             vmem_limit_bytes=VMEM_LIMIT),
        name="mix_even",
    )(h3, norm_w, w_in, ln_g, ln_b, w_s, b_full, sc_w, w_out)


def _mix_odd_kernel(h_ref, nw_ref, win_ref, pw_ref, ps_ref, cvw_ref, cvb_ref, lng_ref, lnb_ref,
                    wout_ref, o_ref, hn_ref, z_ref, p_ref, sa_ref, sb_ref, c_ref, hc_ref, y_ref):
    tm = MIX_TM
    n = HALO + tm
    j = pl.program_id(1)
    x = h_ref[0]
    hn_ref[...] = _rmsnorm(x, nw_ref[...]).astype(BF16)
    z_ref[...] = _dot(hn_ref[...], win_ref[...])

    @pl.when(j == 0)
    def _():
        p_ref[0:HALO, :] = jnp.zeros((HALO, HALF), F32)
        c_ref[0:HALO, :] = jnp.zeros((HALO, HALF), F32)

    g1, g2, g3 = GROUP_W, 2 * GROUP_W, 3 * GROUP_W
    p_ref[HALO:n, :] = z_ref[:, 0:HALF]
    sa_ref[8:n, :] = p_ref[8:n, :] + p_ref[7:n - 1, :]
    sb_ref[16:n, g1:] = sa_ref[16:n, g1:] + sa_ref[14:n - 2, g1:]
    sa_ref[24:n, g2:] = sb_ref[24:n, g2:] + sb_ref[20:n - 4, g2:]
    sb_ref[32:n, g3:] = sa_ref[32:n, g3:] + sa_ref[24:n - 8, g3:]
    pos = j * tm + lax.broadcasted_iota(jnp.int32, (tm, GROUP_W), 0)
    for gi, win in enumerate(POOL_WINDOWS):
        gsl = slice(gi * GROUP_W, (gi + 1) * GROUP_W)
        s_ref = sa_ref if gi % 2 == 0 else sb_ref
        count = jnp.minimum(pos + 1, win).astype(F32)
        pooled = s_ref[HALO:n, gsl] / count - p_ref[HALO:n, gsl]
        yc = _dot(pooled.astype(BF16), pw_ref[gi]) * ps_ref[:, gsl]
        y_ref[:, gsl] = yc.astype(BF16)
    p_ref[0:HALO, :] = p_ref[tm:n, :]

    c_ref[HALO:n, :] = z_ref[:, HALF:2 * HALF] * _sigmoid(z_ref[:, 2 * HALF:3 * HALF])
    def conv_block(r, carry):
        r0 = pl.multiple_of(r * CV_ROWS, CV_ROWS)
        for lt in range(HALF // LANES):
            lsl = slice(lt * LANES, (lt + 1) * LANES)
            chunk = c_ref[pl.ds(r0, HALO + CV_ROWS), lsl]
            acc = None
            for b in range(SUBLANES):
                rolled = chunk if b == 0 else pltpu.roll(chunk, b, axis=0)
                for a in range(HALO // SUBLANES):
                    d = SUBLANES * a + b
                    if d >= CV_WIDTH:
                        continue
                    lo = HALO - SUBLANES * a
                    src = rolled[lo:lo + CV_ROWS].reshape(CV_ROWS // SUBLANES, SUBLANES, LANES)
                    term = cvw_ref[CV_WIDTH - 1 - d, :, lsl][None] * src
                    acc = term if acc is None else acc + term
            acc = acc + cvb_ref[:, lsl][None]
            hc_ref[pl.ds(r0, CV_ROWS), lsl] = acc.reshape(CV_ROWS, LANES)
        return carry

    lax.fori_loop(0, tm // CV_ROWS, conv_block, 0)
    c_ref[0:HALO, :] = c_ref[tm:n, :]
    ln = _layernorm(hc_ref[...], lng_ref[...], lnb_ref[...])
    y_ref[:, HALF:2 * HALF] = (ln * _sigmoid(ln)).astype(BF16)

    o_ref[0] = x + _dot(y_ref[...], wout_ref[...])


def _mix_odd_call(h3, norm_w, w_in, pool_w, pool_scale, cv_w, cv_b, ln_g, ln_b, w_out):
    b, s, _ = h3.shape
    tm = MIX_TM
    row_spec = pl.BlockSpec((1, tm, D_MODEL), lambda i, j: (i, j, 0))
    in_specs = [row_spec, _const_spec((1, D_MODEL)), _const_spec((D_MODEL, 3 * HALF)),
                _const_spec((N_GROUPS, GROUP_W, GROUP_W)), _const_spec((1, HALF)),
                _const_spec((CV_WIDTH, SUBLANES, HALF)), _const_spec((1, HALF)),
                _const_spec((1, HALF)), _const_spec((1, HALF)), _const_spec((D_MODEL, D_MODEL))]
    return pl.pallas_call(
        _mix_odd_kernel,
        grid=(b, s // tm),
        in_specs=in_specs,
        out_specs=row_spec,
        out_shape=jax.ShapeDtypeStruct(h3.shape, F32),
        scratch_shapes=[pltpu.VMEM((tm, D_MODEL), BF16),
                        pltpu.VMEM((tm, 3 * HALF), F32),
                        pltpu.VMEM((HALO + tm, HALF), F32),
                        pltpu.VMEM((HALO + tm, HALF), F32),
                        pltpu.VMEM((HALO + tm, HALF), F32),
                        pltpu.VMEM((HALO + tm, HALF), F32),
                        pltpu.VMEM((tm, HALF), F32),
                        pltpu.VMEM((tm, D_MODEL), BF16)],
        compiler_params=pltpu.CompilerParams(dimension_semantics=("arbitrary", "arbitrary"),
                                             vmem_limit_bytes=VMEM_LIMIT),
        name="mix_odd",
    )(h3, norm_w, w_in, pool_w, pool_scale, cv_w, cv_b, ln_g, ln_b, w_out)


def kernel(x, p, ffn1_norm, ffn1_w_gu, ffn1_w_down, mix_norm, ffn2_norm, ffn2_w_gu, ffn2_w_down,
           ple_norm, ple_w_gate, ple_w_up, ab_w_in, gm_ln_g, gm_ln_b, gm_w_s, gm_b_s, sc_w,
           ab_w_out, cd_w_in, pool_w, pool_scale, cv_w, cv_b, cv_ln_g, cv_ln_b, cd_w_out,
           final_norm):
    bsz, seq, d = x.shape
    depth = p.shape[0]
    t = bsz * seq
    row = lambda v: v.reshape(1, -1)
    h = x.reshape(t, d)
    for i in range(depth):
        j = i // 2
        h = _ffn_call(h, row(ffn1_norm[i]), ffn1_w_gu[i].astype(BF16),
                      ffn1_w_down[i].astype(BF16))
        h3 = h.reshape(bsz, seq, d)
        if i % 2 == 0:
            b_full = jnp.repeat(gm_b_s[j].T, GROUP_W, axis=1)
            h3 = _mix_even_call(h3, row(mix_norm[i]), ab_w_in[j].astype(BF16), row(gm_ln_g[j]),
                                row(gm_ln_b[j]), gm_w_s[j], b_full, sc_w[j],
                                ab_w_out[j].astype(BF16))
        else:
            h3 = _mix_odd_call(h3, row(mix_norm[i]), cd_w_in[j].astype(BF16),
                               pool_w[j].astype(BF16), row(pool_scale[j]),
                               jnp.broadcast_to(cv_w[j][:, None, :], (CV_WIDTH, SUBLANES, HALF)),
                               row(cv_b[j]),
                               row(cv_ln_g[j]), row(cv_ln_b[j]), cd_w_out[j].astype(BF16))
        h = h3.reshape(t, d)
        h = _ffn_call(h, row(ffn2_norm[i]), ffn2_w_gu[i].astype(BF16),
                      ffn2_w_down[i].astype(BF16),
                      ple=(p[i].reshape(t, PLE_DIM), row(ple_norm[i]), ple_w_gate[i].astype(BF16),
                           ple_w_up[i].astype(BF16)),
                      final_norm=row(final_norm) if i == depth - 1 else None)
    return h.reshape(bsz, seq, d)
```

```python
import functools

import jax
import jax.numpy as jnp
from jax import lax
from jax.experimental import pallas as pl
from jax.experimental.pallas import tpu as pltpu

F32 = jnp.float32
BF16 = jnp.bfloat16

D_MODEL = 1024
PLE_DIM = 256
D_FF = 2816
HALF = 512
LANES = 128
SUBLANES = 8
GROUP_W = 128
N_GROUPS = 4
CHUNK = 128
POOL_WINDOWS = (2, 4, 8, 16)
SC_WIDTH = 3
CV_WIDTH = 31
RMS_EPS = 1e-6
LN_EPS = 1e-5

FFN_TM = 512
FFN_FC = 256
FFN_NCH = D_FF // FFN_FC
MIX_TM = 512
CAST_ROWS = 256
SC_HALO = 8
HALO = 32
CV_ROWS = 64
VMEM_LIMIT = 52 * 1024 * 1024


def _rmsnorm(x, g):
    ms = jnp.mean(x * x, axis=-1, keepdims=True)
    return x * lax.rsqrt(ms + RMS_EPS) * g


def _layernorm(x, g, b):
    mu = jnp.mean(x, axis=-1, keepdims=True)
    xc = x - mu
    var = jnp.mean(xc * xc, axis=-1, keepdims=True)
    return xc * lax.rsqrt(var + LN_EPS) * g + b


def _sigmoid(x):
    return 1.0 / (1.0 + jnp.exp(-x))


def _gelu_tanh(x):
    c = 0.7978845608028654
    return 0.5 * x * (1.0 + jnp.tanh(c * (x + 0.044715 * (x * x * x))))


def _dot(a, b):
    return jnp.dot(a, b, preferred_element_type=F32)


def _cast_rows(dst_ref, src_ref):
    rows = src_ref.shape[1]
    step = min(rows, CAST_ROWS)
    for r in range(0, rows, step):
        dst_ref[r:r + step, :] = src_ref[0, r:r + step, :].astype(BF16)


def _const_spec(shape, index=None):
    index = (0,) * len(shape) if index is None else index
    return pl.BlockSpec(shape, lambda *_: index, pipeline_mode=pl.Buffered(1))


def _ffn_kernel(*refs, with_ple, with_final):
    h_ref, nw_ref, wg_blk, wu_blk, wd_blk = refs[:5]
    k = 5
    if with_ple:
        p_ref, pn_ref, wgate_ref, wup_ref = refs[k:k + 4]
        k += 4
    if with_final:
        fn_ref = refs[k]
        k += 1
    o_ref, xn_ref, a_ref, wg_scr, wu_scr, wd_scr = refs[k:k + 6]
    k += 6
    if with_ple:
        wgate_scr, wup_scr = refs[k:k + 2]
    s = pl.program_id(0)

    def normalize():
        xn_ref[...] = _rmsnorm(h_ref[...], nw_ref[0]).astype(BF16)

    def up_chunk(c):
        sl = slice(c * FFN_FC, (c + 1) * FFN_FC)
        g = _dot(xn_ref[...], wg_scr[c])
        u = _dot(xn_ref[...], wu_scr[c])
        a_ref[:, sl] = (g * _sigmoid(g) * u).astype(BF16)

    def finish():
        h1 = h_ref[...] + 0.5 * _dot(a_ref[...], wd_scr[...])
        if with_ple:
            xn_ref[...] = _rmsnorm(h1, pn_ref[0]).astype(BF16)
            gate = _sigmoid(_dot(xn_ref[...], wgate_scr[...]))
            up = _dot(p_ref[0].astype(BF16), wup_scr[...])
            h1 = h1 + gate * up
        if with_final:
            h1 = _rmsnorm(h1, fn_ref[...])
        o_ref[...] = h1

    @pl.when(s == 0)
    def _():
        normalize()
        if with_ple:
            _cast_rows(wgate_scr, wgate_ref)
            _cast_rows(wup_scr, wup_ref)

    for c in range(FFN_NCH):
        @pl.when(s == c)
        def _(c=c):
            wg_scr[c] = wg_blk[0].astype(BF16)
            wu_scr[c] = wu_blk[0].astype(BF16)
            wd_scr[c * FFN_FC:(c + 1) * FFN_FC, :] = wd_blk[0].astype(BF16)
            up_chunk(c)
            if c == FFN_NCH - 1:
                finish()

    @pl.when(s >= FFN_NCH)
    def _():
        normalize()
        for c in range(FFN_NCH):
            up_chunk(c)
        finish()


def _ffn_call(h, layer, norm_w, w_gu, w_d, ple=None, final_norm=None):
    t = h.shape[0]
    tm = FFN_TM
    last = FFN_NCH - 1
    tile = lambda s: jnp.maximum(s - last, 0)
    chunk = lambda s: jnp.minimum(s, last)
    row_spec = pl.BlockSpec((tm, D_MODEL), lambda s: (tile(s), 0))
    in_specs = [row_spec, _const_spec((1, 1, D_MODEL), (layer, 0, 0)),
                pl.BlockSpec((1, D_MODEL, FFN_FC), lambda s: (layer, 0, chunk(s))),
                pl.BlockSpec((1, D_MODEL, FFN_FC), lambda s: (layer, 0, FFN_NCH + chunk(s))),
                pl.BlockSpec((1, FFN_FC, D_MODEL), lambda s: (layer, chunk(s), 0))]
    args = [h, norm_w, w_gu, w_gu, w_d]
    scratch = [pltpu.VMEM((tm, D_MODEL), BF16),
               pltpu.VMEM((tm, D_FF), BF16),
               pltpu.VMEM((FFN_NCH, D_MODEL, FFN_FC), BF16),
               pltpu.VMEM((FFN_NCH, D_MODEL, FFN_FC), BF16),
               pltpu.VMEM((D_FF, D_MODEL), BF16)]
    if ple is not None:
        p, pn, wgate, wup = ple
        in_specs += [pl.BlockSpec((1, tm, PLE_DIM), lambda s: (layer, tile(s), 0)),
                     _const_spec((1, 1, D_MODEL), (layer, 0, 0)),
                     _const_spec((1, D_MODEL, D_MODEL), (layer, 0, 0)),
                     _const_spec((1, PLE_DIM, D_MODEL), (layer, 0, 0))]
        args += [p, pn, wgate, wup]
        scratch += [pltpu.VMEM((D_MODEL, D_MODEL), BF16), pltpu.VMEM((PLE_DIM, D_MODEL), BF16)]
    if final_norm is not None:
        in_specs.append(_const_spec((1, D_MODEL)))
        args.append(final_norm)
    return pl.pallas_call(
        functools.partial(_ffn_kernel, with_ple=ple is not None, with_final=final_norm is not None),
        grid=(last + t // tm,),
        in_specs=in_specs,
        out_specs=row_spec,
        out_shape=jax.ShapeDtypeStruct((t, D_MODEL), F32),
        scratch_shapes=scratch,
        compiler_params=pltpu.CompilerParams(dimension_semantics=("arbitrary",),
                                             vmem_limit_bytes=VMEM_LIMIT),
        name="ffn_ple" if ple is not None else "ffn",
    )(*args)


def _mix_even_kernel(h_ref, nw_ref, win_ref, lng_ref, lnb_ref, ws_ref, bs_ref, scw_ref, wout_ref,
                     o_ref, win_scr, wout_scr, wt_scr, hn_ref, z_ref, vn_ref, q_ref, y_ref):
    tm = MIX_TM
    i = pl.program_id(0)
    j = pl.program_id(1)

    @pl.when((i == 0) & (j == 0))
    def _():
        _cast_rows(win_scr, win_ref)
        _cast_rows(wout_scr, wout_ref)
        row = lax.broadcasted_iota(jnp.int32, (CHUNK, CHUNK), 0)
        col = lax.broadcasted_iota(jnp.int32, (CHUNK, CHUNK), 1)
        for g in range(N_GROUPS):
            wt_scr[g] = jnp.where(col <= row, ws_ref[0, g], 0.0).astype(BF16)

    x = h_ref[0]
    hn_ref[...] = _rmsnorm(x, nw_ref[0]).astype(BF16)
    z_ref[...] = _dot(hn_ref[...], win_scr[...])

    for g in range(N_GROUPS):
        gsl = slice(g * GROUP_W, (g + 1) * GROUP_W)
        v = _gelu_tanh(z_ref[:, HALF + g * GROUP_W:HALF + (g + 1) * GROUP_W])
        vn_ref[:, gsl] = _layernorm(v, lng_ref[0][:, gsl], lnb_ref[0][:, gsl]).astype(BF16)
    for g in range(N_GROUPS):
        gsl = slice(g * GROUP_W, (g + 1) * GROUP_W)
        for c in range(tm // CHUNK):
            rsl = slice(c * CHUNK, (c + 1) * CHUNK)
            mixed = _dot(wt_scr[g], vn_ref[rsl, gsl]) + bs_ref[:, gsl]
            u = _gelu_tanh(z_ref[rsl, gsl])
            y_ref[rsl, gsl] = (u * mixed).astype(BF16)

    @pl.when(j == 0)
    def _():
        q_ref[0:SC_HALO, :] = jnp.zeros((SC_HALO, HALF), F32)

    q = z_ref[:, 3 * HALF:4 * HALF] * z_ref[:, 4 * HALF:5 * HALF]
    q_ref[SC_HALO:SC_HALO + tm, :] = q
    conv = scw_ref[0, 2:3, :] * q
    conv = conv + scw_ref[0, 1:2, :] * q_ref[SC_HALO - 1:SC_HALO - 1 + tm, :]
    conv = conv + scw_ref[0, 0:1, :] * q_ref[SC_HALO - 2:SC_HALO - 2 + tm, :]
    y_ref[:, HALF:2 * HALF] = (z_ref[:, 2 * HALF:3 * HALF] * conv).astype(BF16)
    q_ref[0:SC_HALO, :] = q_ref[tm:tm + SC_HALO, :]

    o_ref[0] = x + _dot(y_ref[...], wout_scr[...])


def _mix_even_call(h3, layer, j, norm_w, w_in, ln_g, ln_b, w_s, b_full, sc_w, w_out):
    b, s, _ = h3.shape
    tm = MIX_TM
    row_spec = pl.BlockSpec((1, tm, D_MODEL), lambda i, t: (i, t, 0))
    in_specs = [row_spec, _const_spec((1, 1, D_MODEL), (layer, 0, 0)),
                _const_spec((1, D_MODEL, 5 * HALF), (j, 0, 0)),
                _const_spec((1, 1, HALF), (j, 0, 0)), _const_spec((1, 1, HALF), (j, 0, 0)),
                _const_spec((1, N_GROUPS, CHUNK, CHUNK), (j, 0, 0, 0)), _const_spec((CHUNK, HALF)),
                _const_spec((1, SC_WIDTH, HALF), (j, 0, 0)),
                _const_spec((1, D_MODEL, D_MODEL), (j, 0, 0))]
    return pl.pallas_call(
        _mix_even_kernel,
        grid=(b, s // tm),
        in_specs=in_specs,
        out_specs=row_spec,
        out_shape=jax.ShapeDtypeStruct(h3.shape, F32),
        scratch_shapes=[pltpu.VMEM((D_MODEL, 5 * HALF), BF16),
                        pltpu.VMEM((D_MODEL, D_MODEL), BF16),
                        pltpu.VMEM((N_GROUPS, CHUNK, CHUNK), BF16),
                        pltpu.VMEM((tm, D_MODEL), BF16),
                        pltpu.VMEM((tm, 5 * HALF), F32),
                        pltpu.VMEM((tm, HALF), BF16),
                        pltpu.VMEM((SC_HALO + tm, HALF), F32),
                        pltpu.VMEM((tm, D_MODEL), BF16)],
        compiler_params=pltpu.CompilerParams(dimension_semantics=("arbitrary", "arbitrary"),
                                             vmem_limit_bytes=VMEM_LIMIT),
        name="mix_even",
    )(h3, norm_w, w_in, ln_g, ln_b, w_s, b_full, sc_w, w_out)


def _mix_odd_kernel(h_ref, nw_ref, win_ref, pw_ref, ps_ref, cvw_ref, cvb_ref, lng_ref, lnb_ref,
                    wout_ref, o_ref, win_scr, wout_scr, pw_scr, hn_ref, z_ref, p_ref, sa_ref,
                    sb_ref, c_ref, hc_ref, y_ref):
    tm = MIX_TM
    n = HALO + tm
    i = pl.program_id(0)
    j = pl.program_id(1)

    @pl.when((i == 0) & (j == 0))
    def _():
        _cast_rows(win_scr, win_ref)
        _cast_rows(wout_scr, wout_ref)
        for g in range(N_GROUPS):
            pw_scr[g] = pw_ref[0, g].astype(BF16)

    x = h_ref[0]
    hn_ref[...] = _rmsnorm(x, nw_ref[0]).astype(BF16)
    z_ref[...] = _dot(hn_ref[...], win_scr[...])

    @pl.when(j == 0)
    def _():
        p_ref[0:HALO, :] = jnp.zeros((HALO, HALF), F32)
        c_ref[0:HALO, :] = jnp.zeros((HALO, HALF), F32)

    g1, g2, g3 = GROUP_W, 2 * GROUP_W, 3 * GROUP_W
    p_ref[HALO:n, :] = z_ref[:, 0:HALF]
    sa_ref[8:n, :] = p_ref[8:n, :] + p_ref[7:n - 1, :]
    sb_ref[16:n, g1:] = sa_ref[16:n, g1:] + sa_ref[14:n - 2, g1:]
    sa_ref[24:n, g2:] = sb_ref[24:n, g2:] + sb_ref[20:n - 4, g2:]
    sb_ref[32:n, g3:] = sa_ref[32:n, g3:] + sa_ref[24:n - 8, g3:]
    pos = j * tm + lax.broadcasted_iota(jnp.int32, (tm, GROUP_W), 0)
    for gi, win in enumerate(POOL_WINDOWS):
        gsl = slice(gi * GROUP_W, (gi + 1) * GROUP_W)
        s_ref = sa_ref if gi % 2 == 0 else sb_ref
        count = jnp.minimum(pos + 1, win).astype(F32)
        pooled = s_ref[HALO:n, gsl] / count - p_ref[HALO:n, gsl]
        yc = _dot(pooled.astype(BF16), pw_scr[gi]) * ps_ref[0][:, gsl]
        y_ref[:, gsl] = yc.astype(BF16)
    p_ref[0:HALO, :] = p_ref[tm:n, :]

    c_ref[HALO:n, :] = z_ref[:, HALF:2 * HALF] * _sigmoid(z_ref[:, 2 * HALF:3 * HALF])
    def conv_block(r, carry):
        r0 = pl.multiple_of(r * CV_ROWS, CV_ROWS)
        for lt in range(HALF // LANES):
            lsl = slice(lt * LANES, (lt + 1) * LANES)
            chunk = c_ref[pl.ds(r0, HALO + CV_ROWS), lsl]
            acc = None
            for b in range(SUBLANES):
                rolled = chunk if b == 0 else pltpu.roll(chunk, b, axis=0)
                for a in range(HALO // SUBLANES):
                    d = SUBLANES * a + b
                    if d >= CV_WIDTH:
                        continue
                    lo = HALO - SUBLANES * a
                    src = rolled[lo:lo + CV_ROWS].reshape(CV_ROWS // SUBLANES, SUBLANES, LANES)
                    term = cvw_ref[CV_WIDTH - 1 - d, :, lsl][None] * src
                    acc = term if acc is None else acc + term
            acc = acc + cvb_ref[0][:, lsl][None]
            hc_ref[pl.ds(r0, CV_ROWS), lsl] = acc.reshape(CV_ROWS, LANES)
        return carry

    lax.fori_loop(0, tm // CV_ROWS, conv_block, 0)
    c_ref[0:HALO, :] = c_ref[tm:n, :]
    ln = _layernorm(hc_ref[...], lng_ref[0], lnb_ref[0])
    y_ref[:, HALF:2 * HALF] = (ln * _sigmoid(ln)).astype(BF16)

    o_ref[0] = x + _dot(y_ref[...], wout_scr[...])


def _mix_odd_call(h3, layer, j, norm_w, w_in, pool_w, pool_scale, cv_w8, cv_b, ln_g, ln_b, w_out):
    b, s, _ = h3.shape
    tm = MIX_TM
    row_spec = pl.BlockSpec((1, tm, D_MODEL), lambda i, t: (i, t, 0))
    vec = _const_spec((1, 1, HALF), (j, 0, 0))
    in_specs = [row_spec, _const_spec((1, 1, D_MODEL), (layer, 0, 0)),
                _const_spec((1, D_MODEL, 3 * HALF), (j, 0, 0)),
                _const_spec((1, N_GROUPS, GROUP_W, GROUP_W), (j, 0, 0, 0)), vec,
                _const_spec((CV_WIDTH, SUBLANES, HALF)), vec, vec, vec,
                _const_spec((1, D_MODEL, D_MODEL), (j, 0, 0))]
    return pl.pallas_call(
        _mix_odd_kernel,
        grid=(b, s // tm),
        in_specs=in_specs,
        out_specs=row_spec,
        out_shape=jax.ShapeDtypeStruct(h3.shape, F32),
        scratch_shapes=[pltpu.VMEM((D_MODEL, 3 * HALF), BF16),
                        pltpu.VMEM((D_MODEL, D_MODEL), BF16),
                        pltpu.VMEM((N_GROUPS, GROUP_W, GROUP_W), BF16),
                        pltpu.VMEM((tm, D_MODEL), BF16),
                        pltpu.VMEM((tm, 3 * HALF), F32),
                        pltpu.VMEM((HALO + tm, HALF), F32),
                        pltpu.VMEM((HALO + tm, HALF), F32),
                        pltpu.VMEM((HALO + tm, HALF), F32),
                        pltpu.VMEM((HALO + tm, HALF), F32),
                        pltpu.VMEM((tm, HALF), F32),
                        pltpu.VMEM((tm, D_MODEL), BF16)],
        compiler_params=pltpu.CompilerParams(dimension_semantics=("arbitrary", "arbitrary"),
                                             vmem_limit_bytes=VMEM_LIMIT),
        name="mix_odd",
    )(h3, norm_w, w_in, pool_w, pool_scale, cv_w8, cv_b, ln_g, ln_b, w_out)


def kernel(x, p, ffn1_norm, ffn1_w_gu, ffn1_w_down, mix_norm, ffn2_norm, ffn2_w_gu, ffn2_w_down,
           ple_norm, ple_w_gate, ple_w_up, ab_w_in, gm_ln_g, gm_ln_b, gm_w_s, gm_b_s, sc_w,
           ab_w_out, cd_w_in, pool_w, pool_scale, cv_w, cv_b, cv_ln_g, cv_ln_b, cd_w_out,
           final_norm):
    bsz, seq, d = x.shape
    depth = p.shape[0]
    t = bsz * seq
    vec3 = lambda v: v.reshape(v.shape[0], 1, v.shape[1])
    p3 = p.reshape(depth, t, PLE_DIM)
    h = x.reshape(t, d)
    for i in range(depth):
        j = i // 2
        h = _ffn_call(h, i, vec3(ffn1_norm), ffn1_w_gu, ffn1_w_down)
        h3 = h.reshape(bsz, seq, d)
        if i % 2 == 0:
            b_full = jnp.repeat(gm_b_s[j].T, GROUP_W, axis=1)
            h3 = _mix_even_call(h3, i, j, vec3(mix_norm), ab_w_in, vec3(gm_ln_g), vec3(gm_ln_b),
                                gm_w_s, b_full, sc_w, ab_w_out)
        else:
            cv_w8 = jnp.broadcast_to(cv_w[j][:, None, :], (CV_WIDTH, SUBLANES, HALF))
            h3 = _mix_odd_call(h3, i, j, vec3(mix_norm), cd_w_in, pool_w, vec3(pool_scale), cv_w8,
                               vec3(cv_b), vec3(cv_ln_g), vec3(cv_ln_b), cd_w_out)
        h = h3.reshape(t, d)
        h = _ffn_call(h, i, vec3(ffn2_norm), ffn2_w_gu, ffn2_w_down,
                      ple=(p3, vec3(ple_norm), ple_w_gate, ple_w_up),
                      final_norm=final_norm.reshape(1, d) if i == depth - 1 else None)
    return h.reshape(bsz, seq, d)
```

```python
import functools

import jax
import jax.numpy as jnp
from jax import lax
from jax.experimental import pallas as pl
from jax.experimental.pallas import tpu as pltpu

F32 = jnp.float32
BF16 = jnp.bfloat16

D_MODEL = 1024
PLE_DIM = 256
D_FF = 2816
HALF = 512
LANES = 128
SUBLANES = 8
GROUP_W = 128
N_GROUPS = 4
CHUNK = 128
POOL_WINDOWS = (2, 4, 8, 16)
SC_WIDTH = 3
CV_WIDTH = 31
RMS_EPS = 1e-6
LN_EPS = 1e-5

FFN_TM = 512
FFN_FC = 256
FFN_NCH = D_FF // FFN_FC
MIX_TM = 512
IN_FC = 256
ODD_NCH = 3 * HALF // IN_FC
CAST_ROWS = 256
SC_HALO = 8
HALO = 32
CV_ROWS = 64
VMEM_LIMIT = 52 * 1024 * 1024


def _rmsnorm(x, g):
    ms = jnp.mean(x * x, axis=-1, keepdims=True)
    return x * lax.rsqrt(ms + RMS_EPS) * g


def _layernorm(x, g, b):
    mu = jnp.mean(x, axis=-1, keepdims=True)
    xc = x - mu
    var = jnp.mean(xc * xc, axis=-1, keepdims=True)
    return xc * lax.rsqrt(var + LN_EPS) * g + b


def _sigmoid(x):
    return 1.0 / (1.0 + jnp.exp(-x))


def _gelu_tanh(x):
    c = 0.7978845608028654
    return 0.5 * x * (1.0 + jnp.tanh(c * (x + 0.044715 * (x * x * x))))


def _dot(a, b):
    return jnp.dot(a, b, preferred_element_type=F32)


def _cast_rows(dst_ref, src_ref):
    rows = src_ref.shape[1]
    step = min(rows, CAST_ROWS)
    for r in range(0, rows, step):
        dst_ref[r:r + step, :] = src_ref[0, r:r + step, :].astype(BF16)


def _const_spec(shape, index=None):
    index = (0,) * len(shape) if index is None else index
    return pl.BlockSpec(shape, lambda *_: index, pipeline_mode=pl.Buffered(1))


def _ffn_kernel(*refs, with_ple, with_final):
    h_ref, nw_ref, wg_blk, wu_blk, wd_blk = refs[:5]
    k = 5
    if with_ple:
        p_ref, pn_ref, wgate_ref, wup_ref = refs[k:k + 4]
        k += 4
    if with_final:
        fn_ref = refs[k]
        k += 1
    o_ref, xn_ref, a_ref, wg_scr, wu_scr, wd_scr = refs[k:k + 6]
    k += 6
    if with_ple:
        wgate_scr, wup_scr = refs[k:k + 2]
    s = pl.program_id(0)

    def normalize():
        xn_ref[...] = _rmsnorm(h_ref[...], nw_ref[0]).astype(BF16)

    def up_chunk(c):
        sl = slice(c * FFN_FC, (c + 1) * FFN_FC)
        g = _dot(xn_ref[...], wg_scr[c])
        u = _dot(xn_ref[...], wu_scr[c])
        a_ref[:, sl] = (g * _sigmoid(g) * u).astype(BF16)

    def finish():
        h1 = h_ref[...] + 0.5 * _dot(a_ref[...], wd_scr[...])
        if with_ple:
            xn_ref[...] = _rmsnorm(h1, pn_ref[0]).astype(BF16)
            gate = _sigmoid(_dot(xn_ref[...], wgate_scr[...]))
            up = _dot(p_ref[0].astype(BF16), wup_scr[...])
            h1 = h1 + gate * up
        if with_final:
            h1 = _rmsnorm(h1, fn_ref[...])
        o_ref[...] = h1

    @pl.when(s == 0)
    def _():
        normalize()
        if with_ple:
            _cast_rows(wgate_scr, wgate_ref)
            _cast_rows(wup_scr, wup_ref)

    for c in range(FFN_NCH):
        @pl.when(s == c)
        def _(c=c):
            wg_scr[c] = wg_blk[0].astype(BF16)
            wu_scr[c] = wu_blk[0].astype(BF16)
            wd_scr[c * FFN_FC:(c + 1) * FFN_FC, :] = wd_blk[0].astype(BF16)
            up_chunk(c)
            if c == FFN_NCH - 1:
                finish()

    @pl.when(s >= FFN_NCH)
    def _():
        normalize()
        for c in range(FFN_NCH):
            up_chunk(c)
        finish()


def _ffn_call(h, layer, norm_w, w_gu, w_d, ple=None, final_norm=None):
    t = h.shape[0]
    tm = FFN_TM
    last = FFN_NCH - 1
    tile = lambda s: jnp.maximum(s - last, 0)
    chunk = lambda s: jnp.minimum(s, last)
    row_spec = pl.BlockSpec((tm, D_MODEL), lambda s: (tile(s), 0))
    in_specs = [row_spec, _const_spec((1, 1, D_MODEL), (layer, 0, 0)),
                pl.BlockSpec((1, D_MODEL, FFN_FC), lambda s: (layer, 0, chunk(s))),
                pl.BlockSpec((1, D_MODEL, FFN_FC), lambda s: (layer, 0, FFN_NCH + chunk(s))),
                pl.BlockSpec((1, FFN_FC, D_MODEL), lambda s: (layer, chunk(s), 0))]
    args = [h, norm_w, w_gu, w_gu, w_d]
    scratch = [pltpu.VMEM((tm, D_MODEL), BF16),
               pltpu.VMEM((tm, D_FF), BF16),
               pltpu.VMEM((FFN_NCH, D_MODEL, FFN_FC), BF16),
               pltpu.VMEM((FFN_NCH, D_MODEL, FFN_FC), BF16),
               pltpu.VMEM((D_FF, D_MODEL), BF16)]
    if ple is not None:
        p, pn, wgate, wup = ple
        in_specs += [pl.BlockSpec((1, tm, PLE_DIM), lambda s: (layer, tile(s), 0)),
                     _const_spec((1, 1, D_MODEL), (layer, 0, 0)),
                     _const_spec((1, D_MODEL, D_MODEL), (layer, 0, 0)),
                     _const_spec((1, PLE_DIM, D_MODEL), (layer, 0, 0))]
        args += [p, pn, wgate, wup]
        scratch += [pltpu.VMEM((D_MODEL, D_MODEL), BF16), pltpu.VMEM((PLE_DIM, D_MODEL), BF16)]
    if final_norm is not None:
        in_specs.append(_const_spec((1, D_MODEL)))
        args.append(final_norm)
    return pl.pallas_call(
        functools.partial(_ffn_kernel, with_ple=ple is not None, with_final=final_norm is not None),
        grid=(last + t // tm,),
        in_specs=in_specs,
        out_specs=row_spec,
        out_shape=jax.ShapeDtypeStruct((t, D_MODEL), F32),
        scratch_shapes=scratch,
        compiler_params=pltpu.CompilerParams(dimension_semantics=("arbitrary",),
                                             vmem_limit_bytes=VMEM_LIMIT),
        name="ffn_ple" if ple is not None else "ffn",
    )(*args)


def _mix_even_kernel(h_ref, nw_ref, win_ref, lng_ref, lnb_ref, ws_ref, bs_ref, scw_ref, wout_ref,
                     o_ref, win_scr, wout_scr, wt_scr, hn_ref, z_ref, vn_ref, q_ref, y_ref):
    tm = MIX_TM
    i = pl.program_id(0)
    j = pl.program_id(1)

    @pl.when((i == 0) & (j == 0))
    def _():
        _cast_rows(win_scr, win_ref)
        _cast_rows(wout_scr, wout_ref)
        row = lax.broadcasted_iota(jnp.int32, (CHUNK, CHUNK), 0)
        col = lax.broadcasted_iota(jnp.int32, (CHUNK, CHUNK), 1)
        for g in range(N_GROUPS):
            wt_scr[g] = jnp.where(col <= row, ws_ref[0, g], 0.0).astype(BF16)

    x = h_ref[0]
    hn_ref[...] = _rmsnorm(x, nw_ref[0]).astype(BF16)
    z_ref[...] = _dot(hn_ref[...], win_scr[...])

    for g in range(N_GROUPS):
        gsl = slice(g * GROUP_W, (g + 1) * GROUP_W)
        v = _gelu_tanh(z_ref[:, HALF + g * GROUP_W:HALF + (g + 1) * GROUP_W])
        vn_ref[:, gsl] = _layernorm(v, lng_ref[0][:, gsl], lnb_ref[0][:, gsl]).astype(BF16)
    for g in range(N_GROUPS):
        gsl = slice(g * GROUP_W, (g + 1) * GROUP_W)
        for c in range(tm // CHUNK):
            rsl = slice(c * CHUNK, (c + 1) * CHUNK)
            mixed = _dot(wt_scr[g], vn_ref[rsl, gsl]) + bs_ref[:, gsl]
            u = _gelu_tanh(z_ref[rsl, gsl])
            y_ref[rsl, gsl] = (u * mixed).astype(BF16)

    @pl.when(j == 0)
    def _():
        q_ref[0:SC_HALO, :] = jnp.zeros((SC_HALO, HALF), F32)

    q = z_ref[:, 3 * HALF:4 * HALF] * z_ref[:, 4 * HALF:5 * HALF]
    q_ref[SC_HALO:SC_HALO + tm, :] = q
    conv = scw_ref[0, 2:3, :] * q
    conv = conv + scw_ref[0, 1:2, :] * q_ref[SC_HALO - 1:SC_HALO - 1 + tm, :]
    conv = conv + scw_ref[0, 0:1, :] * q_ref[SC_HALO - 2:SC_HALO - 2 + tm, :]
    y_ref[:, HALF:2 * HALF] = (z_ref[:, 2 * HALF:3 * HALF] * conv).astype(BF16)
    q_ref[0:SC_HALO, :] = q_ref[tm:tm + SC_HALO, :]

    o_ref[0] = x + _dot(y_ref[...], wout_scr[...])


def _mix_even_call(h3, layer, j, norm_w, w_in, ln_g, ln_b, w_s, b_full, sc_w, w_out):
    b, s, _ = h3.shape
    tm = MIX_TM
    row_spec = pl.BlockSpec((1, tm, D_MODEL), lambda i, t: (i, t, 0))
    in_specs = [row_spec, _const_spec((1, 1, D_MODEL), (layer, 0, 0)),
                _const_spec((1, D_MODEL, 5 * HALF), (j, 0, 0)),
                _const_spec((1, 1, HALF), (j, 0, 0)), _const_spec((1, 1, HALF), (j, 0, 0)),
                _const_spec((1, N_GROUPS, CHUNK, CHUNK), (j, 0, 0, 0)), _const_spec((CHUNK, HALF)),
                _const_spec((1, SC_WIDTH, HALF), (j, 0, 0)),
                _const_spec((1, D_MODEL, D_MODEL), (j, 0, 0))]
    return pl.pallas_call(
        _mix_even_kernel,
        grid=(b, s // tm),
        in_specs=in_specs,
        out_specs=row_spec,
        out_shape=jax.ShapeDtypeStruct(h3.shape, F32),
        scratch_shapes=[pltpu.VMEM((D_MODEL, 5 * HALF), BF16),
                        pltpu.VMEM((D_MODEL, D_MODEL), BF16),
                        pltpu.VMEM((N_GROUPS, CHUNK, CHUNK), BF16),
                        pltpu.VMEM((tm, D_MODEL), BF16),
                        pltpu.VMEM((tm, 5 * HALF), F32),
                        pltpu.VMEM((tm, HALF), BF16),
                        pltpu.VMEM((SC_HALO + tm, HALF), F32),
                        pltpu.VMEM((tm, D_MODEL), BF16)],
        compiler_params=pltpu.CompilerParams(dimension_semantics=("arbitrary", "arbitrary"),
                                             vmem_limit_bytes=VMEM_LIMIT),
        name="mix_even",
    )(h3, norm_w, w_in, ln_g, ln_b, w_s, b_full, sc_w, w_out)


def _mix_odd_kernel(h_ref, hnext_ref, nw_ref, win_ref, pw_ref, ps_ref, cvw_ref, cvb_ref, lng_ref,
                    lnb_ref, wout_ref, o_ref, win_scr, wout_scr, pw_scr, hn_ref, z_scr, p_ref,
                    sa_ref, sb_ref, c_ref, hc_ref, y_ref):
    tm = MIX_TM
    n = HALO + tm
    i = pl.program_id(0)
    j = pl.program_id(1)
    cur = j % 2
    nxt = 1 - cur

    def in_proj_chunk(slot, c):
        z_scr[slot, c] = _dot(hn_ref[...], win_scr[c])

    @pl.when((i == 0) & (j == 0))
    def _():
        for c in range(ODD_NCH):
            win_scr[c] = win_ref[0, :, c * IN_FC:(c + 1) * IN_FC].astype(BF16)
        _cast_rows(wout_scr, wout_ref)
        for g in range(N_GROUPS):
            pw_scr[g] = pw_ref[0, g].astype(BF16)
        hn_ref[...] = _rmsnorm(h_ref[0], nw_ref[0]).astype(BF16)
        for c in range(ODD_NCH):
            in_proj_chunk(0, c)

    hn_ref[...] = _rmsnorm(hnext_ref[0], nw_ref[0]).astype(BF16)

    @pl.when(j == 0)
    def _():
        p_ref[0:HALO, :] = jnp.zeros((HALO, HALF), F32)
        c_ref[0:HALO, :] = jnp.zeros((HALO, HALF), F32)

    per_half = HALF // IN_FC
    for c in range(per_half):
        csl = slice(c * IN_FC, (c + 1) * IN_FC)
        p_ref[HALO:n, csl] = z_scr[cur, c]
        c_ref[HALO:n, csl] = z_scr[cur, per_half + c] * _sigmoid(z_scr[cur, 2 * per_half + c])

    g1, g2, g3 = GROUP_W, 2 * GROUP_W, 3 * GROUP_W
    sa_ref[8:n, :] = p_ref[8:n, :] + p_ref[7:n - 1, :]
    sb_ref[16:n, g1:] = sa_ref[16:n, g1:] + sa_ref[14:n - 2, g1:]
    sa_ref[24:n, g2:] = sb_ref[24:n, g2:] + sb_ref[20:n - 4, g2:]
    sb_ref[32:n, g3:] = sa_ref[32:n, g3:] + sa_ref[24:n - 8, g3:]
    pos = j * tm + lax.broadcasted_iota(jnp.int32, (tm, GROUP_W), 0)
    for gi, win in enumerate(POOL_WINDOWS):
        gsl = slice(gi * GROUP_W, (gi + 1) * GROUP_W)
        s_ref = sa_ref if gi % 2 == 0 else sb_ref
        count = jnp.minimum(pos + 1, win).astype(F32)
        pooled = s_ref[HALO:n, gsl] / count - p_ref[HALO:n, gsl]
        yc = _dot(pooled.astype(BF16), pw_scr[gi]) * ps_ref[0][:, gsl]
        y_ref[:, gsl] = yc.astype(BF16)
    p_ref[0:HALO, :] = p_ref[tm:n, :]

    def conv_block(r):
        r0 = pl.multiple_of(r * CV_ROWS, CV_ROWS)
        for lt in range(HALF // LANES):
            lsl = slice(lt * LANES, (lt + 1) * LANES)
            chunk = c_ref[pl.ds(r0, HALO + CV_ROWS), lsl]
            acc = None
            for b in range(SUBLANES):
                rolled = chunk if b == 0 else pltpu.roll(chunk, b, axis=0)
                for a in range(HALO // SUBLANES):
                    d = SUBLANES * a + b
                    if d >= CV_WIDTH:
                        continue
                    lo = HALO - SUBLANES * a
                    src = rolled[lo:lo + CV_ROWS].reshape(CV_ROWS // SUBLANES, SUBLANES, LANES)
                    term = cvw_ref[CV_WIDTH - 1 - d, :, lsl][None] * src
                    acc = term if acc is None else acc + term
            acc = acc + cvb_ref[0][:, lsl][None]
            hc_ref[pl.ds(r0, CV_ROWS), lsl] = acc.reshape(CV_ROWS, LANES)

    def conv_and_proj(r, carry):
        conv_block(r)
        in_proj_chunk(nxt, r)
        return carry

    def conv_only(r, carry):
        conv_block(r)
        return carry

    lax.fori_loop(0, ODD_NCH, conv_and_proj, 0)
    lax.fori_loop(ODD_NCH, tm // CV_ROWS, conv_only, 0)
    c_ref[0:HALO, :] = c_ref[tm:n, :]
    ln = _layernorm(hc_ref[...], lng_ref[0], lnb_ref[0])
    y_ref[:, HALF:2 * HALF] = (ln * _sigmoid(ln)).astype(BF16)

    o_ref[0] = h_ref[0] + _dot(y_ref[...], wout_scr[...])


def _mix_odd_call(h3, layer, j, norm_w, w_in, pool_w, pool_scale, cv_w8, cv_b, ln_g, ln_b, w_out):
    b, s, _ = h3.shape
    tm = MIX_TM
    tiles = s // tm
    assert tiles % 2 == 0 and ODD_NCH <= tm // CV_ROWS
    row_spec = pl.BlockSpec((1, tm, D_MODEL), lambda i, t: (i, t, 0))

    def next_tile(i, t):
        flat = jnp.minimum(i * tiles + t + 1, b * tiles - 1)
        return (flat // tiles, flat % tiles, 0)

    vec = _const_spec((1, 1, HALF), (j, 0, 0))
    in_specs = [row_spec, pl.BlockSpec((1, tm, D_MODEL), next_tile),
                _const_spec((1, 1, D_MODEL), (layer, 0, 0)),
                _const_spec((1, D_MODEL, 3 * HALF), (j, 0, 0)),
                _const_spec((1, N_GROUPS, GROUP_W, GROUP_W), (j, 0, 0, 0)), vec,
                _const_spec((CV_WIDTH, SUBLANES, HALF)), vec, vec, vec,
                _const_spec((1, D_MODEL, D_MODEL), (j, 0, 0))]
    return pl.pallas_call(
        _mix_odd_kernel,
        grid=(b, tiles),
        in_specs=in_specs,
        out_specs=row_spec,
        out_shape=jax.ShapeDtypeStruct(h3.shape, F32),
        scratch_shapes=[pltpu.VMEM((ODD_NCH, D_MODEL, IN_FC), BF16),
                        pltpu.VMEM((D_MODEL, D_MODEL), BF16),
                        pltpu.VMEM((N_GROUPS, GROUP_W, GROUP_W), BF16),
                        pltpu.VMEM((tm, D_MODEL), BF16),
                        pltpu.VMEM((2, ODD_NCH, tm, IN_FC), F32),
                        pltpu.VMEM((HALO + tm, HALF), F32),
                        pltpu.VMEM((HALO + tm, HALF), F32),
                        pltpu.VMEM((HALO + tm, HALF), F32),
                        pltpu.VMEM((HALO + tm, HALF), F32),
                        pltpu.VMEM((tm, HALF), F32),
                        pltpu.VMEM((tm, D_MODEL), BF16)],
        compiler_params=pltpu.CompilerParams(dimension_semantics=("arbitrary", "arbitrary"),
                                             vmem_limit_bytes=VMEM_LIMIT),
        name="mix_odd",
    )(h3, h3, norm_w, w_in, pool_w, pool_scale, cv_w8, cv_b, ln_g, ln_b, w_out)


def kernel(x, p, ffn1_norm, ffn1_w_gu, ffn1_w_down, mix_norm, ffn2_norm, ffn2_w_gu, ffn2_w_down,
           ple_norm, ple_w_gate, ple_w_up, ab_w_in, gm_ln_g, gm_ln_b, gm_w_s, gm_b_s, sc_w,
           ab_w_out, cd_w_in, pool_w, pool_scale, cv_w, cv_b, cv_ln_g, cv_ln_b, cd_w_out,
           final_norm):
    bsz, seq, d = x.shape
    depth = p.shape[0]
    t = bsz * seq
    vec3 = lambda v: v.reshape(v.shape[0], 1, v.shape[1])
    p3 = p.reshape(depth, t, PLE_DIM)
    h = x.reshape(t, d)
    for i in range(depth):
        j = i // 2
        h = _ffn_call(h, i, vec3(ffn1_norm), ffn1_w_gu, ffn1_w_down)
        h3 = h.reshape(bsz, seq, d)
        if i % 2 == 0:
            b_full = jnp.repeat(gm_b_s[j].T, GROUP_W, axis=1)
            h3 = _mix_even_call(h3, i, j, vec3(mix_norm), ab_w_in, vec3(gm_ln_g), vec3(gm_ln_b),
                                gm_w_s, b_full, sc_w, ab_w_out)
        else:
            cv_w8 = jnp.broadcast_to(cv_w[j][:, None, :], (CV_WIDTH, SUBLANES, HALF))
            h3 = _mix_odd_call(h3, i, j, vec3(mix_norm), cd_w_in, pool_w, vec3(pool_scale), cv_w8,
                               vec3(cv_b), vec3(cv_ln_g), vec3(cv_ln_b), cd_w_out)
        h = h3.reshape(t, d)
        h = _ffn_call(h, i, vec3(ffn2_norm), ffn2_w_gu, ffn2_w_down,
                      ple=(p3, vec3(ple_norm), ple_w_gate, ple_w_up),
                      final_norm=final_norm.reshape(1, d) if i == depth - 1 else None)
    return h.reshape(bsz, seq, d)
```

```python
import functools

import jax
import jax.numpy as jnp
from jax import lax
from jax.experimental import pallas as pl
from jax.experimental.pallas import tpu as pltpu

F32 = jnp.float32
BF16 = jnp.bfloat16

D_MODEL = 1024
PLE_DIM = 256
D_FF = 2816
HALF = 512
LANES = 128
SUBLANES = 8
GROUP_W = 128
N_GROUPS = 4
CHUNK = 128
POOL_WINDOWS = (2, 4, 8, 16)
SC_WIDTH = 3
CV_WIDTH = 31
RMS_EPS = 1e-6
LN_EPS = 1e-5

FFN_TM = 512
FFN_FC = 256
FFN_NCH = D_FF // FFN_FC
MIX_TM = 512
GLU_FC = 256
CAST_ROWS = 256
SC_HALO = 8
HALO = 32
CV_ROWS = 64
VMEM_LIMIT = 52 * 1024 * 1024


def _rmsnorm(x, g):
    ms = jnp.mean(x * x, axis=-1, keepdims=True)
    return x * lax.rsqrt(ms + RMS_EPS) * g


def _layernorm(x, g, b):
    mu = jnp.mean(x, axis=-1, keepdims=True)
    xc = x - mu
    var = jnp.mean(xc * xc, axis=-1, keepdims=True)
    return xc * lax.rsqrt(var + LN_EPS) * g + b


def _sigmoid(x):
    return 0.5 + 0.5 * jnp.tanh(0.5 * x)


def _silu(x):
    hx = 0.5 * x
    return hx + hx * jnp.tanh(hx)


def _gelu_tanh(x):
    c = 0.7978845608028654
    return 0.5 * x * (1.0 + jnp.tanh(c * (x + 0.044715 * (x * x * x))))


def _dot(a, b):
    return jnp.dot(a, b, preferred_element_type=F32)


def _cast_rows(dst_ref, src_ref):
    rows = src_ref.shape[1]
    step = min(rows, CAST_ROWS)
    for r in range(0, rows, step):
        dst_ref[r:r + step, :] = src_ref[0, r:r + step, :].astype(BF16)


def _const_spec(shape, index=None):
    index = (0,) * len(shape) if index is None else index
    return pl.BlockSpec(shape, lambda *_: index, pipeline_mode=pl.Buffered(1))


def _ffn_kernel(*refs, with_ple, with_final):
    h_ref, nw_ref, wg_blk, wu_blk, wd_blk = refs[:5]
    k = 5
    if with_ple:
        p_ref, pn_ref, wgate_ref, wup_ref = refs[k:k + 4]
        k += 4
    if with_final:
        fn_ref = refs[k]
        k += 1
    o_ref, xn_ref, a_ref, wg_scr, wu_scr, wd_scr = refs[k:k + 6]
    k += 6
    if with_ple:
        wgate_scr, wup_scr = refs[k:k + 2]
    s = pl.program_id(0)

    def normalize():
        xn_ref[...] = _rmsnorm(h_ref[...], nw_ref[0]).astype(BF16)

    def up_chunk(c):
        sl = slice(c * FFN_FC, (c + 1) * FFN_FC)
        g = _dot(xn_ref[...], wg_scr[c])
        u = _dot(xn_ref[...], wu_scr[c])
        a_ref[:, sl] = (_silu(g) * u).astype(BF16)

    def finish():
        h1 = h_ref[...] + 0.5 * _dot(a_ref[...], wd_scr[...])
        if with_ple:
            xn_ref[...] = _rmsnorm(h1, pn_ref[0]).astype(BF16)
            gate = _sigmoid(_dot(xn_ref[...], wgate_scr[...]))
            up = _dot(p_ref[0].astype(BF16), wup_scr[...])
            h1 = h1 + gate * up
        if with_final:
            h1 = _rmsnorm(h1, fn_ref[...])
        o_ref[...] = h1

    @pl.when(s == 0)
    def _():
        normalize()
        if with_ple:
            _cast_rows(wgate_scr, wgate_ref)
            _cast_rows(wup_scr, wup_ref)

    for c in range(FFN_NCH):
        @pl.when(s == c)
        def _(c=c):
            wg_scr[c] = wg_blk[0].astype(BF16)
            wu_scr[c] = wu_blk[0].astype(BF16)
            wd_scr[c * FFN_FC:(c + 1) * FFN_FC, :] = wd_blk[0].astype(BF16)
            up_chunk(c)
            if c == FFN_NCH - 1:
                finish()

    @pl.when(s >= FFN_NCH)
    def _():
        normalize()
        for c in range(FFN_NCH):
            up_chunk(c)
        finish()


def _ffn_call(h, layer, norm_w, w_gu, w_d, ple=None, final_norm=None):
    t = h.shape[0]
    tm = FFN_TM
    last = FFN_NCH - 1
    tile = lambda s: jnp.maximum(s - last, 0)
    chunk = lambda s: jnp.minimum(s, last)
    row_spec = pl.BlockSpec((tm, D_MODEL), lambda s: (tile(s), 0))
    in_specs = [row_spec, _const_spec((1, 1, D_MODEL), (layer, 0, 0)),
                pl.BlockSpec((1, D_MODEL, FFN_FC), lambda s: (layer, 0, chunk(s))),
                pl.BlockSpec((1, D_MODEL, FFN_FC), lambda s: (layer, 0, FFN_NCH + chunk(s))),
                pl.BlockSpec((1, FFN_FC, D_MODEL), lambda s: (layer, chunk(s), 0))]
    args = [h, norm_w, w_gu, w_gu, w_d]
    scratch = [pltpu.VMEM((tm, D_MODEL), BF16),
               pltpu.VMEM((tm, D_FF), BF16),
               pltpu.VMEM((FFN_NCH, D_MODEL, FFN_FC), BF16),
               pltpu.VMEM((FFN_NCH, D_MODEL, FFN_FC), BF16),
               pltpu.VMEM((D_FF, D_MODEL), BF16)]
    if ple is not None:
        p, pn, wgate, wup = ple
        in_specs += [pl.BlockSpec((1, tm, PLE_DIM), lambda s: (layer, tile(s), 0)),
                     _const_spec((1, 1, D_MODEL), (layer, 0, 0)),
                     _const_spec((1, D_MODEL, D_MODEL), (layer, 0, 0)),
                     _const_spec((1, PLE_DIM, D_MODEL), (layer, 0, 0))]
        args += [p, pn, wgate, wup]
        scratch += [pltpu.VMEM((D_MODEL, D_MODEL), BF16), pltpu.VMEM((PLE_DIM, D_MODEL), BF16)]
    if final_norm is not None:
        in_specs.append(_const_spec((1, D_MODEL)))
        args.append(final_norm)
    return pl.pallas_call(
        functools.partial(_ffn_kernel, with_ple=ple is not None, with_final=final_norm is not None),
        grid=(last + t // tm,),
        in_specs=in_specs,
        out_specs=row_spec,
        out_shape=jax.ShapeDtypeStruct((t, D_MODEL), F32),
        scratch_shapes=scratch,
        compiler_params=pltpu.CompilerParams(dimension_semantics=("arbitrary",),
                                             vmem_limit_bytes=VMEM_LIMIT),
        name="ffn_ple" if ple is not None else "ffn",
    )(*args)


def _mix_even_kernel(h_ref, nw_ref, win_ref, lng_ref, lnb_ref, ws_ref, bs_ref, scw_ref, wout_ref,
                     o_ref, win_scr, wout_scr, wt_scr, hn_ref, z_ref, vn_ref, q_ref, y_ref):
    tm = MIX_TM
    i = pl.program_id(0)
    j = pl.program_id(1)

    @pl.when((i == 0) & (j == 0))
    def _():
        _cast_rows(win_scr, win_ref)
        _cast_rows(wout_scr, wout_ref)
        row = lax.broadcasted_iota(jnp.int32, (CHUNK, CHUNK), 0)
        col = lax.broadcasted_iota(jnp.int32, (CHUNK, CHUNK), 1)
        for g in range(N_GROUPS):
            wt_scr[g] = jnp.where(col <= row, ws_ref[0, g], 0.0).astype(BF16)

    x = h_ref[0]
    hn_ref[...] = _rmsnorm(x, nw_ref[0]).astype(BF16)
    z_ref[...] = _dot(hn_ref[...], win_scr[...])

    for g in range(N_GROUPS):
        gsl = slice(g * GROUP_W, (g + 1) * GROUP_W)
        v = _gelu_tanh(z_ref[:, HALF + g * GROUP_W:HALF + (g + 1) * GROUP_W])
        vn_ref[:, gsl] = _layernorm(v, lng_ref[0][:, gsl], lnb_ref[0][:, gsl]).astype(BF16)
    for g in range(N_GROUPS):
        gsl = slice(g * GROUP_W, (g + 1) * GROUP_W)
        for c in range(tm // CHUNK):
            rsl = slice(c * CHUNK, (c + 1) * CHUNK)
            mixed = _dot(wt_scr[g], vn_ref[rsl, gsl]) + bs_ref[:, gsl]
            u = _gelu_tanh(z_ref[rsl, gsl])
            y_ref[rsl, gsl] = (u * mixed).astype(BF16)

    @pl.when(j == 0)
    def _():
        q_ref[0:SC_HALO, :] = jnp.zeros((SC_HALO, HALF), F32)

    q = z_ref[:, 3 * HALF:4 * HALF] * z_ref[:, 4 * HALF:5 * HALF]
    q_ref[SC_HALO:SC_HALO + tm, :] = q
    conv = scw_ref[0, 2:3, :] * q
    conv = conv + scw_ref[0, 1:2, :] * q_ref[SC_HALO - 1:SC_HALO - 1 + tm, :]
    conv = conv + scw_ref[0, 0:1, :] * q_ref[SC_HALO - 2:SC_HALO - 2 + tm, :]
    y_ref[:, HALF:2 * HALF] = (z_ref[:, 2 * HALF:3 * HALF] * conv).astype(BF16)
    q_ref[0:SC_HALO, :] = q_ref[tm:tm + SC_HALO, :]

    o_ref[0] = x + _dot(y_ref[...], wout_scr[...])


def _mix_even_call(h3, layer, j, norm_w, w_in, ln_g, ln_b, w_s, b_full, sc_w, w_out):
    b, s, _ = h3.shape
    tm = MIX_TM
    row_spec = pl.BlockSpec((1, tm, D_MODEL), lambda i, t: (i, t, 0))
    in_specs = [row_spec, _const_spec((1, 1, D_MODEL), (layer, 0, 0)),
                _const_spec((1, D_MODEL, 5 * HALF), (j, 0, 0)),
                _const_spec((1, 1, HALF), (j, 0, 0)), _const_spec((1, 1, HALF), (j, 0, 0)),
                _const_spec((1, N_GROUPS, CHUNK, CHUNK), (j, 0, 0, 0)), _const_spec((CHUNK, HALF)),
                _const_spec((1, SC_WIDTH, HALF), (j, 0, 0)),
                _const_spec((1, D_MODEL, D_MODEL), (j, 0, 0))]
    return pl.pallas_call(
        _mix_even_kernel,
        grid=(b, s // tm),
        in_specs=in_specs,
        out_specs=row_spec,
        out_shape=jax.ShapeDtypeStruct(h3.shape, F32),
        scratch_shapes=[pltpu.VMEM((D_MODEL, 5 * HALF), BF16),
                        pltpu.VMEM((D_MODEL, D_MODEL), BF16),
                        pltpu.VMEM((N_GROUPS, CHUNK, CHUNK), BF16),
                        pltpu.VMEM((tm, D_MODEL), BF16),
                        pltpu.VMEM((tm, 5 * HALF), F32),
                        pltpu.VMEM((tm, HALF), BF16),
                        pltpu.VMEM((SC_HALO + tm, HALF), F32),
                        pltpu.VMEM((tm, D_MODEL), BF16)],
        compiler_params=pltpu.CompilerParams(dimension_semantics=("arbitrary", "arbitrary"),
                                             vmem_limit_bytes=VMEM_LIMIT),
        name="mix_even",
    )(h3, norm_w, w_in, ln_g, ln_b, w_s, b_full, sc_w, w_out)


def _mix_odd_kernel(h_ref, nw_ref, win_ref, pw_ref, ps_ref, cvw_ref, cvb_ref, lng_ref, lnb_ref,
                    wout_ref, o_ref, win_scr, wout_scr, pw_scr, hn_ref, p_ref, sa_ref,
                    sb_ref, c_ref, hc_ref, y_ref):
    tm = MIX_TM
    n = HALO + tm
    i = pl.program_id(0)
    j = pl.program_id(1)

    @pl.when((i == 0) & (j == 0))
    def _():
        _cast_rows(win_scr, win_ref)
        _cast_rows(wout_scr, wout_ref)
        for g in range(N_GROUPS):
            pw_scr[g] = pw_ref[0, g].astype(BF16)

    @pl.when(j == 0)
    def _():
        p_ref[0:HALO, :] = jnp.zeros((HALO, HALF), F32)
        c_ref[0:HALO, :] = jnp.zeros((HALO, HALF), F32)

    x = h_ref[0]
    hn_ref[...] = _rmsnorm(x, nw_ref[0]).astype(BF16)
    p_ref[HALO:n, :] = _dot(hn_ref[...], win_scr[:, 0:HALF])
    for c in range(HALF // GLU_FC):
        csl = slice(c * GLU_FC, (c + 1) * GLU_FC)
        a = _dot(hn_ref[...], win_scr[:, HALF + c * GLU_FC:HALF + (c + 1) * GLU_FC])
        g = _dot(hn_ref[...], win_scr[:, 2 * HALF + c * GLU_FC:2 * HALF + (c + 1) * GLU_FC])
        c_ref[HALO:n, csl] = a * _sigmoid(g)

    g1, g2, g3 = GROUP_W, 2 * GROUP_W, 3 * GROUP_W
    sa_ref[8:n, :] = p_ref[8:n, :] + p_ref[7:n - 1, :]
    sb_ref[16:n, g1:] = sa_ref[16:n, g1:] + sa_ref[14:n - 2, g1:]
    sa_ref[24:n, g2:] = sb_ref[24:n, g2:] + sb_ref[20:n - 4, g2:]
    sb_ref[32:n, g3:] = sa_ref[32:n, g3:] + sa_ref[24:n - 8, g3:]
    pos = j * tm + lax.broadcasted_iota(jnp.int32, (tm, GROUP_W), 0)
    for gi, win in enumerate(POOL_WINDOWS):
        gsl = slice(gi * GROUP_W, (gi + 1) * GROUP_W)
        s_ref = sa_ref if gi % 2 == 0 else sb_ref
        count = jnp.minimum(pos + 1, win).astype(F32)
        pooled = s_ref[HALO:n, gsl] / count - p_ref[HALO:n, gsl]
        yc = _dot(pooled.astype(BF16), pw_scr[gi]) * ps_ref[0][:, gsl]
        y_ref[:, gsl] = yc.astype(BF16)
    p_ref[0:HALO, :] = p_ref[tm:n, :]

    def conv_block(r, carry):
        r0 = pl.multiple_of(r * CV_ROWS, CV_ROWS)
        for lt in range(HALF // LANES):
            lsl = slice(lt * LANES, (lt + 1) * LANES)
            chunk = c_ref[pl.ds(r0, HALO + CV_ROWS), lsl]
            acc = None
            for b in range(SUBLANES):
                rolled = chunk if b == 0 else pltpu.roll(chunk, b, axis=0)
                for a in range(HALO // SUBLANES):
                    d = SUBLANES * a + b
                    if d >= CV_WIDTH:
                        continue
                    lo = HALO - SUBLANES * a
                    src = rolled[lo:lo + CV_ROWS].reshape(CV_ROWS // SUBLANES, SUBLANES, LANES)
                    term = cvw_ref[CV_WIDTH - 1 - d, :, lsl][None] * src
                    acc = term if acc is None else acc + term
            acc = acc + cvb_ref[0][:, lsl][None]
            hc_ref[pl.ds(r0, CV_ROWS), lsl] = acc.reshape(CV_ROWS, LANES)
        return carry

    lax.fori_loop(0, tm // CV_ROWS, conv_block, 0)
    c_ref[0:HALO, :] = c_ref[tm:n, :]
    ln = _layernorm(hc_ref[...], lng_ref[0], lnb_ref[0])
    y_ref[:, HALF:2 * HALF] = _silu(ln).astype(BF16)

    o_ref[0] = x + _dot(y_ref[...], wout_scr[...])


def _mix_odd_call(h3, layer, j, norm_w, w_in, pool_w, pool_scale, cv_w8, cv_b, ln_g, ln_b, w_out):
    b, s, _ = h3.shape
    tm = MIX_TM
    tiles = s // tm
    row_spec = pl.BlockSpec((1, tm, D_MODEL), lambda i, t: (i, t, 0))
    vec = _const_spec((1, 1, HALF), (j, 0, 0))
    in_specs = [row_spec, _const_spec((1, 1, D_MODEL), (layer, 0, 0)),
                _const_spec((1, D_MODEL, 3 * HALF), (j, 0, 0)),
                _const_spec((1, N_GROUPS, GROUP_W, GROUP_W), (j, 0, 0, 0)), vec,
                _const_spec((CV_WIDTH, SUBLANES, HALF)), vec, vec, vec,
                _const_spec((1, D_MODEL, D_MODEL), (j, 0, 0))]
    return pl.pallas_call(
        _mix_odd_kernel,
        grid=(b, tiles),
        in_specs=in_specs,
        out_specs=row_spec,
        out_shape=jax.ShapeDtypeStruct(h3.shape, F32),
        scratch_shapes=[pltpu.VMEM((D_MODEL, 3 * HALF), BF16),
                        pltpu.VMEM((D_MODEL, D_MODEL), BF16),
                        pltpu.VMEM((N_GROUPS, GROUP_W, GROUP_W), BF16),
                        pltpu.VMEM((tm, D_MODEL), BF16),
                        pltpu.VMEM((HALO + tm, HALF), F32),
                        pltpu.VMEM((HALO + tm, HALF), F32),
                        pltpu.VMEM((HALO + tm, HALF), F32),
                        pltpu.VMEM((HALO + tm, HALF), F32),
                        pltpu.VMEM((tm, HALF), F32),
                        pltpu.VMEM((tm, D_MODEL), BF16)],
        compiler_params=pltpu.CompilerParams(dimension_semantics=("arbitrary", "arbitrary"),
                                             vmem_limit_bytes=VMEM_LIMIT),
        name="mix_odd",
    )(h3, norm_w, w_in, pool_w, pool_scale, cv_w8, cv_b, ln_g, ln_b, w_out)


def kernel(x, p, ffn1_norm, ffn1_w_gu, ffn1_w_down, mix_norm, ffn2_norm, ffn2_w_gu, ffn2_w_down,
           ple_norm, ple_w_gate, ple_w_up, ab_w_in, gm_ln_g, gm_ln_b, gm_w_s, gm_b_s, sc_w,
           ab_w_out, cd_w_in, pool_w, pool_scale, cv_w, cv_b, cv_ln_g, cv_ln_b, cd_w_out,
           final_norm):
    bsz, seq, d = x.shape
    depth = p.shape[0]
    t = bsz * seq
    vec3 = lambda v: v.reshape(v.shape[0], 1, v.shape[1])
    p3 = p.reshape(depth, t, PLE_DIM)
    h = x.reshape(t, d)
    for i in range(depth):
        j = i // 2
        h = _ffn_call(h, i, vec3(ffn1_norm), ffn1_w_gu, ffn1_w_down)
        h3 = h.reshape(bsz, seq, d)
        if i % 2 == 0:
            b_full = jnp.repeat(gm_b_s[j].T, GROUP_W, axis=1)
            h3 = _mix_even_call(h3, i, j, vec3(mix_norm), ab_w_in, vec3(gm_ln_g), vec3(gm_ln_b),
                                gm_w_s, b_full, sc_w, ab_w_out)
        else:
            cv_w8 = jnp.broadcast_to(cv_w[j][:, None, :], (CV_WIDTH, SUBLANES, HALF))
            h3 = _mix_odd_call(h3, i, j, vec3(mix_norm), cd_w_in, pool_w, vec3(pool_scale), cv_w8,
                               vec3(cv_b), vec3(cv_ln_g), vec3(cv_ln_b), cd_w_out)
        h = h3.reshape(t, d)
        h = _ffn_call(h, i, vec3(ffn2_norm), ffn2_w_gu, ffn2_w_down,
                      ple=(p3, vec3(ple_norm), ple_w_gate, ple_w_up),
                      final_norm=final_norm.reshape(1, d) if i == depth - 1 else None)
    return h.reshape(bsz, seq, d)
```

```python
import functools

import jax
import jax.numpy as jnp
from jax import lax
from jax.experimental import pallas as pl
from jax.experimental.pallas import tpu as pltpu

F32 = jnp.float32
BF16 = jnp.bfloat16

D_MODEL = 1024
PLE_DIM = 256
D_FF = 2816
HALF = 512
LANES = 128
SUBLANES = 8
GROUP_W = 128
N_GROUPS = 4
CHUNK = 128
POOL_WINDOWS = (2, 4, 8, 16)
SC_WIDTH = 3
CV_WIDTH = 31
RMS_EPS = 1e-6
LN_EPS = 1e-5

FFN_TM = 512
FFN_TM_PLAIN = 1024
FFN_FC = 256
FFN_NCH = D_FF // FFN_FC
MIX_TM = 512
MIX_TM_ODD = 1024
GLU_FC = 256
CAST_ROWS = 256
SC_HALO = 8
HALO = 32
CV_ROWS = 64
VMEM_LIMIT = 52 * 1024 * 1024


def _rmsnorm(x, g):
    ms = jnp.mean(x * x, axis=-1, keepdims=True)
    return x * lax.rsqrt(ms + RMS_EPS) * g


def _layernorm(x, g, b):
    mu = jnp.mean(x, axis=-1, keepdims=True)
    xc = x - mu
    var = jnp.mean(xc * xc, axis=-1, keepdims=True)
    return xc * lax.rsqrt(var + LN_EPS) * g + b


def _sigmoid(x):
    return 0.5 + 0.5 * jnp.tanh(0.5 * x)


def _silu(x):
    hx = 0.5 * x
    return hx + hx * jnp.tanh(hx)


def _gelu_tanh(x):
    c = 0.7978845608028654
    return 0.5 * x * (1.0 + jnp.tanh(c * (x + 0.044715 * (x * x * x))))


def _dot(a, b):
    return jnp.dot(a, b, preferred_element_type=F32)


def _cast_rows(dst_ref, src_ref):
    rows = src_ref.shape[1]
    step = min(rows, CAST_ROWS)
    for r in range(0, rows, step):
        dst_ref[r:r + step, :] = src_ref[0, r:r + step, :].astype(BF16)


def _const_spec(shape, index=None):
    index = (0,) * len(shape) if index is None else index
    return pl.BlockSpec(shape, lambda *_: index, pipeline_mode=pl.Buffered(1))


def _ffn_kernel(*refs, with_ple, with_final):
    h_ref, nw_ref, wg_blk, wu_blk, wd_blk = refs[:5]
    k = 5
    if with_ple:
        p_ref, pn_ref, wgate_ref, wup_ref = refs[k:k + 4]
        k += 4
    if with_final:
        fn_ref = refs[k]
        k += 1
    o_ref, xn_ref, a_ref, wg_scr, wu_scr, wd_scr = refs[k:k + 6]
    k += 6
    if with_ple:
        wgate_scr, wup_scr = refs[k:k + 2]
    s = pl.program_id(0)

    def normalize():
        xn_ref[...] = _rmsnorm(h_ref[...], nw_ref[0]).astype(BF16)

    def up_chunk(c):
        sl = slice(c * FFN_FC, (c + 1) * FFN_FC)
        g = _dot(xn_ref[...], wg_scr[c])
        u = _dot(xn_ref[...], wu_scr[c])
        a_ref[:, sl] = (_silu(g) * u).astype(BF16)

    def finish():
        h1 = h_ref[...] + 0.5 * _dot(a_ref[...], wd_scr[...])
        if with_ple:
            xn_ref[...] = _rmsnorm(h1, pn_ref[0]).astype(BF16)
            gate = _sigmoid(_dot(xn_ref[...], wgate_scr[...]))
            up = _dot(p_ref[0].astype(BF16), wup_scr[...])
            h1 = h1 + gate * up
        if with_final:
            h1 = _rmsnorm(h1, fn_ref[...])
        o_ref[...] = h1

    @pl.when(s == 0)
    def _():
        normalize()
        if with_ple:
            _cast_rows(wgate_scr, wgate_ref)
            _cast_rows(wup_scr, wup_ref)

    for c in range(FFN_NCH):
        @pl.when(s == c)
        def _(c=c):
            wg_scr[c] = wg_blk[0].astype(BF16)
            wu_scr[c] = wu_blk[0].astype(BF16)
            wd_scr[c * FFN_FC:(c + 1) * FFN_FC, :] = wd_blk[0].astype(BF16)
            up_chunk(c)
            if c == FFN_NCH - 1:
                finish()

    @pl.when(s >= FFN_NCH)
    def _():
        normalize()
        for c in range(FFN_NCH):
            up_chunk(c)
        finish()


def _ffn_call(h, layer, norm_w, w_gu, w_d, ple=None, final_norm=None):
    t = h.shape[0]
    tm = FFN_TM if ple is not None else FFN_TM_PLAIN
    last = FFN_NCH - 1
    tile = lambda s: jnp.maximum(s - last, 0)
    chunk = lambda s: jnp.minimum(s, last)
    row_spec = pl.BlockSpec((tm, D_MODEL), lambda s: (tile(s), 0))
    in_specs = [row_spec, _const_spec((1, 1, D_MODEL), (layer, 0, 0)),
                pl.BlockSpec((1, D_MODEL, FFN_FC), lambda s: (layer, 0, chunk(s))),
                pl.BlockSpec((1, D_MODEL, FFN_FC), lambda s: (layer, 0, FFN_NCH + chunk(s))),
                pl.BlockSpec((1, FFN_FC, D_MODEL), lambda s: (layer, chunk(s), 0))]
    args = [h, norm_w, w_gu, w_gu, w_d]
    scratch = [pltpu.VMEM((tm, D_MODEL), BF16),
               pltpu.VMEM((tm, D_FF), BF16),
               pltpu.VMEM((FFN_NCH, D_MODEL, FFN_FC), BF16),
               pltpu.VMEM((FFN_NCH, D_MODEL, FFN_FC), BF16),
               pltpu.VMEM((D_FF, D_MODEL), BF16)]
    if ple is not None:
        p, pn, wgate, wup = ple
        in_specs += [pl.BlockSpec((1, tm, PLE_DIM), lambda s: (layer, tile(s), 0)),
                     _const_spec((1, 1, D_MODEL), (layer, 0, 0)),
                     _const_spec((1, D_MODEL, D_MODEL), (layer, 0, 0)),
                     _const_spec((1, PLE_DIM, D_MODEL), (layer, 0, 0))]
        args += [p, pn, wgate, wup]
        scratch += [pltpu.VMEM((D_MODEL, D_MODEL), BF16), pltpu.VMEM((PLE_DIM, D_MODEL), BF16)]
    if final_norm is not None:
        in_specs.append(_const_spec((1, D_MODEL)))
        args.append(final_norm)
    return pl.pallas_call(
        functools.partial(_ffn_kernel, with_ple=ple is not None, with_final=final_norm is not None),
        grid=(last + t // tm,),
        in_specs=in_specs,
        out_specs=row_spec,
        out_shape=jax.ShapeDtypeStruct((t, D_MODEL), F32),
        scratch_shapes=scratch,
        compiler_params=pltpu.CompilerParams(dimension_semantics=("arbitrary",),
                                             vmem_limit_bytes=VMEM_LIMIT),
        name="ffn_ple" if ple is not None else "ffn",
    )(*args)


def _mix_even_kernel(h_ref, nw_ref, win_ref, lng_ref, lnb_ref, ws_ref, bs_ref, scw_ref, wout_ref,
                     o_ref, win_scr, wout_scr, wt_scr, hn_ref, z_ref, vn_ref, q_ref, y_ref):
    tm = MIX_TM
    i = pl.program_id(0)
    j = pl.program_id(1)

    @pl.when((i == 0) & (j == 0))
    def _():
        _cast_rows(win_scr, win_ref)
        _cast_rows(wout_scr, wout_ref)
        row = lax.broadcasted_iota(jnp.int32, (CHUNK, CHUNK), 0)
        col = lax.broadcasted_iota(jnp.int32, (CHUNK, CHUNK), 1)
        for g in range(N_GROUPS):
            wt_scr[g] = jnp.where(col <= row, ws_ref[0, g], 0.0).astype(BF16)

    x = h_ref[0]
    hn_ref[...] = _rmsnorm(x, nw_ref[0]).astype(BF16)
    z_ref[...] = _dot(hn_ref[...], win_scr[...])

    for g in range(N_GROUPS):
        gsl = slice(g * GROUP_W, (g + 1) * GROUP_W)
        v = _gelu_tanh(z_ref[:, HALF + g * GROUP_W:HALF + (g + 1) * GROUP_W])
        vn_ref[:, gsl] = _layernorm(v, lng_ref[0][:, gsl], lnb_ref[0][:, gsl]).astype(BF16)
    for g in range(N_GROUPS):
        gsl = slice(g * GROUP_W, (g + 1) * GROUP_W)
        for c in range(tm // CHUNK):
            rsl = slice(c * CHUNK, (c + 1) * CHUNK)
            mixed = _dot(wt_scr[g], vn_ref[rsl, gsl]) + bs_ref[:, gsl]
            u = _gelu_tanh(z_ref[rsl, gsl])
            y_ref[rsl, gsl] = (u * mixed).astype(BF16)

    @pl.when(j == 0)
    def _():
        q_ref[0:SC_HALO, :] = jnp.zeros((SC_HALO, HALF), F32)

    q = z_ref[:, 3 * HALF:4 * HALF] * z_ref[:, 4 * HALF:5 * HALF]
    q_ref[SC_HALO:SC_HALO + tm, :] = q
    conv = scw_ref[0, 2:3, :] * q
    conv = conv + scw_ref[0, 1:2, :] * q_ref[SC_HALO - 1:SC_HALO - 1 + tm, :]
    conv = conv + scw_ref[0, 0:1, :] * q_ref[SC_HALO - 2:SC_HALO - 2 + tm, :]
    y_ref[:, HALF:2 * HALF] = (z_ref[:, 2 * HALF:3 * HALF] * conv).astype(BF16)
    q_ref[0:SC_HALO, :] = q_ref[tm:tm + SC_HALO, :]

    o_ref[0] = x + _dot(y_ref[...], wout_scr[...])


def _mix_even_call(h3, layer, j, norm_w, w_in, ln_g, ln_b, w_s, b_full, sc_w, w_out):
    b, s, _ = h3.shape
    tm = MIX_TM
    row_spec = pl.BlockSpec((1, tm, D_MODEL), lambda i, t: (i, t, 0))
    in_specs = [row_spec, _const_spec((1, 1, D_MODEL), (layer, 0, 0)),
                _const_spec((1, D_MODEL, 5 * HALF), (j, 0, 0)),
                _const_spec((1, 1, HALF), (j, 0, 0)), _const_spec((1, 1, HALF), (j, 0, 0)),
                _const_spec((1, N_GROUPS, CHUNK, CHUNK), (j, 0, 0, 0)), _const_spec((CHUNK, HALF)),
                _const_spec((1, SC_WIDTH, HALF), (j, 0, 0)),
                _const_spec((1, D_MODEL, D_MODEL), (j, 0, 0))]
    return pl.pallas_call(
        _mix_even_kernel,
        grid=(b, s // tm),
        in_specs=in_specs,
        out_specs=row_spec,
        out_shape=jax.ShapeDtypeStruct(h3.shape, F32),
        scratch_shapes=[pltpu.VMEM((D_MODEL, 5 * HALF), BF16),
                        pltpu.VMEM((D_MODEL, D_MODEL), BF16),
                        pltpu.VMEM((N_GROUPS, CHUNK, CHUNK), BF16),
                        pltpu.VMEM((tm, D_MODEL), BF16),
                        pltpu.VMEM((tm, 5 * HALF), F32),
                        pltpu.VMEM((tm, HALF), BF16),
                        pltpu.VMEM((SC_HALO + tm, HALF), F32),
                        pltpu.VMEM((tm, D_MODEL), BF16)],
        compiler_params=pltpu.CompilerParams(dimension_semantics=("arbitrary", "arbitrary"),
                                             vmem_limit_bytes=VMEM_LIMIT),
        name="mix_even",
    )(h3, norm_w, w_in, ln_g, ln_b, w_s, b_full, sc_w, w_out)


def _mix_odd_kernel(h_ref, nw_ref, win_ref, pw_ref, ps_ref, cvw_ref, cvb_ref, lng_ref, lnb_ref,
                    wout_ref, o_ref, win_scr, wout_scr, pw_scr, hn_ref, p_ref, sa_ref,
                    sb_ref, c_ref, hc_ref, y_ref):
    tm = MIX_TM_ODD
    n = HALO + tm
    i = pl.program_id(0)
    j = pl.program_id(1)

    @pl.when((i == 0) & (j == 0))
    def _():
        _cast_rows(win_scr, win_ref)
        _cast_rows(wout_scr, wout_ref)
        for g in range(N_GROUPS):
            pw_scr[g] = pw_ref[0, g].astype(BF16)

    @pl.when(j == 0)
    def _():
        p_ref[0:HALO, :] = jnp.zeros((HALO, HALF), F32)
        c_ref[0:HALO, :] = jnp.zeros((HALO, HALF), F32)

    x = h_ref[0]
    hn_ref[...] = _rmsnorm(x, nw_ref[0]).astype(BF16)
    p_ref[HALO:n, :] = _dot(hn_ref[...], win_scr[:, 0:HALF])
    for c in range(HALF // GLU_FC):
        csl = slice(c * GLU_FC, (c + 1) * GLU_FC)
        a = _dot(hn_ref[...], win_scr[:, HALF + c * GLU_FC:HALF + (c + 1) * GLU_FC])
        g = _dot(hn_ref[...], win_scr[:, 2 * HALF + c * GLU_FC:2 * HALF + (c + 1) * GLU_FC])
        c_ref[HALO:n, csl] = a * _sigmoid(g)

    g1, g2, g3 = GROUP_W, 2 * GROUP_W, 3 * GROUP_W
    sa_ref[8:n, :] = p_ref[8:n, :] + p_ref[7:n - 1, :]
    sb_ref[16:n, g1:] = sa_ref[16:n, g1:] + sa_ref[14:n - 2, g1:]
    sa_ref[24:n, g2:] = sb_ref[24:n, g2:] + sb_ref[20:n - 4, g2:]
    sb_ref[32:n, g3:] = sa_ref[32:n, g3:] + sa_ref[24:n - 8, g3:]
    pos = j * tm + lax.broadcasted_iota(jnp.int32, (tm, GROUP_W), 0)
    for gi, win in enumerate(POOL_WINDOWS):
        gsl = slice(gi * GROUP_W, (gi + 1) * GROUP_W)
        s_ref = sa_ref if gi % 2 == 0 else sb_ref
        count = jnp.minimum(pos + 1, win).astype(F32)
        pooled = s_ref[HALO:n, gsl] / count - p_ref[HALO:n, gsl]
        yc = _dot(pooled.astype(BF16), pw_scr[gi]) * ps_ref[0][:, gsl]
        y_ref[:, gsl] = yc.astype(BF16)
    p_ref[0:HALO, :] = p_ref[tm:n, :]

    def conv_block(r, carry):
        r0 = pl.multiple_of(r * CV_ROWS, CV_ROWS)
        for lt in range(HALF // LANES):
            lsl = slice(lt * LANES, (lt + 1) * LANES)
            chunk = c_ref[pl.ds(r0, HALO + CV_ROWS), lsl]
            acc = None
            for b in range(SUBLANES):
                rolled = chunk if b == 0 else pltpu.roll(chunk, b, axis=0)
                for a in range(HALO // SUBLANES):
                    d = SUBLANES * a + b
                    if d >= CV_WIDTH:
                        continue
                    lo = HALO - SUBLANES * a
                    src = rolled[lo:lo + CV_ROWS].reshape(CV_ROWS // SUBLANES, SUBLANES, LANES)
                    term = cvw_ref[CV_WIDTH - 1 - d, :, lsl][None] * src
                    acc = term if acc is None else acc + term
            acc = acc + cvb_ref[0][:, lsl][None]
            hc_ref[pl.ds(r0, CV_ROWS), lsl] = acc.reshape(CV_ROWS, LANES)
        return carry

    lax.fori_loop(0, tm // CV_ROWS, conv_block, 0)
    c_ref[0:HALO, :] = c_ref[tm:n, :]
    ln = _layernorm(hc_ref[...], lng_ref[0], lnb_ref[0])
    y_ref[:, HALF:2 * HALF] = _silu(ln).astype(BF16)

    o_ref[0] = x + _dot(y_ref[...], wout_scr[...])


def _mix_odd_call(h3, layer, j, norm_w, w_in, pool_w, pool_scale, cv_w8, cv_b, ln_g, ln_b, w_out):
    b, s, _ = h3.shape
    tm = MIX_TM_ODD
    tiles = s // tm
    row_spec = pl.BlockSpec((1, tm, D_MODEL), lambda i, t: (i, t, 0))
    vec = _const_spec((1, 1, HALF), (j, 0, 0))
    in_specs = [row_spec, _const_spec((1, 1, D_MODEL), (layer, 0, 0)),
                _const_spec((1, D_MODEL, 3 * HALF), (j, 0, 0)),
                _const_spec((1, N_GROUPS, GROUP_W, GROUP_W), (j, 0, 0, 0)), vec,
                _const_spec((CV_WIDTH, SUBLANES, HALF)), vec, vec, vec,
                _const_spec((1, D_MODEL, D_MODEL), (j, 0, 0))]
    return pl.pallas_call(
        _mix_odd_kernel,
        grid=(b, tiles),
        in_specs=in_specs,
        out_specs=row_spec,
        out_shape=jax.ShapeDtypeStruct(h3.shape, F32),
        scratch_shapes=[pltpu.VMEM((D_MODEL, 3 * HALF), BF16),
                        pltpu.VMEM((D_MODEL, D_MODEL), BF16),
                        pltpu.VMEM((N_GROUPS, GROUP_W, GROUP_W), BF16),
                        pltpu.VMEM((tm, D_MODEL), BF16),
                        pltpu.VMEM((HALO + tm, HALF), F32),
                        pltpu.VMEM((HALO + tm, HALF), F32),
                        pltpu.VMEM((HALO + tm, HALF), F32),
                        pltpu.VMEM((HALO + tm, HALF), F32),
                        pltpu.VMEM((tm, HALF), F32),
                        pltpu.VMEM((tm, D_MODEL), BF16)],
        compiler_params=pltpu.CompilerParams(dimension_semantics=("arbitrary", "arbitrary"),
                                             vmem_limit_bytes=VMEM_LIMIT),
        name="mix_odd",
    )(h3, norm_w, w_in, pool_w, pool_scale, cv_w8, cv_b, ln_g, ln_b, w_out)


def kernel(x, p, ffn1_norm, ffn1_w_gu, ffn1_w_down, mix_norm, ffn2_norm, ffn2_w_gu, ffn2_w_down,
           ple_norm, ple_w_gate, ple_w_up, ab_w_in, gm_ln_g, gm_ln_b, gm_w_s, gm_b_s, sc_w,
           ab_w_out, cd_w_in, pool_w, pool_scale, cv_w, cv_b, cv_ln_g, cv_ln_b, cd_w_out,
           final_norm):
    bsz, seq, d = x.shape
    depth = p.shape[0]
    t = bsz * seq
    vec3 = lambda v: v.reshape(v.shape[0], 1, v.shape[1])
    p3 = p.reshape(depth, t, PLE_DIM)
    h = x.reshape(t, d)
    for i in range(depth):
        j = i // 2
        h = _ffn_call(h, i, vec3(ffn1_norm), ffn1_w_gu, ffn1_w_down)
        h3 = h.reshape(bsz, seq, d)
        if i % 2 == 0:
            b_full = jnp.repeat(gm_b_s[j].T, GROUP_W, axis=1)
            h3 = _mix_even_call(h3, i, j, vec3(mix_norm), ab_w_in, vec3(gm_ln_g), vec3(gm_ln_b),
                                gm_w_s, b_full, sc_w, ab_w_out)
        else:
            cv_w8 = jnp.broadcast_to(cv_w[j][:, None, :], (CV_WIDTH, SUBLANES, HALF))
            h3 = _mix_odd_call(h3, i, j, vec3(mix_norm), cd_w_in, pool_w, vec3(pool_scale), cv_w8,
                               vec3(cv_b), vec3(cv_ln_g), vec3(cv_ln_b), cd_w_out)
        h = h3.reshape(t, d)
        h = _ffn_call(h, i, vec3(ffn2_norm), ffn2_w_gu, ffn2_w_down,
                      ple=(p3, vec3(ple_norm), ple_w_gate, ple_w_up),
                      final_norm=final_norm.reshape(1, d) if i == depth - 1 else None)
    return h.reshape(bsz, seq, d)
```

```python
import functools

import jax
import jax.numpy as jnp
from jax import lax
from jax.experimental import pallas as pl
from jax.experimental.pallas import tpu as pltpu

F32 = jnp.float32
BF16 = jnp.bfloat16

D_MODEL = 1024
PLE_DIM = 256
D_FF = 2816
HALF = 512
LANES = 128
SUBLANES = 8
GROUP_W = 128
N_GROUPS = 4
CHUNK = 128
POOL_WINDOWS = (2, 4, 8, 16)
SC_WIDTH = 3
CV_WIDTH = 31
RMS_EPS = 1e-6
LN_EPS = 1e-5

FFN_TM = 512
FFN_TM_PLAIN = 1024
FFN_FC = 256
FFN_NCH = D_FF // FFN_FC
MIX_TM = 512
MIX_TM_ODD = 1024
GLU_FC = 256
CAST_ROWS = 256
SC_HALO = 8
HALO = 32
CV_ROWS = 64
VMEM_LIMIT = 52 * 1024 * 1024


def _rmsnorm(x, g):
    ms = jnp.mean(x * x, axis=-1, keepdims=True)
    return x * lax.rsqrt(ms + RMS_EPS) * g


def _layernorm(x, g, b):
    mu = jnp.mean(x, axis=-1, keepdims=True)
    xc = x - mu
    var = jnp.mean(xc * xc, axis=-1, keepdims=True)
    return xc * lax.rsqrt(var + LN_EPS) * g + b


def _sigmoid(x):
    return 0.5 + 0.5 * jnp.tanh(0.5 * x)


def _silu(x):
    hx = 0.5 * x
    return hx + hx * jnp.tanh(hx)


def _gelu_tanh(x):
    c = 0.7978845608028654
    return 0.5 * x * (1.0 + jnp.tanh(c * (x + 0.044715 * (x * x * x))))


def _dot(a, b):
    return jnp.dot(a, b, preferred_element_type=F32)


def _cast_rows(dst_ref, src_ref):
    rows = src_ref.shape[1]
    step = min(rows, CAST_ROWS)
    for r in range(0, rows, step):
        dst_ref[r:r + step, :] = src_ref[0, r:r + step, :].astype(BF16)


def _const_spec(shape, index=None):
    index = (0,) * len(shape) if index is None else index
    return pl.BlockSpec(shape, lambda *_: index, pipeline_mode=pl.Buffered(1))


def _ffn_kernel(*refs, with_ple, with_final):
    h_ref, nw_ref, wg_blk, wu_blk, wd_blk = refs[:5]
    k = 5
    if with_ple:
        p_ref, pn_ref, wgate_ref, wup_ref = refs[k:k + 4]
        k += 4
    if with_final:
        fn_ref = refs[k]
        k += 1
    o_ref, xn_ref, a_ref, wg_scr, wu_scr, wd_scr = refs[k:k + 6]
    k += 6
    if with_ple:
        wgate_scr, wup_scr = refs[k:k + 2]
    s = pl.program_id(0)

    def normalize():
        xn_ref[...] = _rmsnorm(h_ref[...], nw_ref[0]).astype(BF16)

    def up_chunk(c):
        sl = slice(c * FFN_FC, (c + 1) * FFN_FC)
        g = _dot(xn_ref[...], wg_scr[c])
        u = _dot(xn_ref[...], wu_scr[c])
        a_ref[:, sl] = (_silu(g) * u).astype(BF16)

    def finish():
        h1 = h_ref[...] + 0.5 * _dot(a_ref[...], wd_scr[...])
        if with_ple:
            xn_ref[...] = _rmsnorm(h1, pn_ref[0]).astype(BF16)
            gate = _sigmoid(_dot(xn_ref[...], wgate_scr[...]))
            up = _dot(p_ref[0].astype(BF16), wup_scr[...])
            h1 = h1 + gate * up
        if with_final:
            h1 = _rmsnorm(h1, fn_ref[...])
        o_ref[...] = h1

    @pl.when(s == 0)
    def _():
        normalize()
        if with_ple:
            _cast_rows(wgate_scr, wgate_ref)
            _cast_rows(wup_scr, wup_ref)

    for c in range(FFN_NCH):
        @pl.when(s == c)
        def _(c=c):
            wg_scr[c] = wg_blk[0].astype(BF16)
            wu_scr[c] = wu_blk[0].astype(BF16)
            wd_scr[c * FFN_FC:(c + 1) * FFN_FC, :] = wd_blk[0].astype(BF16)
            up_chunk(c)
            if c == FFN_NCH - 1:
                finish()

    @pl.when(s >= FFN_NCH)
    def _():
        normalize()
        for c in range(FFN_NCH):
            up_chunk(c)
        finish()


def _ffn_call(h, layer, norm_w, w_gu, w_d, ple=None, final_norm=None):
    t = h.shape[0]
    tm = FFN_TM if ple is not None else FFN_TM_PLAIN
    last = FFN_NCH - 1
    tile = lambda s: jnp.maximum(s - last, 0)
    chunk = lambda s: jnp.minimum(s, last)
    row_spec = pl.BlockSpec((tm, D_MODEL), lambda s: (tile(s), 0))
    in_specs = [row_spec, _const_spec((1, 1, D_MODEL), (layer, 0, 0)),
                pl.BlockSpec((1, D_MODEL, FFN_FC), lambda s: (layer, 0, chunk(s))),
                pl.BlockSpec((1, D_MODEL, FFN_FC), lambda s: (layer, 0, FFN_NCH + chunk(s))),
                pl.BlockSpec((1, FFN_FC, D_MODEL), lambda s: (layer, chunk(s), 0))]
    args = [h, norm_w, w_gu, w_gu, w_d]
    scratch = [pltpu.VMEM((tm, D_MODEL), BF16),
               pltpu.VMEM((tm, D_FF), BF16),
               pltpu.VMEM((FFN_NCH, D_MODEL, FFN_FC), BF16),
               pltpu.VMEM((FFN_NCH, D_MODEL, FFN_FC), BF16),
               pltpu.VMEM((D_FF, D_MODEL), BF16)]
    if ple is not None:
        p, pn, wgate, wup = ple
        in_specs += [pl.BlockSpec((1, tm, PLE_DIM), lambda s: (layer, tile(s), 0)),
                     _const_spec((1, 1, D_MODEL), (layer, 0, 0)),
                     _const_spec((1, D_MODEL, D_MODEL), (layer, 0, 0)),
                     _const_spec((1, PLE_DIM, D_MODEL), (layer, 0, 0))]
        args += [p, pn, wgate, wup]
        scratch += [pltpu.VMEM((D_MODEL, D_MODEL), BF16), pltpu.VMEM((PLE_DIM, D_MODEL), BF16)]
    if final_norm is not None:
        in_specs.append(_const_spec((1, D_MODEL)))
        args.append(final_norm)
    return pl.pallas_call(
        functools.partial(_ffn_kernel, with_ple=ple is not None, with_final=final_norm is not None),
        grid=(last + t // tm,),
        in_specs=in_specs,
        out_specs=row_spec,
        out_shape=jax.ShapeDtypeStruct((t, D_MODEL), F32),
        scratch_shapes=scratch,
        compiler_params=pltpu.CompilerParams(dimension_semantics=("arbitrary",),
                                             vmem_limit_bytes=VMEM_LIMIT),
        name="ffn_ple" if ple is not None else "ffn",
    )(*args)


def _mix_even_kernel(h_ref, nw_ref, win_ref, lng_ref, lnb_ref, ws_ref, bs_ref, scw_ref, wout_ref,
                     o_ref, win_scr, wout_scr, wt_scr, hn_ref, z_ref, vn_ref, q_ref, y_ref):
    tm = MIX_TM
    i = pl.program_id(0)
    j = pl.program_id(1)

    @pl.when((i == 0) & (j == 0))
    def _():
        _cast_rows(win_scr, win_ref)
        _cast_rows(wout_scr, wout_ref)
        row = lax.broadcasted_iota(jnp.int32, (CHUNK, CHUNK), 0)
        col = lax.broadcasted_iota(jnp.int32, (CHUNK, CHUNK), 1)
        for g in range(N_GROUPS):
            wt_scr[g] = jnp.where(col <= row, ws_ref[0, g], 0.0).astype(BF16)

    x = h_ref[0]
    hn_ref[...] = _rmsnorm(x, nw_ref[0]).astype(BF16)
    z_ref[...] = _dot(hn_ref[...], win_scr[...])

    for g in range(N_GROUPS):
        gsl = slice(g * GROUP_W, (g + 1) * GROUP_W)
        v = _gelu_tanh(z_ref[:, HALF + g * GROUP_W:HALF + (g + 1) * GROUP_W])
        vn_ref[:, gsl] = _layernorm(v, lng_ref[0][:, gsl], lnb_ref[0][:, gsl]).astype(BF16)
    for g in range(N_GROUPS):
        gsl = slice(g * GROUP_W, (g + 1) * GROUP_W)
        for c in range(tm // CHUNK):
            rsl = slice(c * CHUNK, (c + 1) * CHUNK)
            mixed = _dot(wt_scr[g], vn_ref[rsl, gsl]) + bs_ref[:, gsl]
            u = _gelu_tanh(z_ref[rsl, gsl])
            y_ref[rsl, gsl] = (u * mixed).astype(BF16)

    @pl.when(j == 0)
    def _():
        q_ref[0:SC_HALO, :] = jnp.zeros((SC_HALO, HALF), F32)

    q = z_ref[:, 3 * HALF:4 * HALF] * z_ref[:, 4 * HALF:5 * HALF]
    q_ref[SC_HALO:SC_HALO + tm, :] = q
    conv = scw_ref[0, 2:3, :] * q
    conv = conv + scw_ref[0, 1:2, :] * q_ref[SC_HALO - 1:SC_HALO - 1 + tm, :]
    conv = conv + scw_ref[0, 0:1, :] * q_ref[SC_HALO - 2:SC_HALO - 2 + tm, :]
    y_ref[:, HALF:2 * HALF] = (z_ref[:, 2 * HALF:3 * HALF] * conv).astype(BF16)
    q_ref[0:SC_HALO, :] = q_ref[tm:tm + SC_HALO, :]

    o_ref[0] = x + _dot(y_ref[...], wout_scr[...])


def _mix_even_call(h3, layer, j, norm_w, w_in, ln_g, ln_b, w_s, b_full, sc_w, w_out):
    b, s, _ = h3.shape
    tm = MIX_TM
    row_spec = pl.BlockSpec((1, tm, D_MODEL), lambda i, t: (i, t, 0))
    in_specs = [row_spec, _const_spec((1, 1, D_MODEL), (layer, 0, 0)),
                _const_spec((1, D_MODEL, 5 * HALF), (j, 0, 0)),
                _const_spec((1, 1, HALF), (j, 0, 0)), _const_spec((1, 1, HALF), (j, 0, 0)),
                _const_spec((1, N_GROUPS, CHUNK, CHUNK), (j, 0, 0, 0)), _const_spec((CHUNK, HALF)),
                _const_spec((1, SC_WIDTH, HALF), (j, 0, 0)),
                _const_spec((1, D_MODEL, D_MODEL), (j, 0, 0))]
    return pl.pallas_call(
        _mix_even_kernel,
        grid=(b, s // tm),
        in_specs=in_specs,
        out_specs=row_spec,
        out_shape=jax.ShapeDtypeStruct(h3.shape, F32),
        scratch_shapes=[pltpu.VMEM((D_MODEL, 5 * HALF), BF16),
                        pltpu.VMEM((D_MODEL, D_MODEL), BF16),
                        pltpu.VMEM((N_GROUPS, CHUNK, CHUNK), BF16),
                        pltpu.VMEM((tm, D_MODEL), BF16),
                        pltpu.VMEM((tm, 5 * HALF), F32),
                        pltpu.VMEM((tm, HALF), BF16),
                        pltpu.VMEM((SC_HALO + tm, HALF), F32),
                        pltpu.VMEM((tm, D_MODEL), BF16)],
        compiler_params=pltpu.CompilerParams(dimension_semantics=("arbitrary", "arbitrary"),
                                             vmem_limit_bytes=VMEM_LIMIT),
        name="mix_even",
    )(h3, norm_w, w_in, ln_g, ln_b, w_s, b_full, sc_w, w_out)


def _mix_odd_kernel(zs_ref, h_ref, nw_ref, win_ref, pw_ref, ps_ref, cvw_ref, cvb_ref, lng_ref,
                    lnb_ref, wout_ref, o_ref, win_scr, wout_scr, pw_scr, hn_ref, p_ref, sa_ref,
                    sb_ref, ca_ref, cb_ref, y_ref):
    tm = MIX_TM_ODD
    n = HALO + tm
    i = pl.program_id(0)
    j = pl.program_id(1)

    @pl.when((i == 0) & (j == 0))
    def _():
        _cast_rows(win_scr, win_ref)
        _cast_rows(wout_scr, wout_ref)
        for g in range(N_GROUPS):
            pw_scr[g] = pw_ref[0, g].astype(BF16)

    @pl.when(j == 0)
    def _():
        p_ref[0:HALO, :] = jnp.zeros((HALO, HALF), F32)
        ca_ref[0:HALO, :] = jnp.zeros((HALO, GLU_FC), F32)
        cb_ref[0:HALO, :] = jnp.zeros((HALO, GLU_FC), F32)

    x = h_ref[0]
    hn_ref[...] = _rmsnorm(x, nw_ref[0]).astype(BF16)
    p_ref[HALO:n, :] = _dot(hn_ref[...], win_scr[:, 0:HALF])
    c_refs = (ca_ref, cb_ref)
    for c in range(HALF // GLU_FC):
        a = _dot(hn_ref[...], win_scr[:, HALF + c * GLU_FC:HALF + (c + 1) * GLU_FC])
        g = _dot(hn_ref[...], win_scr[:, 2 * HALF + c * GLU_FC:2 * HALF + (c + 1) * GLU_FC])
        c_refs[c][HALO:n, :] = a * _sigmoid(g)

    g1, g2, g3 = GROUP_W, 2 * GROUP_W, 3 * GROUP_W
    sa_ref[8:n, :] = p_ref[8:n, :] + p_ref[7:n - 1, :]
    sb_ref[16:n, g1:] = sa_ref[16:n, g1:] + sa_ref[14:n - 2, g1:]
    sa_ref[24:n, g2:] = sb_ref[24:n, g2:] + sb_ref[20:n - 4, g2:]
    sb_ref[32:n, g3:] = sa_ref[32:n, g3:] + sa_ref[24:n - 8, g3:]
    pos = j * tm + lax.broadcasted_iota(jnp.int32, (tm, GROUP_W), 0)
    for gi, win in enumerate(POOL_WINDOWS):
        gsl = slice(gi * GROUP_W, (gi + 1) * GROUP_W)
        s_ref = sa_ref if gi % 2 == 0 else sb_ref
        count = jnp.minimum(pos + 1, win).astype(F32)
        pooled = s_ref[HALO:n, gsl] / count - p_ref[HALO:n, gsl]
        yc = _dot(pooled.astype(BF16), pw_scr[gi]) * ps_ref[0][:, gsl]
        y_ref[:, gsl] = yc.astype(BF16)
    p_ref[0:HALO, :] = p_ref[tm:n, :]
    o_ref[0] = x + _dot(y_ref[:, 0:HALF], wout_scr[0:HALF, :])

    zero = zs_ref[0]
    for c in range(HALF // GLU_FC):
        c_ref = c_refs[c]
        for r in range(tm // CV_ROWS):
            r0 = pl.multiple_of(zero + r * CV_ROWS, CV_ROWS)
            for lt in range(GLU_FC // LANES):
                lsl = slice(lt * LANES, (lt + 1) * LANES)
                wsl = slice(c * GLU_FC + lt * LANES, c * GLU_FC + (lt + 1) * LANES)
                chunk = c_ref[pl.ds(r0, HALO + CV_ROWS), lsl]
                acc = None
                for b in range(SUBLANES):
                    rolled = chunk if b == 0 else pltpu.roll(chunk, b, axis=0)
                    for a in range(HALO // SUBLANES):
                        d = SUBLANES * a + b
                        if d >= CV_WIDTH:
                            continue
                        lo = HALO - SUBLANES * a
                        src = rolled[lo:lo + CV_ROWS].reshape(CV_ROWS // SUBLANES, SUBLANES, LANES)
                        term = cvw_ref[CV_WIDTH - 1 - d, :, wsl][None] * src
                        acc = term if acc is None else acc + term
                acc = acc + cvb_ref[0][:, wsl][None]
                c_ref[pl.ds(r0 + n, CV_ROWS), lsl] = acc.reshape(CV_ROWS, LANES)
    for c_ref in c_refs:
        c_ref[0:HALO, :] = c_ref[tm:n, :]
    hc = jnp.concatenate([c_ref[n:n + tm, :] for c_ref in c_refs], axis=-1)
    ln = _layernorm(hc, lng_ref[0], lnb_ref[0])
    y_ref[:, HALF:2 * HALF] = _silu(ln).astype(BF16)

    o_ref[0] = o_ref[0] + _dot(y_ref[:, HALF:2 * HALF], wout_scr[HALF:2 * HALF, :])


def _mix_odd_call(h3, layer, j, norm_w, w_in, pool_w, pool_scale, cv_w8, cv_b, ln_g, ln_b, w_out):
    b, s, _ = h3.shape
    tm = MIX_TM_ODD
    tiles = s // tm
    row_spec = pl.BlockSpec((1, tm, D_MODEL), lambda i, t: (i, t, 0))
    vec = _const_spec((1, 1, HALF), (j, 0, 0))
    in_specs = [pl.BlockSpec(memory_space=pltpu.SMEM), row_spec,
                _const_spec((1, 1, D_MODEL), (layer, 0, 0)),
                _const_spec((1, D_MODEL, 3 * HALF), (j, 0, 0)),
                _const_spec((1, N_GROUPS, GROUP_W, GROUP_W), (j, 0, 0, 0)), vec,
                _const_spec((CV_WIDTH, SUBLANES, HALF)), vec, vec, vec,
                _const_spec((1, D_MODEL, D_MODEL), (j, 0, 0))]
    return pl.pallas_call(
        _mix_odd_kernel,
        grid=(b, tiles),
        in_specs=in_specs,
        out_specs=row_spec,
        out_shape=jax.ShapeDtypeStruct(h3.shape, F32),
        scratch_shapes=[pltpu.VMEM((D_MODEL, 3 * HALF), BF16),
                        pltpu.VMEM((D_MODEL, D_MODEL), BF16),
                        pltpu.VMEM((N_GROUPS, GROUP_W, GROUP_W), BF16),
                        pltpu.VMEM((tm, D_MODEL), BF16),
                        pltpu.VMEM((HALO + tm, HALF), F32),
                        pltpu.VMEM((HALO + tm, HALF), F32),
                        pltpu.VMEM((HALO + tm, HALF), F32),
                        pltpu.VMEM((HALO + 2 * tm, GLU_FC), F32),
                        pltpu.VMEM((HALO + 2 * tm, GLU_FC), F32),
                        pltpu.VMEM((tm, D_MODEL), BF16)],
        compiler_params=pltpu.CompilerParams(dimension_semantics=("arbitrary", "arbitrary"),
                                             vmem_limit_bytes=VMEM_LIMIT),
        name="mix_odd",
    )(jnp.zeros((1,), jnp.int32), h3, norm_w, w_in, pool_w, pool_scale, cv_w8, cv_b, ln_g, ln_b,
      w_out)


def kernel(x, p, ffn1_norm, ffn1_w_gu, ffn1_w_down, mix_norm, ffn2_norm, ffn2_w_gu, ffn2_w_down,
           ple_norm, ple_w_gate, ple_w_up, ab_w_in, gm_ln_g, gm_ln_b, gm_w_s, gm_b_s, sc_w,
           ab_w_out, cd_w_in, pool_w, pool_scale, cv_w, cv_b, cv_ln_g, cv_ln_b, cd_w_out,
           final_norm):
    bsz, seq, d = x.shape
    depth = p.shape[0]
    t = bsz * seq
    vec3 = lambda v: v.reshape(v.shape[0], 1, v.shape[1])
    p3 = p.reshape(depth, t, PLE_DIM)
    h = x.reshape(t, d)
    for i in range(depth):
        j = i // 2
        h = _ffn_call(h, i, vec3(ffn1_norm), ffn1_w_gu, ffn1_w_down)
        h3 = h.reshape(bsz, seq, d)
        if i % 2 == 0:
            b_full = jnp.repeat(gm_b_s[j].T, GROUP_W, axis=1)
            h3 = _mix_even_call(h3, i, j, vec3(mix_norm), ab_w_in, vec3(gm_ln_g), vec3(gm_ln_b),
                                gm_w_s, b_full, sc_w, ab_w_out)
        else:
            cv_w8 = jnp.broadcast_to(cv_w[j][:, None, :], (CV_WIDTH, SUBLANES, HALF))
            h3 = _mix_odd_call(h3, i, j, vec3(mix_norm), cd_w_in, pool_w, vec3(pool_scale), cv_w8,
                               vec3(cv_b), vec3(cv_ln_g), vec3(cv_ln_b), cd_w_out)
        h = h3.reshape(t, d)
        h = _ffn_call(h, i, vec3(ffn2_norm), ffn2_w_gu, ffn2_w_down,
                      ple=(p3, vec3(ple_norm), ple_w_gate, ple_w_up),
                      final_norm=final_norm.reshape(1, d) if i == depth - 1 else None)
    return h.reshape(bsz, seq, d)
```

```python
import functools

import jax
import jax.numpy as jnp
from jax import lax
from jax.experimental import pallas as pl
from jax.experimental.pallas import tpu as pltpu

F32 = jnp.float32
BF16 = jnp.bfloat16

D_MODEL = 1024
PLE_DIM = 256
D_FF = 2816
HALF = 512
LANES = 128
SUBLANES = 8
GROUP_W = 128
N_GROUPS = 4
CHUNK = 128
POOL_WINDOWS = (2, 4, 8, 16)
SC_WIDTH = 3
CV_WIDTH = 31
RMS_EPS = 1e-6
LN_EPS = 1e-5

FFN_TM = 512
FFN_TM_PLAIN = 1024
FFN_FC = 256
FFN_NCH = D_FF // FFN_FC
MIX_TM = 512
MIX_TM_ODD = 1024
GLU_FC = 256
CAST_ROWS = 256
SC_HALO = 8
HALO = 32
CV_ROWS = 64
VMEM_LIMIT = 52 * 1024 * 1024


def _rmsnorm(x, g):
    ms = jnp.mean(x * x, axis=-1, keepdims=True)
    return x * lax.rsqrt(ms + RMS_EPS) * g


def _layernorm(x, g, b):
    mu = jnp.mean(x, axis=-1, keepdims=True)
    xc = x - mu
    var = jnp.mean(xc * xc, axis=-1, keepdims=True)
    return xc * lax.rsqrt(var + LN_EPS) * g + b


def _sigmoid(x):
    return 0.5 + 0.5 * jnp.tanh(0.5 * x)


def _silu(x):
    hx = 0.5 * x
    return hx + hx * jnp.tanh(hx)


def _gelu_tanh(x):
    c = 0.7978845608028654
    return 0.5 * x * (1.0 + jnp.tanh(c * (x + 0.044715 * (x * x * x))))


def _dot(a, b):
    return jnp.dot(a, b, preferred_element_type=F32)


def _cast_rows(dst_ref, src_ref):
    rows = src_ref.shape[1]
    step = min(rows, CAST_ROWS)
    for r in range(0, rows, step):
        dst_ref[r:r + step, :] = src_ref[0, r:r + step, :].astype(BF16)


def _const_spec(shape, index=None):
    index = (0,) * len(shape) if index is None else index
    return pl.BlockSpec(shape, lambda *_: index, pipeline_mode=pl.Buffered(1))


def _ffn_kernel(*refs, with_ple, with_final):
    h_ref, nw_ref, wg_blk, wu_blk, wd_blk = refs[:5]
    k = 5
    if with_ple:
        p_ref, pn_ref, wgate_ref, wup_ref = refs[k:k + 4]
        k += 4
    if with_final:
        fn_ref = refs[k]
        k += 1
    o_ref, xn_ref, a_ref, wg_scr, wu_scr, wd_scr = refs[k:k + 6]
    k += 6
    if with_ple:
        wgate_scr, wup_scr = refs[k:k + 2]
    s = pl.program_id(0)

    def normalize():
        xn_ref[...] = _rmsnorm(h_ref[...], nw_ref[0]).astype(BF16)

    def up_chunk(c):
        sl = slice(c * FFN_FC, (c + 1) * FFN_FC)
        g = _dot(xn_ref[...], wg_scr[c])
        u = _dot(xn_ref[...], wu_scr[c])
        a_ref[:, sl] = (_silu(g) * u).astype(BF16)

    def finish():
        h1 = h_ref[...] + 0.5 * _dot(a_ref[...], wd_scr[...])
        if with_ple:
            xn_ref[...] = _rmsnorm(h1, pn_ref[0]).astype(BF16)
            gate = _sigmoid(_dot(xn_ref[...], wgate_scr[...]))
            up = _dot(p_ref[0].astype(BF16), wup_scr[...])
            h1 = h1 + gate * up
        if with_final:
            h1 = _rmsnorm(h1, fn_ref[...])
        o_ref[...] = h1

    @pl.when(s == 0)
    def _():
        normalize()
        if with_ple:
            _cast_rows(wgate_scr, wgate_ref)
            _cast_rows(wup_scr, wup_ref)

    for c in range(FFN_NCH):
        @pl.when(s == c)
        def _(c=c):
            wg_scr[c] = wg_blk[0].astype(BF16)
            wu_scr[c] = wu_blk[0].astype(BF16)
            wd_scr[c * FFN_FC:(c + 1) * FFN_FC, :] = wd_blk[0].astype(BF16)
            up_chunk(c)
            if c == FFN_NCH - 1:
                finish()

    @pl.when(s >= FFN_NCH)
    def _():
        normalize()
        for c in range(FFN_NCH):
            up_chunk(c)
        finish()


def _ffn_call(h, layer, norm_w, w_gu, w_d, ple=None, final_norm=None):
    t = h.shape[0]
    tm = FFN_TM if (ple is not None or layer > 0) else FFN_TM_PLAIN
    last = FFN_NCH - 1
    tile = lambda s: jnp.maximum(s - last, 0)
    chunk = lambda s: jnp.minimum(s, last)
    row_spec = pl.BlockSpec((tm, D_MODEL), lambda s: (tile(s), 0))
    in_specs = [row_spec, _const_spec((1, 1, D_MODEL), (layer, 0, 0)),
                pl.BlockSpec((1, D_MODEL, FFN_FC), lambda s: (layer, 0, chunk(s))),
                pl.BlockSpec((1, D_MODEL, FFN_FC), lambda s: (layer, 0, FFN_NCH + chunk(s))),
                pl.BlockSpec((1, FFN_FC, D_MODEL), lambda s: (layer, chunk(s), 0))]
    args = [h, norm_w, w_gu, w_gu, w_d]
    scratch = [pltpu.VMEM((tm, D_MODEL), BF16),
               pltpu.VMEM((tm, D_FF), BF16),
               pltpu.VMEM((FFN_NCH, D_MODEL, FFN_FC), BF16),
               pltpu.VMEM((FFN_NCH, D_MODEL, FFN_FC), BF16),
               pltpu.VMEM((D_FF, D_MODEL), BF16)]
    if ple is not None:
        p, pn, wgate, wup = ple
        in_specs += [pl.BlockSpec((1, tm, PLE_DIM), lambda s: (layer, tile(s), 0)),
                     _const_spec((1, 1, D_MODEL), (layer, 0, 0)),
                     _const_spec((1, D_MODEL, D_MODEL), (layer, 0, 0)),
                     _const_spec((1, PLE_DIM, D_MODEL), (layer, 0, 0))]
        args += [p, pn, wgate, wup]
        scratch += [pltpu.VMEM((D_MODEL, D_MODEL), BF16), pltpu.VMEM((PLE_DIM, D_MODEL), BF16)]
    if final_norm is not None:
        in_specs.append(_const_spec((1, D_MODEL)))
        args.append(final_norm)
    return pl.pallas_call(
        functools.partial(_ffn_kernel, with_ple=ple is not None, with_final=final_norm is not None),
        grid=(last + t // tm,),
        in_specs=in_specs,
        out_specs=row_spec,
        out_shape=jax.ShapeDtypeStruct((t, D_MODEL), F32),
        scratch_shapes=scratch,
        compiler_params=pltpu.CompilerParams(dimension_semantics=("arbitrary",),
                                             vmem_limit_bytes=VMEM_LIMIT),
        name="ffn_ple" if ple is not None else "ffn",
    )(*args)


def _mix_even_kernel(h_ref, nw_ref, win_ref, lng_ref, lnb_ref, ws_ref, bs_ref, scw_ref, wout_ref,
                     o_ref, win_scr, wout_scr, wt_scr, hn_ref, z_ref, vn_ref, q_ref, y_ref):
    tm = MIX_TM
    i = pl.program_id(0)
    j = pl.program_id(1)

    @pl.when((i == 0) & (j == 0))
    def _():
        _cast_rows(win_scr, win_ref)
        _cast_rows(wout_scr, wout_ref)
        row = lax.broadcasted_iota(jnp.int32, (CHUNK, CHUNK), 0)
        col = lax.broadcasted_iota(jnp.int32, (CHUNK, CHUNK), 1)
        for g in range(N_GROUPS):
            wt_scr[g] = jnp.where(col <= row, ws_ref[0, g], 0.0).astype(BF16)

    x = h_ref[0]
    hn_ref[...] = _rmsnorm(x, nw_ref[0]).astype(BF16)
    z_ref[...] = _dot(hn_ref[...], win_scr[...])

    for g in range(N_GROUPS):
        gsl = slice(g * GROUP_W, (g + 1) * GROUP_W)
        v = _gelu_tanh(z_ref[:, HALF + g * GROUP_W:HALF + (g + 1) * GROUP_W])
        vn_ref[:, gsl] = _layernorm(v, lng_ref[0][:, gsl], lnb_ref[0][:, gsl]).astype(BF16)
    for g in range(N_GROUPS):
        gsl = slice(g * GROUP_W, (g + 1) * GROUP_W)
        for c in range(tm // CHUNK):
            rsl = slice(c * CHUNK, (c + 1) * CHUNK)
            mixed = _dot(wt_scr[g], vn_ref[rsl, gsl]) + bs_ref[:, gsl]
            u = _gelu_tanh(z_ref[rsl, gsl])
            y_ref[rsl, gsl] = (u * mixed).astype(BF16)

    @pl.when(j == 0)
    def _():
        q_ref[0:SC_HALO, :] = jnp.zeros((SC_HALO, HALF), F32)

    q = z_ref[:, 3 * HALF:4 * HALF] * z_ref[:, 4 * HALF:5 * HALF]
    q_ref[SC_HALO:SC_HALO + tm, :] = q
    conv = scw_ref[0, 2:3, :] * q
    conv = conv + scw_ref[0, 1:2, :] * q_ref[SC_HALO - 1:SC_HALO - 1 + tm, :]
    conv = conv + scw_ref[0, 0:1, :] * q_ref[SC_HALO - 2:SC_HALO - 2 + tm, :]
    y_ref[:, HALF:2 * HALF] = (z_ref[:, 2 * HALF:3 * HALF] * conv).astype(BF16)
    q_ref[0:SC_HALO, :] = q_ref[tm:tm + SC_HALO, :]

    o_ref[0] = x + _dot(y_ref[...], wout_scr[...])


def _mix_even_call(h3, layer, j, norm_w, w_in, ln_g, ln_b, w_s, b_full, sc_w, w_out):
    b, s, _ = h3.shape
    tm = MIX_TM
    row_spec = pl.BlockSpec((1, tm, D_MODEL), lambda i, t: (i, t, 0))
    in_specs = [row_spec, _const_spec((1, 1, D_MODEL), (layer, 0, 0)),
                _const_spec((1, D_MODEL, 5 * HALF), (j, 0, 0)),
                _const_spec((1, 1, HALF), (j, 0, 0)), _const_spec((1, 1, HALF), (j, 0, 0)),
                _const_spec((1, N_GROUPS, CHUNK, CHUNK), (j, 0, 0, 0)), _const_spec((CHUNK, HALF)),
                _const_spec((1, SC_WIDTH, HALF), (j, 0, 0)),
                _const_spec((1, D_MODEL, D_MODEL), (j, 0, 0))]
    return pl.pallas_call(
        _mix_even_kernel,
        grid=(b, s // tm),
        in_specs=in_specs,
        out_specs=row_spec,
        out_shape=jax.ShapeDtypeStruct(h3.shape, F32),
        scratch_shapes=[pltpu.VMEM((D_MODEL, 5 * HALF), BF16),
                        pltpu.VMEM((D_MODEL, D_MODEL), BF16),
                        pltpu.VMEM((N_GROUPS, CHUNK, CHUNK), BF16),
                        pltpu.VMEM((tm, D_MODEL), BF16),
                        pltpu.VMEM((tm, 5 * HALF), F32),
                        pltpu.VMEM((tm, HALF), BF16),
                        pltpu.VMEM((SC_HALO + tm, HALF), F32),
                        pltpu.VMEM((tm, D_MODEL), BF16)],
        compiler_params=pltpu.CompilerParams(dimension_semantics=("arbitrary", "arbitrary"),
                                             vmem_limit_bytes=VMEM_LIMIT),
        name="mix_even",
    )(h3, norm_w, w_in, ln_g, ln_b, w_s, b_full, sc_w, w_out)


def _mix_odd_kernel(zs_ref, h_ref, nw_ref, win_ref, pw_ref, ps_ref, cvw_ref, cvb_ref, lng_ref,
                    lnb_ref, wout_ref, o_ref, win_scr, wout_scr, pw_scr, hn_ref, p_ref, sa_ref,
                    sb_ref, ca_ref, cb_ref, y_ref):
    tm = MIX_TM_ODD
    n = HALO + tm
    i = pl.program_id(0)
    j = pl.program_id(1)

    @pl.when((i == 0) & (j == 0))
    def _():
        _cast_rows(win_scr, win_ref)
        _cast_rows(wout_scr, wout_ref)
        for g in range(N_GROUPS):
            pw_scr[g] = pw_ref[0, g].astype(BF16)

    @pl.when(j == 0)
    def _():
        p_ref[0:HALO, :] = jnp.zeros((HALO, HALF), F32)
        ca_ref[0:HALO, :] = jnp.zeros((HALO, GLU_FC), F32)
        cb_ref[0:HALO, :] = jnp.zeros((HALO, GLU_FC), F32)

    x = h_ref[0]
    hn_ref[...] = _rmsnorm(x, nw_ref[0]).astype(BF16)
    p_ref[HALO:n, :] = _dot(hn_ref[...], win_scr[:, 0:HALF])
    c_refs = (ca_ref, cb_ref)
    for c in range(HALF // GLU_FC):
        a = _dot(hn_ref[...], win_scr[:, HALF + c * GLU_FC:HALF + (c + 1) * GLU_FC])
        g = _dot(hn_ref[...], win_scr[:, 2 * HALF + c * GLU_FC:2 * HALF + (c + 1) * GLU_FC])
        c_refs[c][HALO:n, :] = a * _sigmoid(g)

    g1, g2, g3 = GROUP_W, 2 * GROUP_W, 3 * GROUP_W
    sa_ref[8:n, :] = p_ref[8:n, :] + p_ref[7:n - 1, :]
    sb_ref[16:n, g1:] = sa_ref[16:n, g1:] + sa_ref[14:n - 2, g1:]
    sa_ref[24:n, g2:] = sb_ref[24:n, g2:] + sb_ref[20:n - 4, g2:]
    sb_ref[32:n, g3:] = sa_ref[32:n, g3:] + sa_ref[24:n - 8, g3:]
    pos = j * tm + lax.broadcasted_iota(jnp.int32, (tm, GROUP_W), 0)
    for gi, win in enumerate(POOL_WINDOWS):
        gsl = slice(gi * GROUP_W, (gi + 1) * GROUP_W)
        s_ref = sa_ref if gi % 2 == 0 else sb_ref
        count = jnp.minimum(pos + 1, win).astype(F32)
        pooled = s_ref[HALO:n, gsl] / count - p_ref[HALO:n, gsl]
        yc = _dot(pooled.astype(BF16), pw_scr[gi]) * ps_ref[0][:, gsl]
        y_ref[:, gsl] = yc.astype(BF16)
    p_ref[0:HALO, :] = p_ref[tm:n, :]
    o_ref[0] = x + _dot(y_ref[:, 0:HALF], wout_scr[0:HALF, :])

    zero = zs_ref[0]
    for c in range(HALF // GLU_FC):
        c_ref = c_refs[c]
        for r in range(tm // CV_ROWS):
            r0 = pl.multiple_of(zero + r * CV_ROWS, CV_ROWS)
            for lt in range(GLU_FC // LANES):
                lsl = slice(lt * LANES, (lt + 1) * LANES)
                wsl = slice(c * GLU_FC + lt * LANES, c * GLU_FC + (lt + 1) * LANES)
                chunk = c_ref[pl.ds(r0, HALO + CV_ROWS), lsl]
                acc = None
                for b in range(SUBLANES):
                    rolled = chunk if b == 0 else pltpu.roll(chunk, b, axis=0)
                    for a in range(HALO // SUBLANES):
                        d = SUBLANES * a + b
                        if d >= CV_WIDTH:
                            continue
                        lo = HALO - SUBLANES * a
                        src = rolled[lo:lo + CV_ROWS].reshape(CV_ROWS // SUBLANES, SUBLANES, LANES)
                        term = cvw_ref[CV_WIDTH - 1 - d, :, wsl][None] * src
                        acc = term if acc is None else acc + term
                acc = acc + cvb_ref[0][:, wsl][None]
                c_ref[pl.ds(r0 + n, CV_ROWS), lsl] = acc.reshape(CV_ROWS, LANES)
    for c_ref in c_refs:
        c_ref[0:HALO, :] = c_ref[tm:n, :]
    hc = jnp.concatenate([c_ref[n:n + tm, :] for c_ref in c_refs], axis=-1)
    ln = _layernorm(hc, lng_ref[0], lnb_ref[0])
    y_ref[:, HALF:2 * HALF] = _silu(ln).astype(BF16)

    o_ref[0] = o_ref[0] + _dot(y_ref[:, HALF:2 * HALF], wout_scr[HALF:2 * HALF, :])


def _mix_odd_call(h3, layer, j, norm_w, w_in, pool_w, pool_scale, cv_w8, cv_b, ln_g, ln_b, w_out):
    b, s, _ = h3.shape
    tm = MIX_TM_ODD
    tiles = s // tm
    row_spec = pl.BlockSpec((1, tm, D_MODEL), lambda i, t: (i, t, 0))
    vec = _const_spec((1, 1, HALF), (j, 0, 0))
    in_specs = [pl.BlockSpec(memory_space=pltpu.SMEM), row_spec,
                _const_spec((1, 1, D_MODEL), (layer, 0, 0)),
                _const_spec((1, D_MODEL, 3 * HALF), (j, 0, 0)),
                _const_spec((1, N_GROUPS, GROUP_W, GROUP_W), (j, 0, 0, 0)), vec,
                _const_spec((CV_WIDTH, SUBLANES, HALF)), vec, vec, vec,
                _const_spec((1, D_MODEL, D_MODEL), (j, 0, 0))]
    return pl.pallas_call(
        _mix_odd_kernel,
        grid=(b, tiles),
        in_specs=in_specs,
        out_specs=row_spec,
        out_shape=jax.ShapeDtypeStruct(h3.shape, F32),
        scratch_shapes=[pltpu.VMEM((D_MODEL, 3 * HALF), BF16),
                        pltpu.VMEM((D_MODEL, D_MODEL), BF16),
                        pltpu.VMEM((N_GROUPS, GROUP_W, GROUP_W), BF16),
                        pltpu.VMEM((tm, D_MODEL), BF16),
                        pltpu.VMEM((HALO + tm, HALF), F32),
                        pltpu.VMEM((HALO + tm, HALF), F32),
                        pltpu.VMEM((HALO + tm, HALF), F32),
                        pltpu.VMEM((HALO + 2 * tm, GLU_FC), F32),
                        pltpu.VMEM((HALO + 2 * tm, GLU_FC), F32),
                        pltpu.VMEM((tm, D_MODEL), BF16)],
        compiler_params=pltpu.CompilerParams(dimension_semantics=("arbitrary", "arbitrary"),
                                             vmem_limit_bytes=VMEM_LIMIT),
        name="mix_odd",
    )(jnp.zeros((1,), jnp.int32), h3, norm_w, w_in, pool_w, pool_scale, cv_w8, cv_b, ln_g, ln_b,
      w_out)


def kernel(x, p, ffn1_norm, ffn1_w_gu, ffn1_w_down, mix_norm, ffn2_norm, ffn2_w_gu, ffn2_w_down,
           ple_norm, ple_w_gate, ple_w_up, ab_w_in, gm_ln_g, gm_ln_b, gm_w_s, gm_b_s, sc_w,
           ab_w_out, cd_w_in, pool_w, pool_scale, cv_w, cv_b, cv_ln_g, cv_ln_b, cd_w_out,
           final_norm):
    bsz, seq, d = x.shape
    depth = p.shape[0]
    t = bsz * seq
    vec3 = lambda v: v.reshape(v.shape[0], 1, v.shape[1])
    p3 = p.reshape(depth, t, PLE_DIM)
    h = x.reshape(t, d)
    for i in range(depth):
        j = i // 2
        h = _ffn_call(h, i, vec3(ffn1_norm), ffn1_w_gu, ffn1_w_down)
        h3 = h.reshape(bsz, seq, d)
        if i % 2 == 0:
            b_full = jnp.repeat(gm_b_s[j].T, GROUP_W, axis=1)
            h3 = _mix_even_call(h3, i, j, vec3(mix_norm), ab_w_in, vec3(gm_ln_g), vec3(gm_ln_b),
                                gm_w_s, b_full, sc_w, ab_w_out)
        else:
            cv_w8 = jnp.broadcast_to(cv_w[j][:, None, :], (CV_WIDTH, SUBLANES, HALF))
            h3 = _mix_odd_call(h3, i, j, vec3(mix_norm), cd_w_in, pool_w, vec3(pool_scale), cv_w8,
                               vec3(cv_b), vec3(cv_ln_g), vec3(cv_ln_b), cd_w_out)
        h = h3.reshape(t, d)
        h = _ffn_call(h, i, vec3(ffn2_norm), ffn2_w_gu, ffn2_w_down,
                      ple=(p3, vec3(ple_norm), ple_w_gate, ple_w_up),
                      final_norm=final_norm.reshape(1, d) if i == depth - 1 else None)
    return h.reshape(bsz, seq, d)
```

```python
import functools

import jax
import jax.numpy as jnp
from jax import lax
from jax.experimental import pallas as pl
from jax.experimental.pallas import tpu as pltpu

F32 = jnp.float32
BF16 = jnp.bfloat16

D_MODEL = 1024
PLE_DIM = 256
D_FF = 2816
HALF = 512
LANES = 128
SUBLANES = 8
GROUP_W = 128
N_GROUPS = 4
CHUNK = 128
POOL_WINDOWS = (2, 4, 8, 16)
SC_WIDTH = 3
CV_WIDTH = 31
RMS_EPS = 1e-6
LN_EPS = 1e-5

FFN_TM = 512
FFN_TM_PLAIN = 1024
FFN_FC = 256
FFN_NCH = D_FF // FFN_FC
MIX_TM = 1024
MIX_TM_ODD = 1024
GLU_FC = 256
CAST_ROWS = 256
SC_HALO = 8
HALO = 32
CV_ROWS = 128
VMEM_LIMIT = 52 * 1024 * 1024


def _rmsnorm(x, g):
    ms = jnp.mean(x * x, axis=-1, keepdims=True)
    return x * lax.rsqrt(ms + RMS_EPS) * g


def _layernorm(x, g, b):
    mu = jnp.mean(x, axis=-1, keepdims=True)
    xc = x - mu
    var = jnp.mean(xc * xc, axis=-1, keepdims=True)
    return xc * lax.rsqrt(var + LN_EPS) * g + b


def _sigmoid(x):
    return 0.5 + 0.5 * jnp.tanh(0.5 * x)


def _silu(x):
    hx = 0.5 * x
    return hx + hx * jnp.tanh(hx)


def _gelu_tanh(x):
    c = 0.7978845608028654
    return 0.5 * x * (1.0 + jnp.tanh(c * (x + 0.044715 * (x * x * x))))


def _dot(a, b):
    return jnp.dot(a, b, preferred_element_type=F32)


def _cast_rows(dst_ref, src_ref):
    rows = src_ref.shape[1]
    step = min(rows, CAST_ROWS)
    for r in range(0, rows, step):
        dst_ref[r:r + step, :] = src_ref[0, r:r + step, :].astype(BF16)


def _const_spec(shape, index=None):
    index = (0,) * len(shape) if index is None else index
    return pl.BlockSpec(shape, lambda *_: index, pipeline_mode=pl.Buffered(1))


def _ffn_kernel(*refs, with_ple, with_final):
    h_ref, nw_ref, wg_blk, wu_blk, wd_blk = refs[:5]
    k = 5
    if with_ple:
        p_ref, pn_ref, wgate_ref, wup_ref = refs[k:k + 4]
        k += 4
    if with_final:
        fn_ref = refs[k]
        k += 1
    o_ref, xn_ref, a_ref, wg_scr, wu_scr, wd_scr = refs[k:k + 6]
    k += 6
    if with_ple:
        wgate_scr, wup_scr = refs[k:k + 2]
    s = pl.program_id(0)

    def normalize():
        xn_ref[...] = _rmsnorm(h_ref[...], nw_ref[0]).astype(BF16)

    def up_chunk(c):
        sl = slice(c * FFN_FC, (c + 1) * FFN_FC)
        g = _dot(xn_ref[...], wg_scr[c])
        u = _dot(xn_ref[...], wu_scr[c])
        a_ref[:, sl] = (_silu(g) * u).astype(BF16)

    def finish():
        h1 = h_ref[...] + 0.5 * _dot(a_ref[...], wd_scr[...])
        if with_ple:
            xn_ref[...] = _rmsnorm(h1, pn_ref[0]).astype(BF16)
            gate = _sigmoid(_dot(xn_ref[...], wgate_scr[...]))
            up = _dot(p_ref[0].astype(BF16), wup_scr[...])
            h1 = h1 + gate * up
        if with_final:
            h1 = _rmsnorm(h1, fn_ref[...])
        o_ref[...] = h1

    @pl.when(s == 0)
    def _():
        normalize()
        if with_ple:
            _cast_rows(wgate_scr, wgate_ref)
            _cast_rows(wup_scr, wup_ref)

    for c in range(FFN_NCH):
        @pl.when(s == c)
        def _(c=c):
            wg_scr[c] = wg_blk[0].astype(BF16)
            wu_scr[c] = wu_blk[0].astype(BF16)
            wd_scr[c * FFN_FC:(c + 1) * FFN_FC, :] = wd_blk[0].astype(BF16)
            up_chunk(c)
            if c == FFN_NCH - 1:
                finish()

    @pl.when(s >= FFN_NCH)
    def _():
        normalize()
        for c in range(FFN_NCH):
            up_chunk(c)
        finish()


def _ffn_call(h, layer, norm_w, w_gu, w_d, ple=None, final_norm=None):
    t = h.shape[0]
    tm = FFN_TM if (ple is not None or layer > 0) else FFN_TM_PLAIN
    last = FFN_NCH - 1
    tile = lambda s: jnp.maximum(s - last, 0)
    chunk = lambda s: jnp.minimum(s, last)
    row_spec = pl.BlockSpec((tm, D_MODEL), lambda s: (tile(s), 0))
    in_specs = [row_spec, _const_spec((1, 1, D_MODEL), (layer, 0, 0)),
                pl.BlockSpec((1, D_MODEL, FFN_FC), lambda s: (layer, 0, chunk(s))),
                pl.BlockSpec((1, D_MODEL, FFN_FC), lambda s: (layer, 0, FFN_NCH + chunk(s))),
                pl.BlockSpec((1, FFN_FC, D_MODEL), lambda s: (layer, chunk(s), 0))]
    args = [h, norm_w, w_gu, w_gu, w_d]
    scratch = [pltpu.VMEM((tm, D_MODEL), BF16),
               pltpu.VMEM((tm, D_FF), BF16),
               pltpu.VMEM((FFN_NCH, D_MODEL, FFN_FC), BF16),
               pltpu.VMEM((FFN_NCH, D_MODEL, FFN_FC), BF16),
               pltpu.VMEM((D_FF, D_MODEL), BF16)]
    if ple is not None:
        p, pn, wgate, wup = ple
        in_specs += [pl.BlockSpec((1, tm, PLE_DIM), lambda s: (layer, tile(s), 0)),
                     _const_spec((1, 1, D_MODEL), (layer, 0, 0)),
                     _const_spec((1, D_MODEL, D_MODEL), (layer, 0, 0)),
                     _const_spec((1, PLE_DIM, D_MODEL), (layer, 0, 0))]
        args += [p, pn, wgate, wup]
        scratch += [pltpu.VMEM((D_MODEL, D_MODEL), BF16), pltpu.VMEM((PLE_DIM, D_MODEL), BF16)]
    if final_norm is not None:
        in_specs.append(_const_spec((1, D_MODEL)))
        args.append(final_norm)
    return pl.pallas_call(
        functools.partial(_ffn_kernel, with_ple=ple is not None, with_final=final_norm is not None),
        grid=(last + t // tm,),
        in_specs=in_specs,
        out_specs=row_spec,
        out_shape=jax.ShapeDtypeStruct((t, D_MODEL), F32),
        scratch_shapes=scratch,
        compiler_params=pltpu.CompilerParams(dimension_semantics=("arbitrary",),
                                             vmem_limit_bytes=VMEM_LIMIT),
        name="ffn_ple" if ple is not None else "ffn",
    )(*args)


def _mix_even_kernel(h_ref, nw_ref, win_ref, lng_ref, lnb_ref, ws_ref, bs_ref, scw_ref, wout_ref,
                     o_ref, win_scr, wout_scr, wt_scr, hn_ref, u_ref, bg_ref, vn_ref, q_ref, y_ref):
    tm = MIX_TM
    i = pl.program_id(0)
    j = pl.program_id(1)

    @pl.when((i == 0) & (j == 0))
    def _():
        _cast_rows(win_scr, win_ref)
        _cast_rows(wout_scr, wout_ref)
        row = lax.broadcasted_iota(jnp.int32, (CHUNK, CHUNK), 0)
        col = lax.broadcasted_iota(jnp.int32, (CHUNK, CHUNK), 1)
        for g in range(N_GROUPS):
            wt_scr[g] = jnp.where(col <= row, ws_ref[0, g], 0.0).astype(BF16)

    @pl.when(j == 0)
    def _():
        q_ref[0:SC_HALO, :] = jnp.zeros((SC_HALO, HALF), F32)

    x = h_ref[0]
    hn_ref[...] = _rmsnorm(x, nw_ref[0]).astype(BF16)

    v = _gelu_tanh(_dot(hn_ref[...], win_scr[:, HALF:2 * HALF]))
    for g in range(N_GROUPS):
        gsl = slice(g * GROUP_W, (g + 1) * GROUP_W)
        vn_ref[:, gsl] = _layernorm(v[:, gsl], lng_ref[0][:, gsl], lnb_ref[0][:, gsl]).astype(BF16)
    u_ref[...] = _dot(hn_ref[...], win_scr[:, 0:HALF])
    q_ref[SC_HALO:SC_HALO + tm, :] = (_dot(hn_ref[...], win_scr[:, 3 * HALF:4 * HALF])
                                      * _dot(hn_ref[...], win_scr[:, 4 * HALF:5 * HALF]))
    bg_ref[...] = _dot(hn_ref[...], win_scr[:, 2 * HALF:3 * HALF])
    for g in range(N_GROUPS):
        gsl = slice(g * GROUP_W, (g + 1) * GROUP_W)
        for c in range(tm // CHUNK):
            rsl = slice(c * CHUNK, (c + 1) * CHUNK)
            mixed = _dot(wt_scr[g], vn_ref[rsl, gsl]) + bs_ref[:, gsl]
            y_ref[rsl, gsl] = (_gelu_tanh(u_ref[rsl, gsl]) * mixed).astype(BF16)
    o_ref[0] = x + _dot(y_ref[:, 0:HALF], wout_scr[0:HALF, :])

    conv = scw_ref[0, 2:3, :] * q_ref[SC_HALO:SC_HALO + tm, :]
    conv = conv + scw_ref[0, 1:2, :] * q_ref[SC_HALO - 1:SC_HALO - 1 + tm, :]
    conv = conv + scw_ref[0, 0:1, :] * q_ref[SC_HALO - 2:SC_HALO - 2 + tm, :]
    y_ref[:, HALF:2 * HALF] = (bg_ref[...] * conv).astype(BF16)
    q_ref[0:SC_HALO, :] = q_ref[tm:tm + SC_HALO, :]

    o_ref[0] = o_ref[0] + _dot(y_ref[:, HALF:2 * HALF], wout_scr[HALF:2 * HALF, :])


def _mix_even_call(h3, layer, j, norm_w, w_in, ln_g, ln_b, w_s, b_full, sc_w, w_out):
    b, s, _ = h3.shape
    tm = MIX_TM
    row_spec = pl.BlockSpec((1, tm, D_MODEL), lambda i, t: (i, t, 0))
    in_specs = [row_spec, _const_spec((1, 1, D_MODEL), (layer, 0, 0)),
                _const_spec((1, D_MODEL, 5 * HALF), (j, 0, 0)),
                _const_spec((1, 1, HALF), (j, 0, 0)), _const_spec((1, 1, HALF), (j, 0, 0)),
                _const_spec((1, N_GROUPS, CHUNK, CHUNK), (j, 0, 0, 0)), _const_spec((CHUNK, HALF)),
                _const_spec((1, SC_WIDTH, HALF), (j, 0, 0)),
                _const_spec((1, D_MODEL, D_MODEL), (j, 0, 0))]
    return pl.pallas_call(
        _mix_even_kernel,
        grid=(b, s // tm),
        in_specs=in_specs,
        out_specs=row_spec,
        out_shape=jax.ShapeDtypeStruct(h3.shape, F32),
        scratch_shapes=[pltpu.VMEM((D_MODEL, 5 * HALF), BF16),
                        pltpu.VMEM((D_MODEL, D_MODEL), BF16),
                        pltpu.VMEM((N_GROUPS, CHUNK, CHUNK), BF16),
                        pltpu.VMEM((tm, D_MODEL), BF16),
                        pltpu.VMEM((tm, HALF), F32),
                        pltpu.VMEM((tm, HALF), F32),
                        pltpu.VMEM((tm, HALF), BF16),
                        pltpu.VMEM((SC_HALO + tm, HALF), F32),
                        pltpu.VMEM((tm, D_MODEL), BF16)],
        compiler_params=pltpu.CompilerParams(dimension_semantics=("arbitrary", "arbitrary"),
                                             vmem_limit_bytes=VMEM_LIMIT),
        name="mix_even",
    )(h3, norm_w, w_in, ln_g, ln_b, w_s, b_full, sc_w, w_out)


def _mix_odd_kernel(zs_ref, h_ref, nw_ref, win_ref, pw_ref, ps_ref, cvw_ref, cvb_ref, lng_ref,
                    lnb_ref, wout_ref, o_ref, win_scr, wout_scr, pw_scr, hn_ref, p_ref, sa_ref,
                    sb_ref, ca_ref, cb_ref, y_ref):
    tm = MIX_TM_ODD
    n = HALO + tm
    i = pl.program_id(0)
    j = pl.program_id(1)

    @pl.when((i == 0) & (j == 0))
    def _():
        _cast_rows(win_scr, win_ref)
        _cast_rows(wout_scr, wout_ref)
        for g in range(N_GROUPS):
            pw_scr[g] = pw_ref[0, g].astype(BF16)

    @pl.when(j == 0)
    def _():
        p_ref[0:HALO, :] = jnp.zeros((HALO, HALF), F32)
        ca_ref[0:HALO, :] = jnp.zeros((HALO, GLU_FC), F32)
        cb_ref[0:HALO, :] = jnp.zeros((HALO, GLU_FC), F32)

    x = h_ref[0]
    hn_ref[...] = _rmsnorm(x, nw_ref[0]).astype(BF16)
    p_ref[HALO:n, :] = _dot(hn_ref[...], win_scr[:, 0:HALF])
    c_refs = (ca_ref, cb_ref)
    for c in range(HALF // GLU_FC):
        a = _dot(hn_ref[...], win_scr[:, HALF + c * GLU_FC:HALF + (c + 1) * GLU_FC])
        g = _dot(hn_ref[...], win_scr[:, 2 * HALF + c * GLU_FC:2 * HALF + (c + 1) * GLU_FC])
        c_refs[c][HALO:n, :] = a * _sigmoid(g)

    g1, g2, g3 = GROUP_W, 2 * GROUP_W, 3 * GROUP_W
    sa_ref[8:n, :] = p_ref[8:n, :] + p_ref[7:n - 1, :]
    sb_ref[16:n, g1:] = sa_ref[16:n, g1:] + sa_ref[14:n - 2, g1:]
    sa_ref[24:n, g2:] = sb_ref[24:n, g2:] + sb_ref[20:n - 4, g2:]
    sb_ref[32:n, g3:] = sa_ref[32:n, g3:] + sa_ref[24:n - 8, g3:]
    pos = j * tm + lax.broadcasted_iota(jnp.int32, (tm, GROUP_W), 0)
    for gi, win in enumerate(POOL_WINDOWS):
        gsl = slice(gi * GROUP_W, (gi + 1) * GROUP_W)
        s_ref = sa_ref if gi % 2 == 0 else sb_ref
        count = jnp.minimum(pos + 1, win).astype(F32)
        pooled = s_ref[HALO:n, gsl] / count - p_ref[HALO:n, gsl]
        yc = _dot(pooled.astype(BF16), pw_scr[gi]) * ps_ref[0][:, gsl]
        y_ref[:, gsl] = yc.astype(BF16)
    p_ref[0:HALO, :] = p_ref[tm:n, :]
    o_ref[0] = x + _dot(y_ref[:, 0:HALF], wout_scr[0:HALF, :])

    zero = zs_ref[0]
    for c in range(HALF // GLU_FC):
        c_ref = c_refs[c]
        for r in range(tm // CV_ROWS):
            r0 = pl.multiple_of(zero + r * CV_ROWS, CV_ROWS)
            for lt in range(GLU_FC // LANES):
                lsl = slice(lt * LANES, (lt + 1) * LANES)
                wsl = slice(c * GLU_FC + lt * LANES, c * GLU_FC + (lt + 1) * LANES)
                chunk = c_ref[pl.ds(r0, HALO + CV_ROWS), lsl]
                acc = None
                for b in range(SUBLANES):
                    rolled = chunk if b == 0 else pltpu.roll(chunk, b, axis=0)
                    for a in range(HALO // SUBLANES):
                        d = SUBLANES * a + b
                        if d >= CV_WIDTH:
                            continue
                        lo = HALO - SUBLANES * a
                        src = rolled[lo:lo + CV_ROWS].reshape(CV_ROWS // SUBLANES, SUBLANES, LANES)
                        term = cvw_ref[CV_WIDTH - 1 - d, :, wsl][None] * src
                        acc = term if acc is None else acc + term
                acc = acc + cvb_ref[0][:, wsl][None]
                c_ref[pl.ds(r0 + n, CV_ROWS), lsl] = acc.reshape(CV_ROWS, LANES)
    for c_ref in c_refs:
        c_ref[0:HALO, :] = c_ref[tm:n, :]
    hc = jnp.concatenate([c_ref[n:n + tm, :] for c_ref in c_refs], axis=-1)
    ln = _layernorm(hc, lng_ref[0], lnb_ref[0])
    y_ref[:, HALF:2 * HALF] = _silu(ln).astype(BF16)

    o_ref[0] = o_ref[0] + _dot(y_ref[:, HALF:2 * HALF], wout_scr[HALF:2 * HALF, :])


def _mix_odd_call(h3, layer, j, norm_w, w_in, pool_w, pool_scale, cv_w8, cv_b, ln_g, ln_b, w_out):
    b, s, _ = h3.shape
    tm = MIX_TM_ODD
    tiles = s // tm
    row_spec = pl.BlockSpec((1, tm, D_MODEL), lambda i, t: (i, t, 0))
    vec = _const_spec((1, 1, HALF), (j, 0, 0))
    in_specs = [pl.BlockSpec(memory_space=pltpu.SMEM), row_spec,
                _const_spec((1, 1, D_MODEL), (layer, 0, 0)),
                _const_spec((1, D_MODEL, 3 * HALF), (j, 0, 0)),
                _const_spec((1, N_GROUPS, GROUP_W, GROUP_W), (j, 0, 0, 0)), vec,
                _const_spec((CV_WIDTH, SUBLANES, HALF)), vec, vec, vec,
                _const_spec((1, D_MODEL, D_MODEL), (j, 0, 0))]
    return pl.pallas_call(
        _mix_odd_kernel,
        grid=(b, tiles),
        in_specs=in_specs,
        out_specs=row_spec,
        out_shape=jax.ShapeDtypeStruct(h3.shape, F32),
        scratch_shapes=[pltpu.VMEM((D_MODEL, 3 * HALF), BF16),
                        pltpu.VMEM((D_MODEL, D_MODEL), BF16),
                        pltpu.VMEM((N_GROUPS, GROUP_W, GROUP_W), BF16),
                        pltpu.VMEM((tm, D_MODEL), BF16),
                        pltpu.VMEM((HALO + tm, HALF), F32),
                        pltpu.VMEM((HALO + tm, HALF), F32),
                        pltpu.VMEM((HALO + tm, HALF), F32),
                        pltpu.VMEM((HALO + 2 * tm, GLU_FC), F32),
                        pltpu.VMEM((HALO + 2 * tm, GLU_FC), F32),
                        pltpu.VMEM((tm, D_MODEL), BF16)],
        compiler_params=pltpu.CompilerParams(dimension_semantics=("arbitrary", "arbitrary"),
                                             vmem_limit_bytes=VMEM_LIMIT),
        name="mix_odd",
    )(jnp.zeros((1,), jnp.int32), h3, norm_w, w_in, pool_w, pool_scale, cv_w8, cv_b, ln_g, ln_b,
      w_out)


def kernel(x, p, ffn1_norm, ffn1_w_gu, ffn1_w_down, mix_norm, ffn2_norm, ffn2_w_gu, ffn2_w_down,
           ple_norm, ple_w_gate, ple_w_up, ab_w_in, gm_ln_g, gm_ln_b, gm_w_s, gm_b_s, sc_w,
           ab_w_out, cd_w_in, pool_w, pool_scale, cv_w, cv_b, cv_ln_g, cv_ln_b, cd_w_out,
           final_norm):
    bsz, seq, d = x.shape
    depth = p.shape[0]
    t = bsz * seq
    vec3 = lambda v: v.reshape(v.shape[0], 1, v.shape[1])
    p3 = p.reshape(depth, t, PLE_DIM)
    h = x.reshape(t, d)
    for i in range(depth):
        j = i // 2
        h = _ffn_call(h, i, vec3(ffn1_norm), ffn1_w_gu, ffn1_w_down)
        h3 = h.reshape(bsz, seq, d)
        if i % 2 == 0:
            b_full = jnp.repeat(gm_b_s[j].T, GROUP_W, axis=1)
            h3 = _mix_even_call(h3, i, j, vec3(mix_norm), ab_w_in, vec3(gm_ln_g), vec3(gm_ln_b),
                                gm_w_s, b_full, sc_w, ab_w_out)
        else:
            cv_w8 = jnp.broadcast_to(cv_w[j][:, None, :], (CV_WIDTH, SUBLANES, HALF))
            h3 = _mix_odd_call(h3, i, j, vec3(mix_norm), cd_w_in, pool_w, vec3(pool_scale), cv_w8,
                               vec3(cv_b), vec3(cv_ln_g), vec3(cv_ln_b), cd_w_out)
        h = h3.reshape(t, d)
        h = _ffn_call(h, i, vec3(ffn2_norm), ffn2_w_gu, ffn2_w_down,
                      ple=(p3, vec3(ple_norm), ple_w_gate, ple_w_up),
                      final_norm=final_norm.reshape(1, d) if i == depth - 1 else None)
    return h.reshape(bsz, seq, d)
```

```python
import functools

import jax
import jax.numpy as jnp
from jax import lax
from jax.experimental import pallas as pl
from jax.experimental.pallas import tpu as pltpu

F32 = jnp.float32
BF16 = jnp.bfloat16

D_MODEL = 1024
PLE_DIM = 256
D_FF = 2816
HALF = 512
LANES = 128
SUBLANES = 8
GROUP_W = 128
N_GROUPS = 4
CHUNK = 128
POOL_WINDOWS = (2, 4, 8, 16)
SC_WIDTH = 3
CV_WIDTH = 31
RMS_EPS = 1e-6
LN_EPS = 1e-5

FFN_TM = 512
FFN_TM_PLAIN = 1024
FFN_FC = 256
FFN_NCH = D_FF // FFN_FC
MIX_TM = 1024
MIX_TM_ODD = 1024
GLU_FC = 256
CAST_ROWS = 256
SC_HALO = 8
HALO = 32
CV_ROWS = 128
VMEM_LIMIT = 52 * 1024 * 1024


def _rmsnorm(x, g):
    ms = jnp.mean(x * x, axis=-1, keepdims=True)
    return x * lax.rsqrt(ms + RMS_EPS) * g


def _layernorm(x, g, b):
    mu = jnp.mean(x, axis=-1, keepdims=True)
    xc = x - mu
    var = jnp.mean(xc * xc, axis=-1, keepdims=True)
    return xc * lax.rsqrt(var + LN_EPS) * g + b


def _sigmoid(x):
    return 0.5 + 0.5 * jnp.tanh(0.5 * x)


def _silu(x):
    hx = 0.5 * x
    return hx + hx * jnp.tanh(hx)


def _gelu_tanh(x):
    c = 0.7978845608028654
    return 0.5 * x * (1.0 + jnp.tanh(c * (x + 0.044715 * (x * x * x))))


def _dot(a, b):
    return jnp.dot(a, b, preferred_element_type=F32)


def _cast_rows(dst_ref, src_ref):
    rows = src_ref.shape[1]
    step = min(rows, CAST_ROWS)
    for r in range(0, rows, step):
        dst_ref[r:r + step, :] = src_ref[0, r:r + step, :].astype(BF16)


def _const_spec(shape, index=None):
    index = (0,) * len(shape) if index is None else index
    return pl.BlockSpec(shape, lambda *_: index, pipeline_mode=pl.Buffered(1))


def _ffn_kernel(*refs, with_ple, with_final):
    h_ref, nw_ref, wg_blk, wu_blk, wd_blk = refs[:5]
    k = 5
    if with_ple:
        p_ref, pn_ref, wgate_ref, wup_ref = refs[k:k + 4]
        k += 4
    if with_final:
        fn_ref = refs[k]
        k += 1
    o_ref, xn_ref, a_ref, wg_scr, wu_scr, wd_scr = refs[k:k + 6]
    k += 6
    if with_ple:
        wgate_scr, wup_scr = refs[k:k + 2]
    s = pl.program_id(0)

    def normalize():
        xn_ref[...] = _rmsnorm(h_ref[...], nw_ref[0]).astype(BF16)

    def up_chunk(c):
        sl = slice(c * FFN_FC, (c + 1) * FFN_FC)
        g = _dot(xn_ref[...], wg_scr[c])
        u = _dot(xn_ref[...], wu_scr[c])
        a_ref[:, sl] = (_silu(g) * u).astype(BF16)

    def finish():
        h1 = h_ref[...] + 0.5 * _dot(a_ref[...], wd_scr[...])
        if with_ple:
            xn_ref[...] = _rmsnorm(h1, pn_ref[0]).astype(BF16)
            gate = _sigmoid(_dot(xn_ref[...], wgate_scr[...]))
            up = _dot(p_ref[0].astype(BF16), wup_scr[...])
            h1 = h1 + gate * up
        if with_final:
            h1 = _rmsnorm(h1, fn_ref[...])
        o_ref[...] = h1

    @pl.when(s == 0)
    def _():
        normalize()
        if with_ple:
            _cast_rows(wgate_scr, wgate_ref)
            _cast_rows(wup_scr, wup_ref)

    for c in range(FFN_NCH):
        @pl.when(s == c)
        def _(c=c):
            wg_scr[c] = wg_blk[0].astype(BF16)
            wu_scr[c] = wu_blk[0].astype(BF16)
            wd_scr[c * FFN_FC:(c + 1) * FFN_FC, :] = wd_blk[0].astype(BF16)
            up_chunk(c)
            if c == FFN_NCH - 1:
                finish()

    @pl.when(s >= FFN_NCH)
    def _():
        normalize()
        for c in range(FFN_NCH):
            up_chunk(c)
        finish()


def _ffn_call(h, layer, norm_w, w_gu, w_d, ple=None, final_norm=None):
    t = h.shape[0]
    tm = FFN_TM if ple is not None else (FFN_TM_PLAIN if layer == 0 else 256)
    last = FFN_NCH - 1
    tile = lambda s: jnp.maximum(s - last, 0)
    chunk = lambda s: jnp.minimum(s, last)
    row_spec = pl.BlockSpec((tm, D_MODEL), lambda s: (tile(s), 0))
    in_specs = [row_spec, _const_spec((1, 1, D_MODEL), (layer, 0, 0)),
                pl.BlockSpec((1, D_MODEL, FFN_FC), lambda s: (layer, 0, chunk(s))),
                pl.BlockSpec((1, D_MODEL, FFN_FC), lambda s: (layer, 0, FFN_NCH + chunk(s))),
                pl.BlockSpec((1, FFN_FC, D_MODEL), lambda s: (layer, chunk(s), 0))]
    args = [h, norm_w, w_gu, w_gu, w_d]
    scratch = [pltpu.VMEM((tm, D_MODEL), BF16),
               pltpu.VMEM((tm, D_FF), BF16),
               pltpu.VMEM((FFN_NCH, D_MODEL, FFN_FC), BF16),
               pltpu.VMEM((FFN_NCH, D_MODEL, FFN_FC), BF16),
               pltpu.VMEM((D_FF, D_MODEL), BF16)]
    if ple is not None:
        p, pn, wgate, wup = ple
        in_specs += [pl.BlockSpec((1, tm, PLE_DIM), lambda s: (layer, tile(s), 0)),
                     _const_spec((1, 1, D_MODEL), (layer, 0, 0)),
                     _const_spec((1, D_MODEL, D_MODEL), (layer, 0, 0)),
                     _const_spec((1, PLE_DIM, D_MODEL), (layer, 0, 0))]
        args += [p, pn, wgate, wup]
        scratch += [pltpu.VMEM((D_MODEL, D_MODEL), BF16), pltpu.VMEM((PLE_DIM, D_MODEL), BF16)]
    if final_norm is not None:
        in_specs.append(_const_spec((1, D_MODEL)))
        args.append(final_norm)
    return pl.pallas_call(
        functools.partial(_ffn_kernel, with_ple=ple is not None, with_final=final_norm is not None),
        grid=(last + t // tm,),
        in_specs=in_specs,
        out_specs=row_spec,
        out_shape=jax.ShapeDtypeStruct((t, D_MODEL), F32),
        scratch_shapes=scratch,
        compiler_params=pltpu.CompilerParams(dimension_semantics=("arbitrary",),
                                             vmem_limit_bytes=VMEM_LIMIT),
        name="ffn_ple" if ple is not None else "ffn",
    )(*args)


def _mix_even_kernel(h_ref, nw_ref, win_ref, lng_ref, lnb_ref, ws_ref, bs_ref, scw_ref, wout_ref,
                     o_ref, win_scr, wout_scr, wt_scr, hn_ref, u_ref, bg_ref, vn_ref, q_ref, y_ref):
    tm = MIX_TM
    i = pl.program_id(0)
    j = pl.program_id(1)

    @pl.when((i == 0) & (j == 0))
    def _():
        _cast_rows(win_scr, win_ref)
        _cast_rows(wout_scr, wout_ref)
        row = lax.broadcasted_iota(jnp.int32, (CHUNK, CHUNK), 0)
        col = lax.broadcasted_iota(jnp.int32, (CHUNK, CHUNK), 1)
        for g in range(N_GROUPS):
            wt_scr[g] = jnp.where(col <= row, ws_ref[0, g], 0.0).astype(BF16)

    @pl.when(j == 0)
    def _():
        q_ref[0:SC_HALO, :] = jnp.zeros((SC_HALO, HALF), F32)

    x = h_ref[0]
    hn_ref[...] = _rmsnorm(x, nw_ref[0]).astype(BF16)

    v = _gelu_tanh(_dot(hn_ref[...], win_scr[:, HALF:2 * HALF]))
    for g in range(N_GROUPS):
        gsl = slice(g * GROUP_W, (g + 1) * GROUP_W)
        vn_ref[:, gsl] = _layernorm(v[:, gsl], lng_ref[0][:, gsl], lnb_ref[0][:, gsl]).astype(BF16)
    u_ref[...] = _dot(hn_ref[...], win_scr[:, 0:HALF])
    q_ref[SC_HALO:SC_HALO + tm, :] = (_dot(hn_ref[...], win_scr[:, 3 * HALF:4 * HALF])
                                      * _dot(hn_ref[...], win_scr[:, 4 * HALF:5 * HALF]))
    bg_ref[...] = _dot(hn_ref[...], win_scr[:, 2 * HALF:3 * HALF])
    for g in range(N_GROUPS):
        gsl = slice(g * GROUP_W, (g + 1) * GROUP_W)
        for c in range(tm // CHUNK):
            rsl = slice(c * CHUNK, (c + 1) * CHUNK)
            mixed = _dot(wt_scr[g], vn_ref[rsl, gsl]) + bs_ref[:, gsl]
            y_ref[rsl, gsl] = (_gelu_tanh(u_ref[rsl, gsl]) * mixed).astype(BF16)
    o_ref[0] = x + _dot(y_ref[:, 0:HALF], wout_scr[0:HALF, :])

    conv = scw_ref[0, 2:3, :] * q_ref[SC_HALO:SC_HALO + tm, :]
    conv = conv + scw_ref[0, 1:2, :] * q_ref[SC_HALO - 1:SC_HALO - 1 + tm, :]
    conv = conv + scw_ref[0, 0:1, :] * q_ref[SC_HALO - 2:SC_HALO - 2 + tm, :]
    y_ref[:, HALF:2 * HALF] = (bg_ref[...] * conv).astype(BF16)
    q_ref[0:SC_HALO, :] = q_ref[tm:tm + SC_HALO, :]

    o_ref[0] = o_ref[0] + _dot(y_ref[:, HALF:2 * HALF], wout_scr[HALF:2 * HALF, :])


def _mix_even_call(h3, layer, j, norm_w, w_in, ln_g, ln_b, w_s, b_full, sc_w, w_out):
    b, s, _ = h3.shape
    tm = MIX_TM
    row_spec = pl.BlockSpec((1, tm, D_MODEL), lambda i, t: (i, t, 0))
    in_specs = [row_spec, _const_spec((1, 1, D_MODEL), (layer, 0, 0)),
                _const_spec((1, D_MODEL, 5 * HALF), (j, 0, 0)),
                _const_spec((1, 1, HALF), (j, 0, 0)), _const_spec((1, 1, HALF), (j, 0, 0)),
                _const_spec((1, N_GROUPS, CHUNK, CHUNK), (j, 0, 0, 0)), _const_spec((CHUNK, HALF)),
                _const_spec((1, SC_WIDTH, HALF), (j, 0, 0)),
                _const_spec((1, D_MODEL, D_MODEL), (j, 0, 0))]
    return pl.pallas_call(
        _mix_even_kernel,
        grid=(b, s // tm),
        in_specs=in_specs,
        out_specs=row_spec,
        out_shape=jax.ShapeDtypeStruct(h3.shape, F32),
        scratch_shapes=[pltpu.VMEM((D_MODEL, 5 * HALF), BF16),
                        pltpu.VMEM((D_MODEL, D_MODEL), BF16),
                        pltpu.VMEM((N_GROUPS, CHUNK, CHUNK), BF16),
                        pltpu.VMEM((tm, D_MODEL), BF16),
                        pltpu.VMEM((tm, HALF), F32),
                        pltpu.VMEM((tm, HALF), F32),
                        pltpu.VMEM((tm, HALF), BF16),
                        pltpu.VMEM((SC_HALO + tm, HALF), F32),
                        pltpu.VMEM((tm, D_MODEL), BF16)],
        compiler_params=pltpu.CompilerParams(dimension_semantics=("arbitrary", "arbitrary"),
                                             vmem_limit_bytes=VMEM_LIMIT),
        name="mix_even",
    )(h3, norm_w, w_in, ln_g, ln_b, w_s, b_full, sc_w, w_out)


def _mix_odd_kernel(zs_ref, h_ref, nw_ref, win_ref, pw_ref, ps_ref, cvw_ref, cvb_ref, lng_ref,
                    lnb_ref, wout_ref, o_ref, win_scr, wout_scr, pw_scr, hn_ref, p_ref, sa_ref,
                    sb_ref, ca_ref, cb_ref, y_ref):
    tm = MIX_TM_ODD
    n = HALO + tm
    i = pl.program_id(0)
    j = pl.program_id(1)

    @pl.when((i == 0) & (j == 0))
    def _():
        _cast_rows(win_scr, win_ref)
        _cast_rows(wout_scr, wout_ref)
        for g in range(N_GROUPS):
            pw_scr[g] = pw_ref[0, g].astype(BF16)

    @pl.when(j == 0)
    def _():
        p_ref[0:HALO, :] = jnp.zeros((HALO, HALF), F32)
        ca_ref[0:HALO, :] = jnp.zeros((HALO, GLU_FC), F32)
        cb_ref[0:HALO, :] = jnp.zeros((HALO, GLU_FC), F32)

    x = h_ref[0]
    hn_ref[...] = _rmsnorm(x, nw_ref[0]).astype(BF16)
    p_ref[HALO:n, :] = _dot(hn_ref[...], win_scr[:, 0:HALF])
    c_refs = (ca_ref, cb_ref)
    for c in range(HALF // GLU_FC):
        a = _dot(hn_ref[...], win_scr[:, HALF + c * GLU_FC:HALF + (c + 1) * GLU_FC])
        g = _dot(hn_ref[...], win_scr[:, 2 * HALF + c * GLU_FC:2 * HALF + (c + 1) * GLU_FC])
        c_refs[c][HALO:n, :] = a * _sigmoid(g)

    g1, g2, g3 = GROUP_W, 2 * GROUP_W, 3 * GROUP_W
    sa_ref[8:n, :] = p_ref[8:n, :] + p_ref[7:n - 1, :]
    sb_ref[16:n, g1:] = sa_ref[16:n, g1:] + sa_ref[14:n - 2, g1:]
    sa_ref[24:n, g2:] = sb_ref[24:n, g2:] + sb_ref[20:n - 4, g2:]
    sb_ref[32:n, g3:] = sa_ref[32:n, g3:] + sa_ref[24:n - 8, g3:]
    pos = j * tm + lax.broadcasted_iota(jnp.int32, (tm, GROUP_W), 0)
    for gi, win in enumerate(POOL_WINDOWS):
        gsl = slice(gi * GROUP_W, (gi + 1) * GROUP_W)
        s_ref = sa_ref if gi % 2 == 0 else sb_ref
        count = jnp.minimum(pos + 1, win).astype(F32)
        pooled = s_ref[HALO:n, gsl] / count - p_ref[HALO:n, gsl]
        yc = _dot(pooled.astype(BF16), pw_scr[gi]) * ps_ref[0][:, gsl]
        y_ref[:, gsl] = yc.astype(BF16)
    p_ref[0:HALO, :] = p_ref[tm:n, :]
    o_ref[0] = x + _dot(y_ref[:, 0:HALF], wout_scr[0:HALF, :])

    zero = zs_ref[0]
    for c in range(HALF // GLU_FC):
        c_ref = c_refs[c]
        for r in range(tm // CV_ROWS):
            r0 = pl.multiple_of(zero + r * CV_ROWS, CV_ROWS)
            for lt in range(GLU_FC // LANES):
                lsl = slice(lt * LANES, (lt + 1) * LANES)
                wsl = slice(c * GLU_FC + lt * LANES, c * GLU_FC + (lt + 1) * LANES)
                chunk = c_ref[pl.ds(r0, HALO + CV_ROWS), lsl]
                acc = None
                for b in range(SUBLANES):
                    rolled = chunk if b == 0 else pltpu.roll(chunk, b, axis=0)
                    for a in range(HALO // SUBLANES):
                        d = SUBLANES * a + b
                        if d >= CV_WIDTH:
                            continue
                        lo = HALO - SUBLANES * a
                        src = rolled[lo:lo + CV_ROWS].reshape(CV_ROWS // SUBLANES, SUBLANES, LANES)
                        term = cvw_ref[CV_WIDTH - 1 - d, :, wsl][None] * src
                        acc = term if acc is None else acc + term
                acc = acc + cvb_ref[0][:, wsl][None]
                c_ref[pl.ds(r0 + n, CV_ROWS), lsl] = acc.reshape(CV_ROWS, LANES)
    for c_ref in c_refs:
        c_ref[0:HALO, :] = c_ref[tm:n, :]
    hc = jnp.concatenate([c_ref[n:n + tm, :] for c_ref in c_refs], axis=-1)
    ln = _layernorm(hc, lng_ref[0], lnb_ref[0])
    y_ref[:, HALF:2 * HALF] = _silu(ln).astype(BF16)

    o_ref[0] = o_ref[0] + _dot(y_ref[:, HALF:2 * HALF], wout_scr[HALF:2 * HALF, :])


def _mix_odd_call(h3, layer, j, norm_w, w_in, pool_w, pool_scale, cv_w8, cv_b, ln_g, ln_b, w_out):
    b, s, _ = h3.shape
    tm = MIX_TM_ODD
    tiles = s // tm
    row_spec = pl.BlockSpec((1, tm, D_MODEL), lambda i, t: (i, t, 0))
    vec = _const_spec((1, 1, HALF), (j, 0, 0))
    in_specs = [pl.BlockSpec(memory_space=pltpu.SMEM), row_spec,
                _const_spec((1, 1, D_MODEL), (layer, 0, 0)),
                _const_spec((1, D_MODEL, 3 * HALF), (j, 0, 0)),
                _const_spec((1, N_GROUPS, GROUP_W, GROUP_W), (j, 0, 0, 0)), vec,
                _const_spec((CV_WIDTH, SUBLANES, HALF)), vec, vec, vec,
                _const_spec((1, D_MODEL, D_MODEL), (j, 0, 0))]
    return pl.pallas_call(
        _mix_odd_kernel,
        grid=(b, tiles),
        in_specs=in_specs,
        out_specs=row_spec,
        out_shape=jax.ShapeDtypeStruct(h3.shape, F32),
        scratch_shapes=[pltpu.VMEM((D_MODEL, 3 * HALF), BF16),
                        pltpu.VMEM((D_MODEL, D_MODEL), BF16),
                        pltpu.VMEM((N_GROUPS, GROUP_W, GROUP_W), BF16),
                        pltpu.VMEM((tm, D_MODEL), BF16),
                        pltpu.VMEM((HALO + tm, HALF), F32),
                        pltpu.VMEM((HALO + tm, HALF), F32),
                        pltpu.VMEM((HALO + tm, HALF), F32),
                        pltpu.VMEM((HALO + 2 * tm, GLU_FC), F32),
                        pltpu.VMEM((HALO + 2 * tm, GLU_FC), F32),
                        pltpu.VMEM((tm, D_MODEL), BF16)],
        compiler_params=pltpu.CompilerParams(dimension_semantics=("arbitrary", "arbitrary"),
                                             vmem_limit_bytes=VMEM_LIMIT),
        name="mix_odd",
    )(jnp.zeros((1,), jnp.int32), h3, norm_w, w_in, pool_w, pool_scale, cv_w8, cv_b, ln_g, ln_b,
      w_out)


def kernel(x, p, ffn1_norm, ffn1_w_gu, ffn1_w_down, mix_norm, ffn2_norm, ffn2_w_gu, ffn2_w_down,
           ple_norm, ple_w_gate, ple_w_up, ab_w_in, gm_ln_g, gm_ln_b, gm_w_s, gm_b_s, sc_w,
           ab_w_out, cd_w_in, pool_w, pool_scale, cv_w, cv_b, cv_ln_g, cv_ln_b, cd_w_out,
           final_norm):
    bsz, seq, d = x.shape
    depth = p.shape[0]
    t = bsz * seq
    vec3 = lambda v: v.reshape(v.shape[0], 1, v.shape[1])
    p3 = p.reshape(depth, t, PLE_DIM)
    h = x.reshape(t, d)
    for i in range(depth):
        j = i // 2
        h = _ffn_call(h, i, vec3(ffn1_norm), ffn1_w_gu, ffn1_w_down)
        h3 = h.reshape(bsz, seq, d)
        if i % 2 == 0:
            b_full = jnp.repeat(gm_b_s[j].T, GROUP_W, axis=1)
            h3 = _mix_even_call(h3, i, j, vec3(mix_norm), ab_w_in, vec3(gm_ln_g), vec3(gm_ln_b),
                                gm_w_s, b_full, sc_w, ab_w_out)
        else:
            cv_w8 = jnp.broadcast_to(cv_w[j][:, None, :], (CV_WIDTH, SUBLANES, HALF))
            h3 = _mix_odd_call(h3, i, j, vec3(mix_norm), cd_w_in, pool_w, vec3(pool_scale), cv_w8,
                               vec3(cv_b), vec3(cv_ln_g), vec3(cv_ln_b), cd_w_out)
        h = h3.reshape(t, d)
        h = _ffn_call(h, i, vec3(ffn2_norm), ffn2_w_gu, ffn2_w_down,
                      ple=(p3, vec3(ple_norm), ple_w_gate, ple_w_up),
                      final_norm=final_norm.reshape(1, d) if i == depth - 1 else None)
    return h.reshape(bsz, seq, d)
```

```python
import functools

import jax
import jax.numpy as jnp
from jax import lax
from jax.experimental import pallas as pl
from jax.experimental.pallas import tpu as pltpu

F32 = jnp.float32
BF16 = jnp.bfloat16

D_MODEL = 1024
PLE_DIM = 256
D_FF = 2816
HALF = 512
LANES = 128
SUBLANES = 8
GROUP_W = 128
N_GROUPS = 4
CHUNK = 128
POOL_WINDOWS = (2, 4, 8, 16)
SC_WIDTH = 3
CV_WIDTH = 31
RMS_EPS = 1e-6
LN_EPS = 1e-5

FFN_TM = 512
FFN_TM_FIRST = 1024
FFN_FC = 256
FFN_NCH = D_FF // FFN_FC
MIX_TM = 1024
MIX_TM_ODD = 1024
GLU_FC = 256
CAST_ROWS = 256
SC_HALO = 8
HALO = 32
CV_ROWS = 128
VMEM_LIMIT = 52 * 1024 * 1024


def _rms_scale(x):
    return lax.rsqrt(jnp.mean(x * x, axis=-1, keepdims=True) + RMS_EPS)


def _rmsnorm(x, g):
    return x * _rms_scale(x) * g


def _layernorm(x, g, b):
    mu = jnp.mean(x, axis=-1, keepdims=True)
    xc = x - mu
    var = jnp.mean(xc * xc, axis=-1, keepdims=True)
    return xc * lax.rsqrt(var + LN_EPS) * g + b


def _sigmoid(x):
    return 0.5 + 0.5 * jnp.tanh(0.5 * x)


def _silu(x):
    hx = 0.5 * x
    return hx + hx * jnp.tanh(hx)


def _gelu_tanh(x):
    c = 0.7978845608028654
    return 0.5 * x * (1.0 + jnp.tanh(c * (x + 0.044715 * (x * x * x))))


def _dot(a, b):
    return jnp.dot(a, b, preferred_element_type=F32)


def _cast_rows(dst_ref, src_ref):
    rows = src_ref.shape[1]
    step = min(rows, CAST_ROWS)
    for r in range(0, rows, step):
        dst_ref[r:r + step, :] = src_ref[0, r:r + step, :].astype(BF16)


def _const_spec(shape, index=None):
    index = (0,) * len(shape) if index is None else index
    return pl.BlockSpec(shape, lambda *_: index, pipeline_mode=pl.Buffered(1))


def _ffn_kernel(*refs, with_ple, with_final):
    h_ref, nw_ref, wg_blk, wu_blk, wd_blk = refs[:5]
    k = 5
    if with_ple:
        p_ref, pn_ref, wgate_ref, wup_ref = refs[k:k + 4]
        k += 4
    if with_final:
        fn_ref = refs[k]
        k += 1
    o_ref, xn_ref, r_ref, a_ref, wg_scr, wu_scr, wd_scr = refs[k:k + 7]
    k += 7
    if with_ple:
        wgate_scr, wup_scr = refs[k:k + 2]
    s = pl.program_id(0)

    def normalize():
        x = h_ref[...]
        xn_ref[...] = (x * nw_ref[0]).astype(BF16)
        r_ref[...] = jnp.broadcast_to(_rms_scale(x), r_ref.shape)

    def up_chunk(c):
        sl = slice(c * FFN_FC, (c + 1) * FFN_FC)
        g = r_ref[...] * _dot(xn_ref[...], wg_scr[c])
        u = r_ref[...] * _dot(xn_ref[...], wu_scr[c])
        a_ref[:, sl] = (_silu(g) * u).astype(BF16)

    def finish():
        h1 = h_ref[...] + 0.5 * _dot(a_ref[...], wd_scr[...])
        if with_ple:
            xn_ref[...] = (h1 * pn_ref[0]).astype(BF16)
            gate = _sigmoid(_rms_scale(h1) * _dot(xn_ref[...], wgate_scr[...]))
            up = _dot(p_ref[0].astype(BF16), wup_scr[...])
            h1 = h1 + gate * up
        if with_final:
            h1 = _rmsnorm(h1, fn_ref[...])
        o_ref[...] = h1

    @pl.when(s == 0)
    def _():
        normalize()
        if with_ple:
            _cast_rows(wgate_scr, wgate_ref)
            _cast_rows(wup_scr, wup_ref)

    for c in range(FFN_NCH):
        @pl.when(s == c)
        def _(c=c):
            wg_scr[c] = wg_blk[0].astype(BF16)
            wu_scr[c] = wu_blk[0].astype(BF16)
            wd_scr[c * FFN_FC:(c + 1) * FFN_FC, :] = wd_blk[0].astype(BF16)
            up_chunk(c)
            if c == FFN_NCH - 1:
                finish()

    @pl.when(s >= FFN_NCH)
    def _():
        normalize()
        for c in range(FFN_NCH):
            up_chunk(c)
        finish()


def _ffn_tile(layer, with_ple):
    return FFN_TM_FIRST if (layer == 0 and not with_ple) else FFN_TM


def _ffn_call(h, layer, norm_w, w_gu, w_d, ple=None, final_norm=None):
    t = h.shape[0]
    tm = _ffn_tile(layer, ple is not None)
    assert t % tm == 0 and D_FF % FFN_FC == 0
    last = FFN_NCH - 1
    tile = lambda s: jnp.maximum(s - last, 0)
    chunk = lambda s: jnp.minimum(s, last)
    row_spec = pl.BlockSpec((tm, D_MODEL), lambda s: (tile(s), 0))
    in_specs = [row_spec, _const_spec((1, 1, D_MODEL), (layer, 0, 0)),
                pl.BlockSpec((1, D_MODEL, FFN_FC), lambda s: (layer, 0, chunk(s))),
                pl.BlockSpec((1, D_MODEL, FFN_FC), lambda s: (layer, 0, FFN_NCH + chunk(s))),
                pl.BlockSpec((1, FFN_FC, D_MODEL), lambda s: (layer, chunk(s), 0))]
    args = [h, norm_w, w_gu, w_gu, w_d]
    scratch = [pltpu.VMEM((tm, D_MODEL), BF16),
               pltpu.VMEM((tm, FFN_FC), F32),
               pltpu.VMEM((tm, D_FF), BF16),
               pltpu.VMEM((FFN_NCH, D_MODEL, FFN_FC), BF16),
               pltpu.VMEM((FFN_NCH, D_MODEL, FFN_FC), BF16),
               pltpu.VMEM((D_FF, D_MODEL), BF16)]
    if ple is not None:
        p, pn, wgate, wup = ple
        in_specs += [pl.BlockSpec((1, tm, PLE_DIM), lambda s: (layer, tile(s), 0)),
                     _const_spec((1, 1, D_MODEL), (layer, 0, 0)),
                     _const_spec((1, D_MODEL, D_MODEL), (layer, 0, 0)),
                     _const_spec((1, PLE_DIM, D_MODEL), (layer, 0, 0))]
        args += [p, pn, wgate, wup]
        scratch += [pltpu.VMEM((D_MODEL, D_MODEL), BF16), pltpu.VMEM((PLE_DIM, D_MODEL), BF16)]
    if final_norm is not None:
        in_specs.append(_const_spec((1, D_MODEL)))
        args.append(final_norm)
    return pl.pallas_call(
        functools.partial(_ffn_kernel, with_ple=ple is not None, with_final=final_norm is not None),
        grid=(last + t // tm,),
        in_specs=in_specs,
        out_specs=row_spec,
        out_shape=jax.ShapeDtypeStruct((t, D_MODEL), F32),
        scratch_shapes=scratch,
        compiler_params=pltpu.CompilerParams(dimension_semantics=("arbitrary",),
                                             vmem_limit_bytes=VMEM_LIMIT),
        name="ffn_ple" if ple is not None else "ffn",
    )(*args)


def _mix_even_kernel(h_ref, nw_ref, win_ref, lng_ref, lnb_ref, ws_ref, bs_ref, scw_ref, wout_ref,
                     o_ref, win_scr, wout_scr, wt_scr, hn_ref, u_ref, bg_ref, vn_ref, q_ref, y_ref):
    tm = MIX_TM
    i = pl.program_id(0)
    j = pl.program_id(1)

    @pl.when((i == 0) & (j == 0))
    def _():
        _cast_rows(win_scr, win_ref)
        _cast_rows(wout_scr, wout_ref)
        row = lax.broadcasted_iota(jnp.int32, (CHUNK, CHUNK), 0)
        col = lax.broadcasted_iota(jnp.int32, (CHUNK, CHUNK), 1)
        for g in range(N_GROUPS):
            wt_scr[g] = jnp.where(col <= row, ws_ref[0, g], 0.0).astype(BF16)

    @pl.when(j == 0)
    def _():
        q_ref[0:SC_HALO, :] = jnp.zeros((SC_HALO, HALF), F32)

    x = h_ref[0]
    hn_ref[...] = _rmsnorm(x, nw_ref[0]).astype(BF16)

    v = _gelu_tanh(_dot(hn_ref[...], win_scr[:, HALF:2 * HALF]))
    for g in range(N_GROUPS):
        gsl = slice(g * GROUP_W, (g + 1) * GROUP_W)
        vn_ref[:, gsl] = _layernorm(v[:, gsl], lng_ref[0][:, gsl], lnb_ref[0][:, gsl]).astype(BF16)
    u_ref[...] = _dot(hn_ref[...], win_scr[:, 0:HALF])
    q_ref[SC_HALO:SC_HALO + tm, :] = (_dot(hn_ref[...], win_scr[:, 3 * HALF:4 * HALF])
                                      * _dot(hn_ref[...], win_scr[:, 4 * HALF:5 * HALF]))
    bg_ref[...] = _dot(hn_ref[...], win_scr[:, 2 * HALF:3 * HALF])
    for g in range(N_GROUPS):
        gsl = slice(g * GROUP_W, (g + 1) * GROUP_W)
        for c in range(tm // CHUNK):
            rsl = slice(c * CHUNK, (c + 1) * CHUNK)
            mixed = _dot(wt_scr[g], vn_ref[rsl, gsl]) + bs_ref[:, gsl]
            y_ref[rsl, gsl] = (_gelu_tanh(u_ref[rsl, gsl]) * mixed).astype(BF16)
    o_ref[0] = x + _dot(y_ref[:, 0:HALF], wout_scr[0:HALF, :])

    conv = scw_ref[0, 2:3, :] * q_ref[SC_HALO:SC_HALO + tm, :]
    conv = conv + scw_ref[0, 1:2, :] * q_ref[SC_HALO - 1:SC_HALO - 1 + tm, :]
    conv = conv + scw_ref[0, 0:1, :] * q_ref[SC_HALO - 2:SC_HALO - 2 + tm, :]
    y_ref[:, HALF:2 * HALF] = (bg_ref[...] * conv).astype(BF16)
    q_ref[0:SC_HALO, :] = q_ref[tm:tm + SC_HALO, :]

    o_ref[0] = o_ref[0] + _dot(y_ref[:, HALF:2 * HALF], wout_scr[HALF:2 * HALF, :])


def _mix_even_call(h3, layer, j, norm_w, w_in, ln_g, ln_b, w_s, b_full, sc_w, w_out):
    b, s, _ = h3.shape
    tm = MIX_TM
    assert s % tm == 0 and tm % CHUNK == 0 and SC_HALO >= SC_WIDTH - 1
    row_spec = pl.BlockSpec((1, tm, D_MODEL), lambda i, t: (i, t, 0))
    in_specs = [row_spec, _const_spec((1, 1, D_MODEL), (layer, 0, 0)),
                _const_spec((1, D_MODEL, 5 * HALF), (j, 0, 0)),
                _const_spec((1, 1, HALF), (j, 0, 0)), _const_spec((1, 1, HALF), (j, 0, 0)),
                _const_spec((1, N_GROUPS, CHUNK, CHUNK), (j, 0, 0, 0)), _const_spec((CHUNK, HALF)),
                _const_spec((1, SC_WIDTH, HALF), (j, 0, 0)),
                _const_spec((1, D_MODEL, D_MODEL), (j, 0, 0))]
    return pl.pallas_call(
        _mix_even_kernel,
        grid=(b, s // tm),
        in_specs=in_specs,
        out_specs=row_spec,
        out_shape=jax.ShapeDtypeStruct(h3.shape, F32),
        scratch_shapes=[pltpu.VMEM((D_MODEL, 5 * HALF), BF16),
                        pltpu.VMEM((D_MODEL, D_MODEL), BF16),
                        pltpu.VMEM((N_GROUPS, CHUNK, CHUNK), BF16),
                        pltpu.VMEM((tm, D_MODEL), BF16),
                        pltpu.VMEM((tm, HALF), F32),
                        pltpu.VMEM((tm, HALF), F32),
                        pltpu.VMEM((tm, HALF), BF16),
                        pltpu.VMEM((SC_HALO + tm, HALF), F32),
                        pltpu.VMEM((tm, D_MODEL), BF16)],
        compiler_params=pltpu.CompilerParams(dimension_semantics=("arbitrary", "arbitrary"),
                                             vmem_limit_bytes=VMEM_LIMIT),
        name="mix_even",
    )(h3, norm_w, w_in, ln_g, ln_b, w_s, b_full, sc_w, w_out)


def _mix_odd_kernel(zs_ref, h_ref, nw_ref, win_ref, pw_ref, ps_ref, cvw_ref, cvb_ref, lng_ref,
                    lnb_ref, wout_ref, o_ref, win_scr, wout_scr, pw_scr, hn_ref, p_ref, sa_ref,
                    sb_ref, ca_ref, cb_ref, y_ref):
    tm = MIX_TM_ODD
    n = HALO + tm
    i = pl.program_id(0)
    j = pl.program_id(1)

    @pl.when((i == 0) & (j == 0))
    def _():
        _cast_rows(win_scr, win_ref)
        _cast_rows(wout_scr, wout_ref)
        for g in range(N_GROUPS):
            pw_scr[g] = pw_ref[0, g].astype(BF16)

    @pl.when(j == 0)
    def _():
        p_ref[0:HALO, :] = jnp.zeros((HALO, HALF), F32)
        ca_ref[0:HALO, :] = jnp.zeros((HALO, GLU_FC), F32)
        cb_ref[0:HALO, :] = jnp.zeros((HALO, GLU_FC), F32)

    x = h_ref[0]
    hn_ref[...] = _rmsnorm(x, nw_ref[0]).astype(BF16)
    p_ref[HALO:n, :] = _dot(hn_ref[...], win_scr[:, 0:HALF])
    c_refs = (ca_ref, cb_ref)
    for c in range(HALF // GLU_FC):
        a = _dot(hn_ref[...], win_scr[:, HALF + c * GLU_FC:HALF + (c + 1) * GLU_FC])
        g = _dot(hn_ref[...], win_scr[:, 2 * HALF + c * GLU_FC:2 * HALF + (c + 1) * GLU_FC])
        c_refs[c][HALO:n, :] = a * _sigmoid(g)

    g1, g2, g3 = GROUP_W, 2 * GROUP_W, 3 * GROUP_W
    sa_ref[8:n, :] = p_ref[8:n, :] + p_ref[7:n - 1, :]
    sb_ref[16:n, g1:] = sa_ref[16:n, g1:] + sa_ref[14:n - 2, g1:]
    sa_ref[24:n, g2:] = sb_ref[24:n, g2:] + sb_ref[20:n - 4, g2:]
    sb_ref[32:n, g3:] = sa_ref[32:n, g3:] + sa_ref[24:n - 8, g3:]
    head = slice(HALO, HALO + max(POOL_WINDOWS))
    pos = lax.broadcasted_iota(jnp.int32, (max(POOL_WINDOWS), GROUP_W), 0)
    for gi, win in enumerate(POOL_WINDOWS):
        gsl = slice(gi * GROUP_W, (gi + 1) * GROUP_W)
        s_ref = sa_ref if gi % 2 == 0 else sb_ref
        count = jnp.minimum(pos + 1, win).astype(F32)
        s_ref[head, gsl] = s_ref[head, gsl] * jnp.where(j == 0, float(win) / count, 1.0)
        pooled = s_ref[HALO:n, gsl] * (1.0 / win) - p_ref[HALO:n, gsl]
        yc = _dot(pooled.astype(BF16), pw_scr[gi]) * ps_ref[0][:, gsl]
        y_ref[:, gsl] = yc.astype(BF16)
    p_ref[0:HALO, :] = p_ref[tm:n, :]
    o_ref[0] = x + _dot(y_ref[:, 0:HALF], wout_scr[0:HALF, :])

    zero = zs_ref[0]
    for c in range(HALF // GLU_FC):
        c_ref = c_refs[c]
        for r in range(tm // CV_ROWS):
            r0 = pl.multiple_of(zero + r * CV_ROWS, CV_ROWS)
            for lt in range(GLU_FC // LANES):
                lsl = slice(lt * LANES, (lt + 1) * LANES)
                wsl = slice(c * GLU_FC + lt * LANES, c * GLU_FC + (lt + 1) * LANES)
                chunk = c_ref[pl.ds(r0, HALO + CV_ROWS), lsl]
                acc = None
                for b in range(SUBLANES):
                    rolled = chunk if b == 0 else pltpu.roll(chunk, b, axis=0)
                    for a in range(HALO // SUBLANES):
                        d = SUBLANES * a + b
                        if d >= CV_WIDTH:
                            continue
                        lo = HALO - SUBLANES * a
                        src = rolled[lo:lo + CV_ROWS].reshape(CV_ROWS // SUBLANES, SUBLANES, LANES)
                        term = cvw_ref[CV_WIDTH - 1 - d, :, wsl][None] * src
                        acc = term if acc is None else acc + term
                acc = acc + cvb_ref[0][:, wsl][None]
                c_ref[pl.ds(r0 + n, CV_ROWS), lsl] = acc.reshape(CV_ROWS, LANES)
    for c_ref in c_refs:
        c_ref[0:HALO, :] = c_ref[tm:n, :]
    hc = jnp.concatenate([c_ref[n:n + tm, :] for c_ref in c_refs], axis=-1)
    ln = _layernorm(hc, lng_ref[0], lnb_ref[0])
    y_ref[:, HALF:2 * HALF] = _silu(ln).astype(BF16)

    o_ref[0] = o_ref[0] + _dot(y_ref[:, HALF:2 * HALF], wout_scr[HALF:2 * HALF, :])


def _mix_odd_call(h3, layer, j, norm_w, w_in, pool_w, pool_scale, cv_w8, cv_b, ln_g, ln_b, w_out):
    b, s, _ = h3.shape
    tm = MIX_TM_ODD
    tiles = s // tm
    assert s % tm == 0 and tm % CV_ROWS == 0 and HALO >= CV_WIDTH - 1
    assert HALO >= 2 * max(POOL_WINDOWS) - 1 and tm >= max(POOL_WINDOWS)
    row_spec = pl.BlockSpec((1, tm, D_MODEL), lambda i, t: (i, t, 0))
    vec = _const_spec((1, 1, HALF), (j, 0, 0))
    in_specs = [pl.BlockSpec(memory_space=pltpu.SMEM), row_spec,
                _const_spec((1, 1, D_MODEL), (layer, 0, 0)),
                _const_spec((1, D_MODEL, 3 * HALF), (j, 0, 0)),
                _const_spec((1, N_GROUPS, GROUP_W, GROUP_W), (j, 0, 0, 0)), vec,
                _const_spec((CV_WIDTH, SUBLANES, HALF)), vec, vec, vec,
                _const_spec((1, D_MODEL, D_MODEL), (j, 0, 0))]
    return pl.pallas_call(
        _mix_odd_kernel,
        grid=(b, tiles),
        in_specs=in_specs,
        out_specs=row_spec,
        out_shape=jax.ShapeDtypeStruct(h3.shape, F32),
        scratch_shapes=[pltpu.VMEM((D_MODEL, 3 * HALF), BF16),
                        pltpu.VMEM((D_MODEL, D_MODEL), BF16),
                        pltpu.VMEM((N_GROUPS, GROUP_W, GROUP_W), BF16),
                        pltpu.VMEM((tm, D_MODEL), BF16),
                        pltpu.VMEM((HALO + tm, HALF), F32),
                        pltpu.VMEM((HALO + tm, HALF), F32),
                        pltpu.VMEM((HALO + tm, HALF), F32),
                        pltpu.VMEM((HALO + 2 * tm, GLU_FC), F32),
                        pltpu.VMEM((HALO + 2 * tm, GLU_FC), F32),
                        pltpu.VMEM((tm, D_MODEL), BF16)],
        compiler_params=pltpu.CompilerParams(dimension_semantics=("arbitrary", "arbitrary"),
                                             vmem_limit_bytes=VMEM_LIMIT),
        name="mix_odd",
    )(jnp.zeros((1,), jnp.int32), h3, norm_w, w_in, pool_w, pool_scale, cv_w8, cv_b, ln_g, ln_b,
      w_out)


def kernel(x, p, ffn1_norm, ffn1_w_gu, ffn1_w_down, mix_norm, ffn2_norm, ffn2_w_gu, ffn2_w_down,
           ple_norm, ple_w_gate, ple_w_up, ab_w_in, gm_ln_g, gm_ln_b, gm_w_s, gm_b_s, sc_w,
           ab_w_out, cd_w_in, pool_w, pool_scale, cv_w, cv_b, cv_ln_g, cv_ln_b, cd_w_out,
           final_norm):
    bsz, seq, d = x.shape
    depth = p.shape[0]
    t = bsz * seq
    vec3 = lambda v: v.reshape(v.shape[0], 1, v.shape[1])
    p3 = p.reshape(depth, t, PLE_DIM)
    h = x.reshape(t, d)
    for i in range(depth):
        j = i // 2
        h = _ffn_call(h, i, vec3(ffn1_norm), ffn1_w_gu, ffn1_w_down)
        h3 = h.reshape(bsz, seq, d)
        if i % 2 == 0:
            b_full = jnp.repeat(gm_b_s[j].T, GROUP_W, axis=1)
            h3 = _mix_even_call(h3, i, j, vec3(mix_norm), ab_w_in, vec3(gm_ln_g), vec3(gm_ln_b),
                                gm_w_s, b_full, sc_w, ab_w_out)
        else:
            cv_w8 = jnp.broadcast_to(cv_w[j][:, None, :], (CV_WIDTH, SUBLANES, HALF))
            h3 = _mix_odd_call(h3, i, j, vec3(mix_norm), cd_w_in, pool_w, vec3(pool_scale), cv_w8,
                               vec3(cv_b), vec3(cv_ln_g), vec3(cv_ln_b), cd_w_out)
        h = h3.reshape(t, d)
        h = _ffn_call(h, i, vec3(ffn2_norm), ffn2_w_gu, ffn2_w_down,
                      ple=(p3, vec3(ple_norm), ple_w_gate, ple_w_up),
                      final_norm=final_norm.reshape(1, d) if i == depth - 1 else None)
    return h.reshape(bsz, seq, d)
```

```python
import functools
import math

import jax
import jax.numpy as jnp
from jax import lax
from jax.experimental import pallas as pl
from jax.experimental.pallas import tpu as pltpu

F32 = jnp.float32
BF16 = jnp.bfloat16

D_MODEL = 1024
PLE_DIM = 256
D_FF = 2816
HALF = 512
LANES = 128
SUBLANES = 8
GROUP_W = 128
N_GROUPS = 4
CHUNK = 128
POOL_WINDOWS = (2, 4, 8, 16)
SC_WIDTH = 3
CV_WIDTH = 31
RMS_EPS = 1e-6
LN_EPS = 1e-5

FFN_TM = 512
FFN_TM_FIRST = 1024
FFN_FC = 256
FFN_NCH = D_FF // FFN_FC
MIX_TM = 1024
MIX_TM_ODD = 1024
GLU_FC = 256
CAST_ROWS = 256
SC_HALO = 8
HALO = 32
CV_ROWS = 128
SPILL_MARGIN = 6 * 1024 * 1024


def _rms_scale(x):
    return lax.rsqrt(jnp.mean(x * x, axis=-1, keepdims=True) + RMS_EPS)


def _rmsnorm(x, g):
    return x * _rms_scale(x) * g


def _layernorm(x, g, b):
    mu = jnp.mean(x, axis=-1, keepdims=True)
    xc = x - mu
    var = jnp.mean(xc * xc, axis=-1, keepdims=True)
    return xc * lax.rsqrt(var + LN_EPS) * g + b


def _sigmoid(x):
    return 0.5 + 0.5 * jnp.tanh(0.5 * x)


def _silu(x):
    hx = 0.5 * x
    return hx + hx * jnp.tanh(hx)


def _gelu_tanh(x):
    c = 0.7978845608028654
    return 0.5 * x * (1.0 + jnp.tanh(c * (x + 0.044715 * (x * x * x))))


def _dot(a, b):
    return jnp.dot(a, b, preferred_element_type=F32)


def _cast_rows(dst_ref, src_ref):
    rows = src_ref.shape[1]
    step = min(rows, CAST_ROWS)
    for r in range(0, rows, step):
        dst_ref[r:r + step, :] = src_ref[0, r:r + step, :].astype(BF16)


def _const_spec(shape, index=None):
    index = (0,) * len(shape) if index is None else index
    return pl.BlockSpec(shape, lambda *_: index, pipeline_mode=pl.Buffered(1))


def _vmem_limit(args, in_specs, out_spec, out_dtype, scratch):
    nbytes = lambda shape, dtype: math.prod(shape) * jnp.dtype(dtype).itemsize
    total = 2 * nbytes(out_spec.block_shape, out_dtype) + SPILL_MARGIN
    for arg, spec in zip(args, in_specs):
        if spec.block_shape is not None:
            total += (1 if spec.pipeline_mode is not None else 2) * nbytes(spec.block_shape, arg.dtype)
    return total + sum(nbytes(s.shape, s.dtype) for s in scratch)


def _ffn_kernel(*refs, with_ple, with_final):
    h_ref, nw_ref, wg_blk, wu_blk, wd_blk = refs[:5]
    k = 5
    if with_ple:
        p_ref, pn_ref, wgate_ref, wup_ref = refs[k:k + 4]
        k += 4
    if with_final:
        fn_ref = refs[k]
        k += 1
    o_ref, xn_ref, r_ref, a_ref, wg_scr, wu_scr, wd_scr = refs[k:k + 7]
    k += 7
    if with_ple:
        wgate_scr, wup_scr = refs[k:k + 2]
    s = pl.program_id(0)

    def normalize():
        x = h_ref[...]
        xn_ref[...] = (x * nw_ref[0]).astype(BF16)
        r_ref[...] = jnp.broadcast_to(_rms_scale(x), r_ref.shape)

    def up_chunk(c):
        sl = slice(c * FFN_FC, (c + 1) * FFN_FC)
        g = r_ref[...] * _dot(xn_ref[...], wg_scr[c])
        u = r_ref[...] * _dot(xn_ref[...], wu_scr[c])
        a_ref[:, sl] = (_silu(g) * u).astype(BF16)

    def finish():
        h1 = h_ref[...] + 0.5 * _dot(a_ref[...], wd_scr[...])
        if with_ple:
            xn_ref[...] = (h1 * pn_ref[0]).astype(BF16)
            gate = _sigmoid(_rms_scale(h1) * _dot(xn_ref[...], wgate_scr[...]))
            up = _dot(p_ref[0].astype(BF16), wup_scr[...])
            h1 = h1 + gate * up
        if with_final:
            h1 = _rmsnorm(h1, fn_ref[...])
        o_ref[...] = h1

    @pl.when(s == 0)
    def _():
        normalize()
        if with_ple:
            _cast_rows(wgate_scr, wgate_ref)
            _cast_rows(wup_scr, wup_ref)

    for c in range(FFN_NCH):
        @pl.when(s == c)
        def _(c=c):
            wg_scr[c] = wg_blk[0].astype(BF16)
            wu_scr[c] = wu_blk[0].astype(BF16)
            wd_scr[c * FFN_FC:(c + 1) * FFN_FC, :] = wd_blk[0].astype(BF16)
            up_chunk(c)
            if c == FFN_NCH - 1:
                finish()

    @pl.when(s >= FFN_NCH)
    def _():
        normalize()
        for c in range(FFN_NCH):
            up_chunk(c)
        finish()


def _ffn_tile(layer, with_ple):
    return FFN_TM_FIRST if (layer == 0 and not with_ple) else FFN_TM


def _ffn_call(h, layer, norm_w, w_gu, w_d, ple=None, final_norm=None):
    t = h.shape[0]
    tm = _ffn_tile(layer, ple is not None)
    assert t % tm == 0 and D_FF % FFN_FC == 0
    last = FFN_NCH - 1
    tile = lambda s: jnp.maximum(s - last, 0)
    chunk = lambda s: jnp.minimum(s, last)
    row_spec = pl.BlockSpec((tm, D_MODEL), lambda s: (tile(s), 0))
    in_specs = [row_spec, _const_spec((1, 1, D_MODEL), (layer, 0, 0)),
                pl.BlockSpec((1, D_MODEL, FFN_FC), lambda s: (layer, 0, chunk(s))),
                pl.BlockSpec((1, D_MODEL, FFN_FC), lambda s: (layer, 0, FFN_NCH + chunk(s))),
                pl.BlockSpec((1, FFN_FC, D_MODEL), lambda s: (layer, chunk(s), 0))]
    args = [h, norm_w, w_gu, w_gu, w_d]
    scratch = [pltpu.VMEM((tm, D_MODEL), BF16),
               pltpu.VMEM((tm, FFN_FC), F32),
               pltpu.VMEM((tm, D_FF), BF16),
               pltpu.VMEM((FFN_NCH, D_MODEL, FFN_FC), BF16),
               pltpu.VMEM((FFN_NCH, D_MODEL, FFN_FC), BF16),
               pltpu.VMEM((D_FF, D_MODEL), BF16)]
    if ple is not None:
        p, pn, wgate, wup = ple
        in_specs += [pl.BlockSpec((1, tm, PLE_DIM), lambda s: (layer, tile(s), 0)),
                     _const_spec((1, 1, D_MODEL), (layer, 0, 0)),
                     _const_spec((1, D_MODEL, D_MODEL), (layer, 0, 0)),
                     _const_spec((1, PLE_DIM, D_MODEL), (layer, 0, 0))]
        args += [p, pn, wgate, wup]
        scratch += [pltpu.VMEM((D_MODEL, D_MODEL), BF16), pltpu.VMEM((PLE_DIM, D_MODEL), BF16)]
    if final_norm is not None:
        in_specs.append(_const_spec((1, D_MODEL)))
        args.append(final_norm)
    return pl.pallas_call(
        functools.partial(_ffn_kernel, with_ple=ple is not None, with_final=final_norm is not None),
        grid=(last + t // tm,),
        in_specs=in_specs,
        out_specs=row_spec,
        out_shape=jax.ShapeDtypeStruct((t, D_MODEL), F32),
        scratch_shapes=scratch,
        compiler_params=pltpu.CompilerParams(
            dimension_semantics=("arbitrary",),
            vmem_limit_bytes=_vmem_limit(args, in_specs, row_spec, F32, scratch)),
        name="ffn_ple" if ple is not None else "ffn",
    )(*args)


def _mix_even_kernel(h_ref, nw_ref, win_ref, lng_ref, lnb_ref, ws_ref, bs_ref, scw_ref, wout_ref,
                     o_ref, win_scr, wout_scr, wt_scr, hn_ref, u_ref, bg_ref, vn_ref, q_ref, y_ref):
    tm = MIX_TM
    i = pl.program_id(0)
    j = pl.program_id(1)

    @pl.when((i == 0) & (j == 0))
    def _():
        _cast_rows(win_scr, win_ref)
        _cast_rows(wout_scr, wout_ref)
        row = lax.broadcasted_iota(jnp.int32, (CHUNK, CHUNK), 0)
        col = lax.broadcasted_iota(jnp.int32, (CHUNK, CHUNK), 1)
        for g in range(N_GROUPS):
            wt_scr[g] = jnp.where(col <= row, ws_ref[0, g], 0.0).astype(BF16)

    @pl.when(j == 0)
    def _():
        q_ref[0:SC_HALO, :] = jnp.zeros((SC_HALO, HALF), F32)

    x = h_ref[0]
    hn_ref[...] = _rmsnorm(x, nw_ref[0]).astype(BF16)

    v = _gelu_tanh(_dot(hn_ref[...], win_scr[:, HALF:2 * HALF]))
    for g in range(N_GROUPS):
        gsl = slice(g * GROUP_W, (g + 1) * GROUP_W)
        vn_ref[:, gsl] = _layernorm(v[:, gsl], lng_ref[0][:, gsl], lnb_ref[0][:, gsl]).astype(BF16)
    u_ref[...] = _dot(hn_ref[...], win_scr[:, 0:HALF])
    q_ref[SC_HALO:SC_HALO + tm, :] = (_dot(hn_ref[...], win_scr[:, 3 * HALF:4 * HALF])
                                      * _dot(hn_ref[...], win_scr[:, 4 * HALF:5 * HALF]))
    bg_ref[...] = _dot(hn_ref[...], win_scr[:, 2 * HALF:3 * HALF])
    for g in range(N_GROUPS):
        gsl = slice(g * GROUP_W, (g + 1) * GROUP_W)
        for c in range(tm // CHUNK):
            rsl = slice(c * CHUNK, (c + 1) * CHUNK)
            mixed = _dot(wt_scr[g], vn_ref[rsl, gsl]) + bs_ref[:, gsl]
            y_ref[rsl, gsl] = (_gelu_tanh(u_ref[rsl, gsl]) * mixed).astype(BF16)
    o_ref[0] = x + _dot(y_ref[:, 0:HALF], wout_scr[0:HALF, :])

    conv = scw_ref[0, 2:3, :] * q_ref[SC_HALO:SC_HALO + tm, :]
    conv = conv + scw_ref[0, 1:2, :] * q_ref[SC_HALO - 1:SC_HALO - 1 + tm, :]
    conv = conv + scw_ref[0, 0:1, :] * q_ref[SC_HALO - 2:SC_HALO - 2 + tm, :]
    y_ref[:, HALF:2 * HALF] = (bg_ref[...] * conv).astype(BF16)
    q_ref[0:SC_HALO, :] = q_ref[tm:tm + SC_HALO, :]

    o_ref[0] = o_ref[0] + _dot(y_ref[:, HALF:2 * HALF], wout_scr[HALF:2 * HALF, :])


def _mix_even_call(h3, layer, j, norm_w, w_in, ln_g, ln_b, w_s, b_full, sc_w, w_out):
    b, s, _ = h3.shape
    tm = MIX_TM
    assert s % tm == 0 and tm % CHUNK == 0 and SC_HALO >= SC_WIDTH - 1
    row_spec = pl.BlockSpec((1, tm, D_MODEL), lambda i, t: (i, t, 0))
    in_specs = [row_spec, _const_spec((1, 1, D_MODEL), (layer, 0, 0)),
                _const_spec((1, D_MODEL, 5 * HALF), (j, 0, 0)),
                _const_spec((1, 1, HALF), (j, 0, 0)), _const_spec((1, 1, HALF), (j, 0, 0)),
                _const_spec((1, N_GROUPS, CHUNK, CHUNK), (j, 0, 0, 0)), _const_spec((CHUNK, HALF)),
                _const_spec((1, SC_WIDTH, HALF), (j, 0, 0)),
                _const_spec((1, D_MODEL, D_MODEL), (j, 0, 0))]
    args = [h3, norm_w, w_in, ln_g, ln_b, w_s, b_full, sc_w, w_out]
    scratch = [pltpu.VMEM((D_MODEL, 5 * HALF), BF16),
               pltpu.VMEM((D_MODEL, D_MODEL), BF16),
               pltpu.VMEM((N_GROUPS, CHUNK, CHUNK), BF16),
               pltpu.VMEM((tm, D_MODEL), BF16),
               pltpu.VMEM((tm, HALF), F32),
               pltpu.VMEM((tm, HALF), F32),
               pltpu.VMEM((tm, HALF), BF16),
               pltpu.VMEM((SC_HALO + tm, HALF), F32),
               pltpu.VMEM((tm, D_MODEL), BF16)]
    return pl.pallas_call(
        _mix_even_kernel,
        grid=(b, s // tm),
        in_specs=in_specs,
        out_specs=row_spec,
        out_shape=jax.ShapeDtypeStruct(h3.shape, F32),
        scratch_shapes=scratch,
        compiler_params=pltpu.CompilerParams(
            dimension_semantics=("arbitrary", "arbitrary"),
            vmem_limit_bytes=_vmem_limit(args, in_specs, row_spec, F32, scratch)),
        name="mix_even",
    )(*args)


def _mix_odd_kernel(zs_ref, h_ref, nw_ref, win_ref, pw_ref, ps_ref, cvw_ref, cvb_ref, lng_ref,
                    lnb_ref, wout_ref, o_ref, win_scr, wout_scr, pw_scr, hn_ref, p_ref, sa_ref,
                    sb_ref, ca_ref, cb_ref, y_ref):
    tm = MIX_TM_ODD
    n = HALO + tm
    i = pl.program_id(0)
    j = pl.program_id(1)

    @pl.when((i == 0) & (j == 0))
    def _():
        _cast_rows(win_scr, win_ref)
        _cast_rows(wout_scr, wout_ref)
        for g in range(N_GROUPS):
            pw_scr[g] = pw_ref[0, g].astype(BF16)

    @pl.when(j == 0)
    def _():
        p_ref[0:HALO, :] = jnp.zeros((HALO, HALF), F32)
        ca_ref[0:HALO, :] = jnp.zeros((HALO, GLU_FC), F32)
        cb_ref[0:HALO, :] = jnp.zeros((HALO, GLU_FC), F32)

    x = h_ref[0]
    hn_ref[...] = _rmsnorm(x, nw_ref[0]).astype(BF16)
    p_ref[HALO:n, :] = _dot(hn_ref[...], win_scr[:, 0:HALF])
    c_refs = (ca_ref, cb_ref)
    for c in range(HALF // GLU_FC):
        a = _dot(hn_ref[...], win_scr[:, HALF + c * GLU_FC:HALF + (c + 1) * GLU_FC])
        g = _dot(hn_ref[...], win_scr[:, 2 * HALF + c * GLU_FC:2 * HALF + (c + 1) * GLU_FC])
        c_refs[c][HALO:n, :] = a * _sigmoid(g)

    g1, g2, g3 = GROUP_W, 2 * GROUP_W, 3 * GROUP_W
    sa_ref[8:n, :] = p_ref[8:n, :] + p_ref[7:n - 1, :]
    sb_ref[16:n, g1:] = sa_ref[16:n, g1:] + sa_ref[14:n - 2, g1:]
    sa_ref[24:n, g2:] = sb_ref[24:n, g2:] + sb_ref[20:n - 4, g2:]
    sb_ref[32:n, g3:] = sa_ref[32:n, g3:] + sa_ref[24:n - 8, g3:]
    head = slice(HALO, HALO + max(POOL_WINDOWS))
    pos = lax.broadcasted_iota(jnp.int32, (max(POOL_WINDOWS), GROUP_W), 0)
    for gi, win in enumerate(POOL_WINDOWS):
        gsl = slice(gi * GROUP_W, (gi + 1) * GROUP_W)
        s_ref = sa_ref if gi % 2 == 0 else sb_ref
        count = jnp.minimum(pos + 1, win).astype(F32)
        s_ref[head, gsl] = s_ref[head, gsl] * jnp.where(j == 0, float(win) / count, 1.0)
        pooled = s_ref[HALO:n, gsl] * (1.0 / win) - p_ref[HALO:n, gsl]
        yc = _dot(pooled.astype(BF16), pw_scr[gi]) * ps_ref[0][:, gsl]
        y_ref[:, gsl] = yc.astype(BF16)
    p_ref[0:HALO, :] = p_ref[tm:n, :]
    o_ref[0] = x + _dot(y_ref[:, 0:HALF], wout_scr[0:HALF, :])

    zero = zs_ref[0]
    for c in range(HALF // GLU_FC):
        c_ref = c_refs[c]
        for r in range(tm // CV_ROWS):
            r0 = pl.multiple_of(zero + r * CV_ROWS, CV_ROWS)
            for lt in range(GLU_FC // LANES):
                lsl = slice(lt * LANES, (lt + 1) * LANES)
                wsl = slice(c * GLU_FC + lt * LANES, c * GLU_FC + (lt + 1) * LANES)
                chunk = c_ref[pl.ds(r0, HALO + CV_ROWS), lsl]
                acc = None
                for b in range(SUBLANES):
                    rolled = chunk if b == 0 else pltpu.roll(chunk, b, axis=0)
                    for a in range(HALO // SUBLANES):
                        d = SUBLANES * a + b
                        if d >= CV_WIDTH:
                            continue
                        lo = HALO - SUBLANES * a
                        src = rolled[lo:lo + CV_ROWS].reshape(CV_ROWS // SUBLANES, SUBLANES, LANES)
                        term = cvw_ref[CV_WIDTH - 1 - d, :, wsl][None] * src
                        acc = term if acc is None else acc + term
                acc = acc + cvb_ref[0][:, wsl][None]
                c_ref[pl.ds(r0 + n, CV_ROWS), lsl] = acc.reshape(CV_ROWS, LANES)
    for c_ref in c_refs:
        c_ref[0:HALO, :] = c_ref[tm:n, :]
    hc = jnp.concatenate([c_ref[n:n + tm, :] for c_ref in c_refs], axis=-1)
    ln = _layernorm(hc, lng_ref[0], lnb_ref[0])
    y_ref[:, HALF:2 * HALF] = _silu(ln).astype(BF16)

    o_ref[0] = o_ref[0] + _dot(y_ref[:, HALF:2 * HALF], wout_scr[HALF:2 * HALF, :])


def _mix_odd_call(h3, layer, j, norm_w, w_in, pool_w, pool_scale, cv_w8, cv_b, ln_g, ln_b, w_out):
    b, s, _ = h3.shape
    tm = MIX_TM_ODD
    tiles = s // tm
    assert s % tm == 0 and tm % CV_ROWS == 0 and HALO >= CV_WIDTH - 1
    assert HALO >= 2 * max(POOL_WINDOWS) - 1 and tm >= max(POOL_WINDOWS)
    row_spec = pl.BlockSpec((1, tm, D_MODEL), lambda i, t: (i, t, 0))
    vec = _const_spec((1, 1, HALF), (j, 0, 0))
    in_specs = [pl.BlockSpec(memory_space=pltpu.SMEM), row_spec,
                _const_spec((1, 1, D_MODEL), (layer, 0, 0)),
                _const_spec((1, D_MODEL, 3 * HALF), (j, 0, 0)),
                _const_spec((1, N_GROUPS, GROUP_W, GROUP_W), (j, 0, 0, 0)), vec,
                _const_spec((CV_WIDTH, SUBLANES, HALF)), vec, vec, vec,
                _const_spec((1, D_MODEL, D_MODEL), (j, 0, 0))]
    args = [jnp.zeros((1,), jnp.int32), h3, norm_w, w_in, pool_w, pool_scale, cv_w8, cv_b, ln_g, ln_b,
            w_out]
    scratch = [pltpu.VMEM((D_MODEL, 3 * HALF), BF16),
               pltpu.VMEM((D_MODEL, D_MODEL), BF16),
               pltpu.VMEM((N_GROUPS, GROUP_W, GROUP_W), BF16),
               pltpu.VMEM((tm, D_MODEL), BF16),
               pltpu.VMEM((HALO + tm, HALF), F32),
               pltpu.VMEM((HALO + tm, HALF), F32),
               pltpu.VMEM((HALO + tm, HALF), F32),
               pltpu.VMEM((HALO + 2 * tm, GLU_FC), F32),
               pltpu.VMEM((HALO + 2 * tm, GLU_FC), F32),
               pltpu.VMEM((tm, D_MODEL), BF16)]
    return pl.pallas_call(
        _mix_odd_kernel,
        grid=(b, tiles),
        in_specs=in_specs,
        out_specs=row_spec,
        out_shape=jax.ShapeDtypeStruct(h3.shape, F32),
        scratch_shapes=scratch,
        compiler_params=pltpu.CompilerParams(
            dimension_semantics=("arbitrary", "arbitrary"),
            vmem_limit_bytes=_vmem_limit(args, in_specs, row_spec, F32, scratch)),
        name="mix_odd",
    )(*args)


def kernel(x, p, ffn1_norm, ffn1_w_gu, ffn1_w_down, mix_norm, ffn2_norm, ffn2_w_gu, ffn2_w_down,
           ple_norm, ple_w_gate, ple_w_up, ab_w_in, gm_ln_g, gm_ln_b, gm_w_s, gm_b_s, sc_w,
           ab_w_out, cd_w_in, pool_w, pool_scale, cv_w, cv_b, cv_ln_g, cv_ln_b, cd_w_out,
           final_norm):
    bsz, seq, d = x.shape
    depth = p.shape[0]
    t = bsz * seq
    vec3 = lambda v: v.reshape(v.shape[0], 1, v.shape[1])
    p3 = p.reshape(depth, t, PLE_DIM)
    h = x.reshape(t, d)
    for i in range(depth):
        j = i // 2
        h = _ffn_call(h, i, vec3(ffn1_norm), ffn1_w_gu, ffn1_w_down)
        h3 = h.reshape(bsz, seq, d)
        if i % 2 == 0:
            b_full = jnp.repeat(gm_b_s[j].T, GROUP_W, axis=1)
            h3 = _mix_even_call(h3, i, j, vec3(mix_norm), ab_w_in, vec3(gm_ln_g), vec3(gm_ln_b),
                                gm_w_s, b_full, sc_w, ab_w_out)
        else:
            cv_w8 = jnp.broadcast_to(cv_w[j][:, None, :], (CV_WIDTH, SUBLANES, HALF))
            h3 = _mix_odd_call(h3, i, j, vec3(mix_norm), cd_w_in, pool_w, vec3(pool_scale), cv_w8,
                               vec3(cv_b), vec3(cv_ln_g), vec3(cv_ln_b), cd_w_out)
        h = h3.reshape(t, d)
        h = _ffn_call(h, i, vec3(ffn2_norm), ffn2_w_gu, ffn2_w_down,
                      ple=(p3, vec3(ple_norm), ple_w_gate, ple_w_up),
                      final_norm=final_norm.reshape(1, d) if i == depth - 1 else None)
    return h.reshape(bsz, seq, d)
```

```python
import functools
import math

import jax
import jax.numpy as jnp
from jax import lax
from jax.experimental import pallas as pl
from jax.experimental.pallas import tpu as pltpu

F32 = jnp.float32
BF16 = jnp.bfloat16

D_MODEL = 1024
PLE_DIM = 256
D_FF = 2816
HALF = 512
LANES = 128
SUBLANES = 8
GROUP_W = 128
N_GROUPS = 4
CHUNK = 128
POOL_WINDOWS = (2, 4, 8, 16)
SC_WIDTH = 3
CV_WIDTH = 31
RMS_EPS = 1e-6
LN_EPS = 1e-5

FFN_TM = 512
FFN_TM_FIRST = 1024
FFN_FC = 256
FFN_NCH = D_FF // FFN_FC
MIX_TM = 1024
MIX_TM_ODD = 1024
GLU_FC = 256
CAST_ROWS = 256
SC_HALO = 8
HALO = 32
CV_ROWS = 128
SPILL_MARGIN = 6 * 1024 * 1024


def _rms_scale(x):
    return lax.rsqrt(jnp.mean(x * x, axis=-1, keepdims=True) + RMS_EPS)


def _rmsnorm(x, g):
    return x * _rms_scale(x) * g


def _layernorm(x, g, b):
    mu = jnp.mean(x, axis=-1, keepdims=True)
    xc = x - mu
    var = jnp.mean(xc * xc, axis=-1, keepdims=True)
    return xc * lax.rsqrt(var + LN_EPS) * g + b


def _sigmoid(x):
    return 0.5 + 0.5 * jnp.tanh(0.5 * x)


def _silu(x):
    hx = 0.5 * x
    return hx + hx * jnp.tanh(hx)


def _gelu_tanh(x):
    c = 0.7978845608028654
    return 0.5 * x * (1.0 + jnp.tanh(c * (x + 0.044715 * (x * x * x))))


def _dot(a, b):
    return jnp.dot(a, b, preferred_element_type=F32)


def _cast_rows(dst_ref, src_ref):
    rows = src_ref.shape[1]
    step = min(rows, CAST_ROWS)
    for r in range(0, rows, step):
        dst_ref[r:r + step, :] = src_ref[0, r:r + step, :].astype(BF16)


def _const_spec(shape, index=None):
    index = (0,) * len(shape) if index is None else index
    return pl.BlockSpec(shape, lambda *_: index, pipeline_mode=pl.Buffered(1))


def _vmem_limit(args, in_specs, out_spec, out_dtype, scratch):
    nbytes = lambda shape, dtype: math.prod(shape) * jnp.dtype(dtype).itemsize
    total = 2 * nbytes(out_spec.block_shape, out_dtype) + SPILL_MARGIN
    for arg, spec in zip(args, in_specs):
        if spec.block_shape is not None:
            total += (1 if spec.pipeline_mode is not None else 2) * nbytes(spec.block_shape, arg.dtype)
    return total + sum(nbytes(s.shape, s.dtype) for s in scratch)


def _ffn_kernel(*refs, with_ple, with_final):
    h_ref, nw_ref, wg_blk, wu_blk, wd_blk = refs[:5]
    k = 5
    if with_ple:
        p_ref, pn_ref, wgate_ref, wup_ref = refs[k:k + 4]
        k += 4
    if with_final:
        fn_ref = refs[k]
        k += 1
    o_ref, xn_ref, r_ref, a_ref, wg_scr, wu_scr, wd_scr = refs[k:k + 7]
    k += 7
    if with_ple:
        wgate_scr, wup_scr = refs[k:k + 2]
    s = pl.program_id(0)

    def normalize():
        x = h_ref[...]
        xn_ref[...] = (x * nw_ref[0]).astype(BF16)
        r_ref[...] = jnp.broadcast_to(_rms_scale(x), r_ref.shape)

    def up_chunk(c):
        sl = slice(c * FFN_FC, (c + 1) * FFN_FC)
        g = r_ref[...] * _dot(xn_ref[...], wg_scr[c])
        u = r_ref[...] * _dot(xn_ref[...], wu_scr[c])
        a_ref[:, sl] = (_silu(g) * u).astype(BF16)

    def finish():
        h1 = h_ref[...] + 0.5 * _dot(a_ref[...], wd_scr[...])
        if with_ple:
            xn_ref[...] = (h1 * pn_ref[0]).astype(BF16)
            gate = _sigmoid(_rms_scale(h1) * _dot(xn_ref[...], wgate_scr[...]))
            up = _dot(p_ref[0].astype(BF16), wup_scr[...])
            h1 = h1 + gate * up
        if with_final:
            h1 = _rmsnorm(h1, fn_ref[...])
        o_ref[...] = h1

    @pl.when(s == 0)
    def _():
        normalize()
        if with_ple:
            _cast_rows(wgate_scr, wgate_ref)
            _cast_rows(wup_scr, wup_ref)

    for c in range(FFN_NCH):
        @pl.when(s == c)
        def _(c=c):
            wg_scr[c] = wg_blk[0].astype(BF16)
            wu_scr[c] = wu_blk[0].astype(BF16)
            wd_scr[c * FFN_FC:(c + 1) * FFN_FC, :] = wd_blk[0].astype(BF16)
            up_chunk(c)
            if c == FFN_NCH - 1:
                finish()

    @pl.when(s >= FFN_NCH)
    def _():
        normalize()
        for c in range(FFN_NCH):
            up_chunk(c)
        finish()


def _ffn_tile(layer, with_ple):
    return FFN_TM_FIRST if not with_ple else FFN_TM


def _ffn_call(h, layer, norm_w, w_gu, w_d, ple=None, final_norm=None):
    t = h.shape[0]
    tm = _ffn_tile(layer, ple is not None)
    assert t % tm == 0 and D_FF % FFN_FC == 0
    last = FFN_NCH - 1
    tile = lambda s: jnp.maximum(s - last, 0)
    chunk = lambda s: jnp.minimum(s, last)
    row_spec = pl.BlockSpec((tm, D_MODEL), lambda s: (tile(s), 0))
    in_specs = [row_spec, _const_spec((1, 1, D_MODEL), (layer, 0, 0)),
                pl.BlockSpec((1, D_MODEL, FFN_FC), lambda s: (layer, 0, chunk(s))),
                pl.BlockSpec((1, D_MODEL, FFN_FC), lambda s: (layer, 0, FFN_NCH + chunk(s))),
                pl.BlockSpec((1, FFN_FC, D_MODEL), lambda s: (layer, chunk(s), 0))]
    args = [h, norm_w, w_gu, w_gu, w_d]
    scratch = [pltpu.VMEM((tm, D_MODEL), BF16),
               pltpu.VMEM((tm, FFN_FC), F32),
               pltpu.VMEM((tm, D_FF), BF16),
               pltpu.VMEM((FFN_NCH, D_MODEL, FFN_FC), BF16),
               pltpu.VMEM((FFN_NCH, D_MODEL, FFN_FC), BF16),
               pltpu.VMEM((D_FF, D_MODEL), BF16)]
    if ple is not None:
        p, pn, wgate, wup = ple
        in_specs += [pl.BlockSpec((1, tm, PLE_DIM), lambda s: (layer, tile(s), 0)),
                     _const_spec((1, 1, D_MODEL), (layer, 0, 0)),
                     _const_spec((1, D_MODEL, D_MODEL), (layer, 0, 0)),
                     _const_spec((1, PLE_DIM, D_MODEL), (layer, 0, 0))]
        args += [p, pn, wgate, wup]
        scratch += [pltpu.VMEM((D_MODEL, D_MODEL), BF16), pltpu.VMEM((PLE_DIM, D_MODEL), BF16)]
    if final_norm is not None:
        in_specs.append(_const_spec((1, D_MODEL)))
        args.append(final_norm)
    return pl.pallas_call(
        functools.partial(_ffn_kernel, with_ple=ple is not None, with_final=final_norm is not None),
        grid=(last + t // tm,),
        in_specs=in_specs,
        out_specs=row_spec,
        out_shape=jax.ShapeDtypeStruct((t, D_MODEL), F32),
        scratch_shapes=scratch,
        compiler_params=pltpu.CompilerParams(
            dimension_semantics=("arbitrary",),
            vmem_limit_bytes=_vmem_limit(args, in_specs, row_spec, F32, scratch)),
        name="ffn_ple" if ple is not None else "ffn",
    )(*args)


def _mix_even_kernel(h_ref, nw_ref, win_ref, lng_ref, lnb_ref, ws_ref, bs_ref, scw_ref, wout_ref,
                     o_ref, win_scr, wout_scr, wt_scr, hn_ref, u_ref, bg_ref, vn_ref, q_ref, y_ref):
    tm = MIX_TM
    i = pl.program_id(0)
    j = pl.program_id(1)

    @pl.when((i == 0) & (j == 0))
    def _():
        _cast_rows(win_scr, win_ref)
        _cast_rows(wout_scr, wout_ref)
        row = lax.broadcasted_iota(jnp.int32, (CHUNK, CHUNK), 0)
        col = lax.broadcasted_iota(jnp.int32, (CHUNK, CHUNK), 1)
        for g in range(N_GROUPS):
            wt_scr[g] = jnp.where(col <= row, ws_ref[0, g], 0.0).astype(BF16)

    @pl.when(j == 0)
    def _():
        q_ref[0:SC_HALO, :] = jnp.zeros((SC_HALO, HALF), F32)

    x = h_ref[0]
    hn_ref[...] = _rmsnorm(x, nw_ref[0]).astype(BF16)

    v = _gelu_tanh(_dot(hn_ref[...], win_scr[:, HALF:2 * HALF]))
    for g in range(N_GROUPS):
        gsl = slice(g * GROUP_W, (g + 1) * GROUP_W)
        vn_ref[:, gsl] = _layernorm(v[:, gsl], lng_ref[0][:, gsl], lnb_ref[0][:, gsl]).astype(BF16)
    u_ref[...] = _dot(hn_ref[...], win_scr[:, 0:HALF])
    q_ref[SC_HALO:SC_HALO + tm, :] = (_dot(hn_ref[...], win_scr[:, 3 * HALF:4 * HALF])
                                      * _dot(hn_ref[...], win_scr[:, 4 * HALF:5 * HALF]))
    bg_ref[...] = _dot(hn_ref[...], win_scr[:, 2 * HALF:3 * HALF])
    for g in range(N_GROUPS):
        gsl = slice(g * GROUP_W, (g + 1) * GROUP_W)
        for c in range(tm // CHUNK):
            rsl = slice(c * CHUNK, (c + 1) * CHUNK)
            mixed = _dot(wt_scr[g], vn_ref[rsl, gsl]) + bs_ref[:, gsl]
            y_ref[rsl, gsl] = (_gelu_tanh(u_ref[rsl, gsl]) * mixed).astype(BF16)
    o_ref[0] = x + _dot(y_ref[:, 0:HALF], wout_scr[0:HALF, :])

    conv = scw_ref[0, 2:3, :] * q_ref[SC_HALO:SC_HALO + tm, :]
    conv = conv + scw_ref[0, 1:2, :] * q_ref[SC_HALO - 1:SC_HALO - 1 + tm, :]
    conv = conv + scw_ref[0, 0:1, :] * q_ref[SC_HALO - 2:SC_HALO - 2 + tm, :]
    y_ref[:, HALF:2 * HALF] = (bg_ref[...] * conv).astype(BF16)
    q_ref[0:SC_HALO, :] = q_ref[tm:tm + SC_HALO, :]

    o_ref[0] = o_ref[0] + _dot(y_ref[:, HALF:2 * HALF], wout_scr[HALF:2 * HALF, :])


def _mix_even_call(h3, layer, j, norm_w, w_in, ln_g, ln_b, w_s, b_full, sc_w, w_out):
    b, s, _ = h3.shape
    tm = MIX_TM
    assert s % tm == 0 and tm % CHUNK == 0 and SC_HALO >= SC_WIDTH - 1
    row_spec = pl.BlockSpec((1, tm, D_MODEL), lambda i, t: (i, t, 0))
    in_specs = [row_spec, _const_spec((1, 1, D_MODEL), (layer, 0, 0)),
                _const_spec((1, D_MODEL, 5 * HALF), (j, 0, 0)),
                _const_spec((1, 1, HALF), (j, 0, 0)), _const_spec((1, 1, HALF), (j, 0, 0)),
                _const_spec((1, N_GROUPS, CHUNK, CHUNK), (j, 0, 0, 0)), _const_spec((CHUNK, HALF)),
                _const_spec((1, SC_WIDTH, HALF), (j, 0, 0)),
                _const_spec((1, D_MODEL, D_MODEL), (j, 0, 0))]
    args = [h3, norm_w, w_in, ln_g, ln_b, w_s, b_full, sc_w, w_out]
    scratch = [pltpu.VMEM((D_MODEL, 5 * HALF), BF16),
               pltpu.VMEM((D_MODEL, D_MODEL), BF16),
               pltpu.VMEM((N_GROUPS, CHUNK, CHUNK), BF16),
               pltpu.VMEM((tm, D_MODEL), BF16),
               pltpu.VMEM((tm, HALF), F32),
               pltpu.VMEM((tm, HALF), F32),
               pltpu.VMEM((tm, HALF), BF16),
               pltpu.VMEM((SC_HALO + tm, HALF), F32),
               pltpu.VMEM((tm, D_MODEL), BF16)]
    return pl.pallas_call(
        _mix_even_kernel,
        grid=(b, s // tm),
        in_specs=in_specs,
        out_specs=row_spec,
        out_shape=jax.ShapeDtypeStruct(h3.shape, F32),
        scratch_shapes=scratch,
        compiler_params=pltpu.CompilerParams(
            dimension_semantics=("arbitrary", "arbitrary"),
            vmem_limit_bytes=_vmem_limit(args, in_specs, row_spec, F32, scratch)),
        name="mix_even",
    )(*args)


def _mix_odd_kernel(zs_ref, h_ref, nw_ref, win_ref, pw_ref, ps_ref, cvw_ref, cvb_ref, lng_ref,
                    lnb_ref, wout_ref, o_ref, win_scr, wout_scr, pw_scr, hn_ref, p_ref, sa_ref,
                    sb_ref, ca_ref, cb_ref, y_ref):
    tm = MIX_TM_ODD
    n = HALO + tm
    i = pl.program_id(0)
    j = pl.program_id(1)

    @pl.when((i == 0) & (j == 0))
    def _():
        _cast_rows(win_scr, win_ref)
        _cast_rows(wout_scr, wout_ref)
        for g in range(N_GROUPS):
            pw_scr[g] = pw_ref[0, g].astype(BF16)

    @pl.when(j == 0)
    def _():
        p_ref[0:HALO, :] = jnp.zeros((HALO, HALF), F32)
        ca_ref[0:HALO, :] = jnp.zeros((HALO, GLU_FC), F32)
        cb_ref[0:HALO, :] = jnp.zeros((HALO, GLU_FC), F32)

    x = h_ref[0]
    hn_ref[...] = _rmsnorm(x, nw_ref[0]).astype(BF16)
    p_ref[HALO:n, :] = _dot(hn_ref[...], win_scr[:, 0:HALF])
    c_refs = (ca_ref, cb_ref)
    for c in range(HALF // GLU_FC):
        a = _dot(hn_ref[...], win_scr[:, HALF + c * GLU_FC:HALF + (c + 1) * GLU_FC])
        g = _dot(hn_ref[...], win_scr[:, 2 * HALF + c * GLU_FC:2 * HALF + (c + 1) * GLU_FC])
        c_refs[c][HALO:n, :] = a * _sigmoid(g)

    g1, g2, g3 = GROUP_W, 2 * GROUP_W, 3 * GROUP_W
    sa_ref[8:n, :] = p_ref[8:n, :] + p_ref[7:n - 1, :]
    sb_ref[16:n, g1:] = sa_ref[16:n, g1:] + sa_ref[14:n - 2, g1:]
    sa_ref[24:n, g2:] = sb_ref[24:n, g2:] + sb_ref[20:n - 4, g2:]
    sb_ref[32:n, g3:] = sa_ref[32:n, g3:] + sa_ref[24:n - 8, g3:]
    head = slice(HALO, HALO + max(POOL_WINDOWS))
    pos = lax.broadcasted_iota(jnp.int32, (max(POOL_WINDOWS), GROUP_W), 0)
    for gi, win in enumerate(POOL_WINDOWS):
        gsl = slice(gi * GROUP_W, (gi + 1) * GROUP_W)
        s_ref = sa_ref if gi % 2 == 0 else sb_ref
        count = jnp.minimum(pos + 1, win).astype(F32)
        s_ref[head, gsl] = s_ref[head, gsl] * jnp.where(j == 0, float(win) / count, 1.0)
        pooled = s_ref[HALO:n, gsl] * (1.0 / win) - p_ref[HALO:n, gsl]
        yc = _dot(pooled.astype(BF16), pw_scr[gi]) * ps_ref[0][:, gsl]
        y_ref[:, gsl] = yc.astype(BF16)
    p_ref[0:HALO, :] = p_ref[tm:n, :]
    o_ref[0] = x + _dot(y_ref[:, 0:HALF], wout_scr[0:HALF, :])

    zero = zs_ref[0]
    for c in range(HALF // GLU_FC):
        c_ref = c_refs[c]
        for r in range(tm // CV_ROWS):
            r0 = pl.multiple_of(zero + r * CV_ROWS, CV_ROWS)
            for lt in range(GLU_FC // LANES):
                lsl = slice(lt * LANES, (lt + 1) * LANES)
                wsl = slice(c * GLU_FC + lt * LANES, c * GLU_FC + (lt + 1) * LANES)
                chunk = c_ref[pl.ds(r0, HALO + CV_ROWS), lsl]
                acc = None
                for b in range(SUBLANES):
                    rolled = chunk if b == 0 else pltpu.roll(chunk, b, axis=0)
                    for a in range(HALO // SUBLANES):
                        d = SUBLANES * a + b
                        if d >= CV_WIDTH:
                            continue
                        lo = HALO - SUBLANES * a
                        src = rolled[lo:lo + CV_ROWS].reshape(CV_ROWS // SUBLANES, SUBLANES, LANES)
                        term = cvw_ref[CV_WIDTH - 1 - d, :, wsl][None] * src
                        acc = term if acc is None else acc + term
                acc = acc + cvb_ref[0][:, wsl][None]
                c_ref[pl.ds(r0 + n, CV_ROWS), lsl] = acc.reshape(CV_ROWS, LANES)
    for c_ref in c_refs:
        c_ref[0:HALO, :] = c_ref[tm:n, :]
    hc = jnp.concatenate([c_ref[n:n + tm, :] for c_ref in c_refs], axis=-1)
    ln = _layernorm(hc, lng_ref[0], lnb_ref[0])
    y_ref[:, HALF:2 * HALF] = _silu(ln).astype(BF16)

    o_ref[0] = o_ref[0] + _dot(y_ref[:, HALF:2 * HALF], wout_scr[HALF:2 * HALF, :])


def _mix_odd_call(h3, layer, j, norm_w, w_in, pool_w, pool_scale, cv_w8, cv_b, ln_g, ln_b, w_out):
    b, s, _ = h3.shape
    tm = MIX_TM_ODD
    tiles = s // tm
    assert s % tm == 0 and tm % CV_ROWS == 0 and HALO >= CV_WIDTH - 1
    assert HALO >= 2 * max(POOL_WINDOWS) - 1 and tm >= max(POOL_WINDOWS)
    row_spec = pl.BlockSpec((1, tm, D_MODEL), lambda i, t: (i, t, 0))
    vec = _const_spec((1, 1, HALF), (j, 0, 0))
    in_specs = [pl.BlockSpec(memory_space=pltpu.SMEM), row_spec,
                _const_spec((1, 1, D_MODEL), (layer, 0, 0)),
                _const_spec((1, D_MODEL, 3 * HALF), (j, 0, 0)),
                _const_spec((1, N_GROUPS, GROUP_W, GROUP_W), (j, 0, 0, 0)), vec,
                _const_spec((CV_WIDTH, SUBLANES, HALF)), vec, vec, vec,
                _const_spec((1, D_MODEL, D_MODEL), (j, 0, 0))]
    args = [jnp.zeros((1,), jnp.int32), h3, norm_w, w_in, pool_w, pool_scale, cv_w8, cv_b, ln_g, ln_b,
            w_out]
    scratch = [pltpu.VMEM((D_MODEL, 3 * HALF), BF16),
               pltpu.VMEM((D_MODEL, D_MODEL), BF16),
               pltpu.VMEM((N_GROUPS, GROUP_W, GROUP_W), BF16),
               pltpu.VMEM((tm, D_MODEL), BF16),
               pltpu.VMEM((HALO + tm, HALF), F32),
               pltpu.VMEM((HALO + tm, HALF), F32),
               pltpu.VMEM((HALO + tm, HALF), F32),
               pltpu.VMEM((HALO + 2 * tm, GLU_FC), F32),
               pltpu.VMEM((HALO + 2 * tm, GLU_FC), F32),
               pltpu.VMEM((tm, D_MODEL), BF16)]
    return pl.pallas_call(
        _mix_odd_kernel,
        grid=(b, tiles),
        in_specs=in_specs,
        out_specs=row_spec,
        out_shape=jax.ShapeDtypeStruct(h3.shape, F32),
        scratch_shapes=scratch,
        compiler_params=pltpu.CompilerParams(
            dimension_semantics=("arbitrary", "arbitrary"),
            vmem_limit_bytes=_vmem_limit(args, in_specs, row_spec, F32, scratch)),
        name="mix_odd",
    )(*args)


def kernel(x, p, ffn1_norm, ffn1_w_gu, ffn1_w_down, mix_norm, ffn2_norm, ffn2_w_gu, ffn2_w_down,
           ple_norm, ple_w_gate, ple_w_up, ab_w_in, gm_ln_g, gm_ln_b, gm_w_s, gm_b_s, sc_w,
           ab_w_out, cd_w_in, pool_w, pool_scale, cv_w, cv_b, cv_ln_g, cv_ln_b, cd_w_out,
           final_norm):
    bsz, seq, d = x.shape
    depth = p.shape[0]
    t = bsz * seq
    vec3 = lambda v: v.reshape(v.shape[0], 1, v.shape[1])
    p3 = p.reshape(depth, t, PLE_DIM)
    h = x.reshape(t, d)
    for i in range(depth):
        j = i // 2
        h = _ffn_call(h, i, vec3(ffn1_norm), ffn1_w_gu, ffn1_w_down)
        h3 = h.reshape(bsz, seq, d)
        if i % 2 == 0:
            b_full = jnp.repeat(gm_b_s[j].T, GROUP_W, axis=1)
            h3 = _mix_even_call(h3, i, j, vec3(mix_norm), ab_w_in, vec3(gm_ln_g), vec3(gm_ln_b),
                                gm_w_s, b_full, sc_w, ab_w_out)
        else:
            cv_w8 = jnp.broadcast_to(cv_w[j][:, None, :], (CV_WIDTH, SUBLANES, HALF))
            h3 = _mix_odd_call(h3, i, j, vec3(mix_norm), cd_w_in, pool_w, vec3(pool_scale), cv_w8,
                               vec3(cv_b), vec3(cv_ln_g), vec3(cv_ln_b), cd_w_out)
        h = h3.reshape(t, d)
        h = _ffn_call(h, i, vec3(ffn2_norm), ffn2_w_gu, ffn2_w_down,
                      ple=(p3, vec3(ple_norm), ple_w_gate, ple_w_up),
                      final_norm=final_norm.reshape(1, d) if i == depth - 1 else None)
    return h.reshape(bsz, seq, d)
```

```python
import functools
import math

import jax
import jax.numpy as jnp
from jax import lax
from jax.experimental import pallas as pl
from jax.experimental.pallas import tpu as pltpu

F32 = jnp.float32
BF16 = jnp.bfloat16

D_MODEL = 1024
PLE_DIM = 256
D_FF = 2816
HALF = 512
LANES = 128
SUBLANES = 8
GROUP_W = 128
N_GROUPS = 4
CHUNK = 128
POOL_WINDOWS = (2, 4, 8, 16)
SC_WIDTH = 3
CV_WIDTH = 31
RMS_EPS = 1e-6
LN_EPS = 1e-5

FFN_TM = 512
FFN_TM_FIRST = 1024
FFN_FC = 256
FFN_NCH = D_FF // FFN_FC
MIX_TM = 1024
MIX_TM_ODD = 1024
ANCHOR_SLOTS = 4
ODD_PARTS = 2
GLU_FC = 256
CAST_ROWS = 256
SC_HALO = 8
HALO = 32
CV_ROWS = 128
SPILL_MARGIN = 6 * 1024 * 1024


def _rms_scale(x):
    return lax.rsqrt(jnp.mean(x * x, axis=-1, keepdims=True) + RMS_EPS)


def _rmsnorm(x, g):
    return x * _rms_scale(x) * g


def _layernorm(x, g, b):
    mu = jnp.mean(x, axis=-1, keepdims=True)
    xc = x - mu
    var = jnp.mean(xc * xc, axis=-1, keepdims=True)
    return xc * lax.rsqrt(var + LN_EPS) * g + b


def _sigmoid(x):
    return 0.5 + 0.5 * jnp.tanh(0.5 * x)


def _silu(x):
    hx = 0.5 * x
    return hx + hx * jnp.tanh(hx)


def _gelu_tanh(x):
    c = 0.7978845608028654
    return 0.5 * x * (1.0 + jnp.tanh(c * (x + 0.044715 * (x * x * x))))


def _dot(a, b):
    return jnp.dot(a, b, preferred_element_type=F32)


def _cast_rows(dst_ref, src_ref):
    rows = src_ref.shape[1]
    step = min(rows, CAST_ROWS)
    for r in range(0, rows, step):
        dst_ref[r:r + step, :] = src_ref[0, r:r + step, :].astype(BF16)


def _const_spec(shape, index=None):
    index = (0,) * len(shape) if index is None else index
    return pl.BlockSpec(shape, lambda *_: index, pipeline_mode=pl.Buffered(1))


def _vmem_limit(args, in_specs, out_spec, out_dtype, scratch):
    nbytes = lambda shape, dtype: math.prod(shape) * jnp.dtype(dtype).itemsize
    total = 2 * nbytes(out_spec.block_shape, out_dtype) + SPILL_MARGIN
    for arg, spec in zip(args, in_specs):
        if spec.block_shape is not None:
            total += (1 if spec.pipeline_mode is not None else 2) * nbytes(spec.block_shape, arg.dtype)
    return total + sum(nbytes(s.shape, s.dtype) for s in scratch)


def _ffn_kernel(*refs, with_ple, with_final):
    h_ref, nw_ref, wg_blk, wu_blk, wd_blk = refs[:5]
    k = 5
    if with_ple:
        p_ref, pn_ref, wgate_ref, wup_ref = refs[k:k + 4]
        k += 4
    if with_final:
        fn_ref = refs[k]
        k += 1
    o_ref, xn_ref, r_ref, a_ref, wg_scr, wu_scr, wd_scr = refs[k:k + 7]
    k += 7
    if with_ple:
        wgate_scr, wup_scr = refs[k:k + 2]
    s = pl.program_id(0)

    def normalize():
        x = h_ref[...]
        xn_ref[...] = (x * nw_ref[0]).astype(BF16)
        r_ref[...] = jnp.broadcast_to(_rms_scale(x), r_ref.shape)

    def up_chunk(c):
        sl = slice(c * FFN_FC, (c + 1) * FFN_FC)
        g = r_ref[...] * _dot(xn_ref[...], wg_scr[c])
        u = r_ref[...] * _dot(xn_ref[...], wu_scr[c])
        a_ref[:, sl] = (_silu(g) * u).astype(BF16)

    def finish():
        h1 = h_ref[...] + 0.5 * _dot(a_ref[...], wd_scr[...])
        if with_ple:
            xn_ref[...] = (h1 * pn_ref[0]).astype(BF16)
            gate = _sigmoid(_rms_scale(h1) * _dot(xn_ref[...], wgate_scr[...]))
            up = _dot(p_ref[0].astype(BF16), wup_scr[...])
            h1 = h1 + gate * up
        if with_final:
            h1 = _rmsnorm(h1, fn_ref[...])
        o_ref[...] = h1

    @pl.when(s == 0)
    def _():
        normalize()
        if with_ple:
            _cast_rows(wgate_scr, wgate_ref)
            _cast_rows(wup_scr, wup_ref)

    for c in range(FFN_NCH):
        @pl.when(s == c)
        def _(c=c):
            wg_scr[c] = wg_blk[0].astype(BF16)
            wu_scr[c] = wu_blk[0].astype(BF16)
            wd_scr[c * FFN_FC:(c + 1) * FFN_FC, :] = wd_blk[0].astype(BF16)
            up_chunk(c)
            if c == FFN_NCH - 1:
                finish()

    @pl.when(s >= FFN_NCH)
    def _():
        normalize()
        for c in range(FFN_NCH):
            up_chunk(c)
        finish()


def _ffn_tile(layer, with_ple):
    return FFN_TM_FIRST if (layer == 0 and not with_ple) else FFN_TM


def _ffn_call(h, layer, norm_w, w_gu, w_d, ple=None, final_norm=None):
    t = h.shape[0]
    tm = _ffn_tile(layer, ple is not None)
    assert t % tm == 0 and D_FF % FFN_FC == 0
    last = FFN_NCH - 1
    tile = lambda s: jnp.maximum(s - last, 0)
    chunk = lambda s: jnp.minimum(s, last)
    row_spec = pl.BlockSpec((tm, D_MODEL), lambda s: (tile(s), 0))
    in_specs = [row_spec, _const_spec((1, 1, D_MODEL), (layer, 0, 0)),
                pl.BlockSpec((1, D_MODEL, FFN_FC), lambda s: (layer, 0, chunk(s))),
                pl.BlockSpec((1, D_MODEL, FFN_FC), lambda s: (layer, 0, FFN_NCH + chunk(s))),
                pl.BlockSpec((1, FFN_FC, D_MODEL), lambda s: (layer, chunk(s), 0))]
    args = [h, norm_w, w_gu, w_gu, w_d]
    scratch = [pltpu.VMEM((tm, D_MODEL), BF16),
               pltpu.VMEM((tm, FFN_FC), F32),
               pltpu.VMEM((tm, D_FF), BF16),
               pltpu.VMEM((FFN_NCH, D_MODEL, FFN_FC), BF16),
               pltpu.VMEM((FFN_NCH, D_MODEL, FFN_FC), BF16),
               pltpu.VMEM((D_FF, D_MODEL), BF16)]
    if ple is not None:
        p, pn, wgate, wup = ple
        in_specs += [pl.BlockSpec((1, tm, PLE_DIM), lambda s: (layer, tile(s), 0)),
                     _const_spec((1, 1, D_MODEL), (layer, 0, 0)),
                     _const_spec((1, D_MODEL, D_MODEL), (layer, 0, 0)),
                     _const_spec((1, PLE_DIM, D_MODEL), (layer, 0, 0))]
        args += [p, pn, wgate, wup]
        scratch += [pltpu.VMEM((D_MODEL, D_MODEL), BF16), pltpu.VMEM((PLE_DIM, D_MODEL), BF16)]
    if final_norm is not None:
        in_specs.append(_const_spec((1, D_MODEL)))
        args.append(final_norm)
    return pl.pallas_call(
        functools.partial(_ffn_kernel, with_ple=ple is not None, with_final=final_norm is not None),
        grid=(last + t // tm,),
        in_specs=in_specs,
        out_specs=row_spec,
        out_shape=jax.ShapeDtypeStruct((t, D_MODEL), F32),
        scratch_shapes=scratch,
        compiler_params=pltpu.CompilerParams(
            dimension_semantics=("arbitrary",),
            vmem_limit_bytes=_vmem_limit(args, in_specs, row_spec, F32, scratch)),
        name="ffn_ple" if ple is not None else "ffn",
    )(*args)


def _mix_even_kernel(h_ref, nw_ref, win_ref, lng_ref, lnb_ref, ws_ref, bs_ref, scw_ref, wout_ref,
                     o_ref, win_scr, wout_scr, wt_scr, hn_ref, u_ref, bg_ref, vn_ref, q_ref, y_ref):
    tm = MIX_TM
    i = pl.program_id(0)
    j = pl.program_id(1)

    @pl.when((i == 0) & (j == 0))
    def _():
        _cast_rows(win_scr, win_ref)
        _cast_rows(wout_scr, wout_ref)
        row = lax.broadcasted_iota(jnp.int32, (CHUNK, CHUNK), 0)
        col = lax.broadcasted_iota(jnp.int32, (CHUNK, CHUNK), 1)
        for g in range(N_GROUPS):
            wt_scr[g] = jnp.where(col <= row, ws_ref[0, g], 0.0).astype(BF16)

    @pl.when(j == 0)
    def _():
        q_ref[0:SC_HALO, :] = jnp.zeros((SC_HALO, HALF), F32)

    x = h_ref[0]
    hn_ref[...] = _rmsnorm(x, nw_ref[0]).astype(BF16)

    v = _gelu_tanh(_dot(hn_ref[...], win_scr[:, HALF:2 * HALF]))
    for g in range(N_GROUPS):
        gsl = slice(g * GROUP_W, (g + 1) * GROUP_W)
        vn_ref[:, gsl] = _layernorm(v[:, gsl], lng_ref[0][:, gsl], lnb_ref[0][:, gsl]).astype(BF16)
    u_ref[...] = _dot(hn_ref[...], win_scr[:, 0:HALF])
    q_ref[SC_HALO:SC_HALO + tm, :] = (_dot(hn_ref[...], win_scr[:, 3 * HALF:4 * HALF])
                                      * _dot(hn_ref[...], win_scr[:, 4 * HALF:5 * HALF]))
    bg_ref[...] = _dot(hn_ref[...], win_scr[:, 2 * HALF:3 * HALF])
    for g in range(N_GROUPS):
        gsl = slice(g * GROUP_W, (g + 1) * GROUP_W)
        for c in range(tm // CHUNK):
            rsl = slice(c * CHUNK, (c + 1) * CHUNK)
            mixed = _dot(wt_scr[g], vn_ref[rsl, gsl]) + bs_ref[:, gsl]
            y_ref[rsl, gsl] = (_gelu_tanh(u_ref[rsl, gsl]) * mixed).astype(BF16)
    o_ref[0] = x + _dot(y_ref[:, 0:HALF], wout_scr[0:HALF, :])

    conv = scw_ref[0, 2:3, :] * q_ref[SC_HALO:SC_HALO + tm, :]
    conv = conv + scw_ref[0, 1:2, :] * q_ref[SC_HALO - 1:SC_HALO - 1 + tm, :]
    conv = conv + scw_ref[0, 0:1, :] * q_ref[SC_HALO - 2:SC_HALO - 2 + tm, :]
    y_ref[:, HALF:2 * HALF] = (bg_ref[...] * conv).astype(BF16)
    q_ref[0:SC_HALO, :] = q_ref[tm:tm + SC_HALO, :]

    o_ref[0] = o_ref[0] + _dot(y_ref[:, HALF:2 * HALF], wout_scr[HALF:2 * HALF, :])


def _mix_even_call(h3, layer, j, norm_w, w_in, ln_g, ln_b, w_s, b_full, sc_w, w_out):
    b, s, _ = h3.shape
    tm = MIX_TM
    assert s % tm == 0 and tm % CHUNK == 0 and SC_HALO >= SC_WIDTH - 1
    row_spec = pl.BlockSpec((1, tm, D_MODEL), lambda i, t: (i, t, 0))
    in_specs = [row_spec, _const_spec((1, 1, D_MODEL), (layer, 0, 0)),
                _const_spec((1, D_MODEL, 5 * HALF), (j, 0, 0)),
                _const_spec((1, 1, HALF), (j, 0, 0)), _const_spec((1, 1, HALF), (j, 0, 0)),
                _const_spec((1, N_GROUPS, CHUNK, CHUNK), (j, 0, 0, 0)), _const_spec((CHUNK, HALF)),
                _const_spec((1, SC_WIDTH, HALF), (j, 0, 0)),
                _const_spec((1, D_MODEL, D_MODEL), (j, 0, 0))]
    args = [h3, norm_w, w_in, ln_g, ln_b, w_s, b_full, sc_w, w_out]
    scratch = [pltpu.VMEM((D_MODEL, 5 * HALF), BF16),
               pltpu.VMEM((D_MODEL, D_MODEL), BF16),
               pltpu.VMEM((N_GROUPS, CHUNK, CHUNK), BF16),
               pltpu.VMEM((tm, D_MODEL), BF16),
               pltpu.VMEM((tm, HALF), F32),
               pltpu.VMEM((tm, HALF), F32),
               pltpu.VMEM((tm, HALF), BF16),
               pltpu.VMEM((SC_HALO + tm, HALF), F32),
               pltpu.VMEM((tm, D_MODEL), BF16)]
    return pl.pallas_call(
        _mix_even_kernel,
        grid=(b, s // tm),
        in_specs=in_specs,
        out_specs=row_spec,
        out_shape=jax.ShapeDtypeStruct(h3.shape, F32),
        scratch_shapes=scratch,
        compiler_params=pltpu.CompilerParams(
            dimension_semantics=("arbitrary", "arbitrary"),
            vmem_limit_bytes=_vmem_limit(args, in_specs, row_spec, F32, scratch)),
        name="mix_even",
    )(*args)


class _RowView:
    def __init__(self, ref, base):
        self.ref, self.base = ref, base

    def _shift(self, idx):
        rows, lanes = idx
        if isinstance(rows, slice):
            rows = slice(rows.start + self.base, rows.stop + self.base)
        else:
            rows = pl.ds(rows.start + self.base, rows.size)
        return rows, lanes

    def __getitem__(self, idx):
        return self.ref[self._shift(idx)]

    def __setitem__(self, idx, val):
        self.ref[self._shift(idx)] = val


def _mix_odd_kernel(zs_ref, h_ref, nw_ref, win_ref, pw_ref, ps_ref, cvw_ref, cvb_ref, lng_ref,
                    lnb_ref, wout_ref, o_ref, win_scr, wout_scr, pw_scr, hn_ref, p_ref, sa_ref,
                    sb_ref, raw_ref, *rest):
    c_refs, y_ref = rest[:-1], rest[-1]
    n_lane = HALF // GLU_FC

    unit_rows = HALO + 2 * (MIX_TM_ODD // ODD_PARTS)

    def c_ref_of(part, c):
        u = part * n_lane + c
        return _RowView(c_refs[(u + 1) // 2], unit_rows * ((u + 1) % 2) if u > 0 else 0)

    tm = MIX_TM_ODD
    n = HALO + tm
    rp = tm // ODD_PARTS
    npart = HALO + rp
    i = pl.program_id(0)
    j = pl.program_id(1)

    @pl.when((i == 0) & (j == 0))
    def _():
        _cast_rows(win_scr, win_ref)
        _cast_rows(wout_scr, wout_ref)
        for g in range(N_GROUPS):
            pw_scr[g] = pw_ref[0, g].astype(BF16)

    @pl.when(j == 0)
    def _():
        p_ref[0:HALO, :] = jnp.zeros((HALO, HALF), F32)
        for c in range(n_lane):
            c_ref_of(0, c)[0:HALO, :] = jnp.zeros((HALO, GLU_FC), F32)

    zero = zs_ref[0]
    n_units = ODD_PARTS * n_lane
    anchor_count = {}

    def anchor(acc, u):
        if u < n_units:
            ref = c_refs[(u + 1) // 2]
            slot = anchor_count.get(u, 0)
            anchor_count[u] = slot + 1
            spare = ref.shape[0] - ANCHOR_SLOTS * SUBLANES + slot * SUBLANES
            ref[spare:spare + SUBLANES, :] = acc[acc.shape[0] - SUBLANES:, acc.shape[1] - GLU_FC:]

    def front(part):
        rows = slice(part * rp, (part + 1) * rp)
        for c in range(n_lane):
            a = _dot(hn_ref[rows, :], win_scr[:, HALF + c * GLU_FC:HALF + (c + 1) * GLU_FC])
            g = _dot(hn_ref[rows, :], win_scr[:, 2 * HALF + c * GLU_FC:2 * HALF + (c + 1) * GLU_FC])
            if part == 0:
                c_ref_of(part, c)[HALO:npart, :] = a * _sigmoid(g)
            else:
                raw_ref[c, :, 0:GLU_FC] = a
                raw_ref[c, :, GLU_FC:2 * GLU_FC] = g
        pin = _dot(hn_ref[rows, :], win_scr[:, 0:HALF])
        p_ref[HALO + part * rp:HALO + (part + 1) * rp, :] = pin

    def glu(part):
        for c in range(n_lane):
            if part > 0:
                c_ref_of(part, c)[0:HALO, :] = c_ref_of(part - 1, c)[rp:npart, :]
                c_ref_of(part, c)[HALO:npart, :] = (raw_ref[c, :, 0:GLU_FC]
                                                    * _sigmoid(raw_ref[c, :, GLU_FC:2 * GLU_FC]))

    def pool(part):
        rows = slice(part * rp, (part + 1) * rp)
        b0 = part * rp
        e = b0 + npart
        g1, g2, g3 = GROUP_W, 2 * GROUP_W, 3 * GROUP_W
        sa_ref[b0 + 8:e, :] = p_ref[b0 + 8:e, :] + p_ref[b0 + 7:e - 1, :]
        sb_ref[b0 + 16:e, g1:] = sa_ref[b0 + 16:e, g1:] + sa_ref[b0 + 14:e - 2, g1:]
        sa_ref[b0 + 24:e, g2:] = sb_ref[b0 + 24:e, g2:] + sb_ref[b0 + 20:e - 4, g2:]
        sb_ref[b0 + 32:e, g3:] = sa_ref[b0 + 32:e, g3:] + sa_ref[b0 + 24:e - 8, g3:]
        head = slice(HALO, HALO + max(POOL_WINDOWS))
        pos = lax.broadcasted_iota(jnp.int32, (max(POOL_WINDOWS), GROUP_W), 0)
        for gi, win in enumerate(POOL_WINDOWS):
            gsl = slice(gi * GROUP_W, (gi + 1) * GROUP_W)
            s_ref = sa_ref if gi % 2 == 0 else sb_ref
            if part == 0:
                count = jnp.minimum(pos + 1, win).astype(F32)
                s_ref[head, gsl] = s_ref[head, gsl] * jnp.where(j == 0, float(win) / count, 1.0)
            pooled = s_ref[b0 + HALO:e, gsl] * (1.0 / win) - p_ref[b0 + HALO:e, gsl]
            yc = _dot(pooled.astype(BF16), pw_scr[gi]) * ps_ref[0][:, gsl]
            y_ref[rows, gsl] = yc.astype(BF16)
        acc = _dot(y_ref[rows, 0:HALF], wout_scr[0:HALF, :])
        o_ref[0, rows, :] = h_ref[0, rows, :] + acc
        anchor(acc, (part + 1) * n_lane + n_lane - 1)

    def conv(part):
        for c in range(n_lane):
            c_ref = c_ref_of(part, c)
            for r in range(rp // CV_ROWS):
                r0 = pl.multiple_of(zero + r * CV_ROWS, CV_ROWS)
                for lt in range(GLU_FC // LANES):
                    lsl = slice(lt * LANES, (lt + 1) * LANES)
                    wsl = slice(c * GLU_FC + lt * LANES, c * GLU_FC + (lt + 1) * LANES)
                    chunk = c_ref[pl.ds(r0, HALO + CV_ROWS), lsl]
                    acc = None
                    for b in range(SUBLANES):
                        rolled = chunk if b == 0 else pltpu.roll(chunk, b, axis=0)
                        for a in range(HALO // SUBLANES):
                            d = SUBLANES * a + b
                            if d >= CV_WIDTH:
                                continue
                            lo = HALO - SUBLANES * a
                            src = rolled[lo:lo + CV_ROWS].reshape(CV_ROWS // SUBLANES, SUBLANES, LANES)
                            term = cvw_ref[CV_WIDTH - 1 - d, :, wsl][None] * src
                            acc = term if acc is None else acc + term
                    acc = acc + cvb_ref[0][:, wsl][None]
                    c_ref[pl.ds(r0 + npart, CV_ROWS), lsl] = acc.reshape(CV_ROWS, LANES)

    def back(part):
        rows = slice(part * rp, (part + 1) * rp)
        hc = jnp.concatenate([c_ref_of(part, c)[npart:npart + rp, :] for c in range(n_lane)], axis=-1)
        ln = _layernorm(hc, lng_ref[0], lnb_ref[0])
        y_ref[rows, HALF:2 * HALF] = _silu(ln).astype(BF16)
        acc = _dot(y_ref[rows, HALF:2 * HALF], wout_scr[HALF:2 * HALF, :])
        o_ref[0, rows, :] = o_ref[0, rows, :] + acc
        anchor(acc, (part + 1) * n_lane + n_lane - 1)

    hn_ref[...] = _rmsnorm(h_ref[0], nw_ref[0]).astype(BF16)
    front(0)
    anchor(hn_ref[tm - 2 * SUBLANES:tm, :].astype(F32), 0)
    for part in range(ODD_PARTS):
        if part + 1 < ODD_PARTS:
            front(part + 1)
        pool(part)
        glu(part)
        conv(part)
        back(part)
    p_ref[0:HALO, :] = p_ref[tm:n, :]
    for c in range(n_lane):
        c_ref_of(0, c)[0:HALO, :] = c_ref_of(ODD_PARTS - 1, c)[rp:npart, :]


def _mix_odd_call(h3, layer, j, norm_w, w_in, pool_w, pool_scale, cv_w8, cv_b, ln_g, ln_b, w_out):
    b, s, _ = h3.shape
    tm = MIX_TM_ODD
    tiles = s // tm
    assert s % tm == 0 and (tm // ODD_PARTS) % CV_ROWS == 0 and HALO >= CV_WIDTH - 1 and ODD_PARTS == 2
    assert HALO >= 2 * max(POOL_WINDOWS) - 1 and tm >= max(POOL_WINDOWS)
    row_spec = pl.BlockSpec((1, tm, D_MODEL), lambda i, t: (i, t, 0))
    vec = _const_spec((1, 1, HALF), (j, 0, 0))
    in_specs = [pl.BlockSpec(memory_space=pltpu.SMEM), row_spec,
                _const_spec((1, 1, D_MODEL), (layer, 0, 0)),
                _const_spec((1, D_MODEL, 3 * HALF), (j, 0, 0)),
                _const_spec((1, N_GROUPS, GROUP_W, GROUP_W), (j, 0, 0, 0)), vec,
                _const_spec((CV_WIDTH, SUBLANES, HALF)), vec, vec, vec,
                _const_spec((1, D_MODEL, D_MODEL), (j, 0, 0))]
    args = [jnp.zeros((1,), jnp.int32), h3, norm_w, w_in, pool_w, pool_scale, cv_w8, cv_b, ln_g, ln_b,
            w_out]
    scratch = [pltpu.VMEM((D_MODEL, 3 * HALF), BF16),
               pltpu.VMEM((D_MODEL, D_MODEL), BF16),
               pltpu.VMEM((N_GROUPS, GROUP_W, GROUP_W), BF16),
               pltpu.VMEM((tm, D_MODEL), BF16),
               pltpu.VMEM((HALO + tm, HALF), F32),
               pltpu.VMEM((HALO + tm, HALF), F32),
               pltpu.VMEM((HALO + tm, HALF), F32),
               pltpu.VMEM((HALF // GLU_FC, tm // ODD_PARTS, 2 * GLU_FC), F32),
               pltpu.VMEM((HALO + 2 * tm // ODD_PARTS + ANCHOR_SLOTS * SUBLANES, GLU_FC), F32),
               *[pltpu.VMEM((2 * (HALO + 2 * tm // ODD_PARTS) + ANCHOR_SLOTS * SUBLANES, GLU_FC), F32)
                 for _ in range(ODD_PARTS * (HALF // GLU_FC) // 2 - 1)],
               pltpu.VMEM((HALO + 2 * tm // ODD_PARTS + ANCHOR_SLOTS * SUBLANES, GLU_FC), F32),
               pltpu.VMEM((tm, D_MODEL), BF16)]
    return pl.pallas_call(
        _mix_odd_kernel,
        grid=(b, tiles),
        in_specs=in_specs,
        out_specs=row_spec,
        out_shape=jax.ShapeDtypeStruct(h3.shape, F32),
        scratch_shapes=scratch,
        compiler_params=pltpu.CompilerParams(
            dimension_semantics=("arbitrary", "arbitrary"),
            vmem_limit_bytes=_vmem_limit(args, in_specs, row_spec, F32, scratch)),
        name="mix_odd",
    )(*args)


def kernel(x, p, ffn1_norm, ffn1_w_gu, ffn1_w_down, mix_norm, ffn2_norm, ffn2_w_gu, ffn2_w_down,
           ple_norm, ple_w_gate, ple_w_up, ab_w_in, gm_ln_g, gm_ln_b, gm_w_s, gm_b_s, sc_w,
           ab_w_out, cd_w_in, pool_w, pool_scale, cv_w, cv_b, cv_ln_g, cv_ln_b, cd_w_out,
           final_norm):
    bsz, seq, d = x.shape
    depth = p.shape[0]
    t = bsz * seq
    vec3 = lambda v: v.reshape(v.shape[0], 1, v.shape[1])
    p3 = p.reshape(depth, t, PLE_DIM)
    h = x.reshape(t, d)
    for i in range(depth):
        j = i // 2
        h = _ffn_call(h, i, vec3(ffn1_norm), ffn1_w_gu, ffn1_w_down)
        h3 = h.reshape(bsz, seq, d)
        if i % 2 == 0:
            b_full = jnp.repeat(gm_b_s[j].T, GROUP_W, axis=1)
            h3 = _mix_even_call(h3, i, j, vec3(mix_norm), ab_w_in, vec3(gm_ln_g), vec3(gm_ln_b),
                                gm_w_s, b_full, sc_w, ab_w_out)
        else:
            cv_w8 = jnp.broadcast_to(cv_w[j][:, None, :], (CV_WIDTH, SUBLANES, HALF))
            h3 = _mix_odd_call(h3, i, j, vec3(mix_norm), cd_w_in, pool_w, vec3(pool_scale), cv_w8,
                               vec3(cv_b), vec3(cv_ln_g), vec3(cv_ln_b), cd_w_out)
        h = h3.reshape(t, d)
        h = _ffn_call(h, i, vec3(ffn2_norm), ffn2_w_gu, ffn2_w_down,
                      ple=(p3, vec3(ple_norm), ple_w_gate, ple_w_up),
                      final_norm=final_norm.reshape(1, d) if i == depth - 1 else None)
    return h.reshape(bsz, seq, d)
```

```python
import functools
import math

import jax
import jax.numpy as jnp
from jax import lax
from jax.experimental import pallas as pl
from jax.experimental.pallas import tpu as pltpu

F32 = jnp.float32
BF16 = jnp.bfloat16

D_MODEL = 1024
PLE_DIM = 256
D_FF = 2816
HALF = 512
LANES = 128
SUBLANES = 8
GROUP_W = 128
N_GROUPS = 4
CHUNK = 128
POOL_WINDOWS = (2, 4, 8, 16)
SC_WIDTH = 3
CV_WIDTH = 31
RMS_EPS = 1e-6
LN_EPS = 1e-5

FFN_TM = 512
FFN_TM_FIRST = 1024
FFN_FC = 256
FFN_NCH = D_FF // FFN_FC
MIX_TM = 1024
MIX_TM_ODD = 1024
ANCHOR_SLOTS = 4
ODD_PARTS = 2
GLU_FC = 256
CAST_ROWS = 256
SC_HALO = 8
HALO = 32
CV_ROWS = 128
SPILL_MARGIN = 6 * 1024 * 1024


def _rms_scale(x):
    return lax.rsqrt(jnp.mean(x * x, axis=-1, keepdims=True) + RMS_EPS)


def _rmsnorm(x, g):
    return x * _rms_scale(x) * g


def _layernorm(x, g, b):
    mu = jnp.mean(x, axis=-1, keepdims=True)
    xc = x - mu
    var = jnp.mean(xc * xc, axis=-1, keepdims=True)
    return xc * lax.rsqrt(var + LN_EPS) * g + b


def _sigmoid(x):
    return 0.5 + 0.5 * jnp.tanh(0.5 * x)


def _silu(x):
    hx = 0.5 * x
    return hx + hx * jnp.tanh(hx)


def _gelu_tanh(x):
    c = 0.7978845608028654
    return 0.5 * x * (1.0 + jnp.tanh(c * (x + 0.044715 * (x * x * x))))


def _dot(a, b):
    return jnp.dot(a, b, preferred_element_type=F32)


def _cast_rows(dst_ref, src_ref):
    rows = src_ref.shape[1]
    step = min(rows, CAST_ROWS)
    for r in range(0, rows, step):
        dst_ref[r:r + step, :] = src_ref[0, r:r + step, :].astype(BF16)


def _const_spec(shape, index=None):
    index = (0,) * len(shape) if index is None else index
    return pl.BlockSpec(shape, lambda *_: index, pipeline_mode=pl.Buffered(1))


def _vmem_limit(args, in_specs, out_spec, out_dtype, scratch):
    nbytes = lambda shape, dtype: math.prod(shape) * jnp.dtype(dtype).itemsize
    total = 2 * nbytes(out_spec.block_shape, out_dtype) + SPILL_MARGIN
    for arg, spec in zip(args, in_specs):
        if spec.block_shape is not None:
            total += (1 if spec.pipeline_mode is not None else 2) * nbytes(spec.block_shape, arg.dtype)
    return total + sum(nbytes(s.shape, s.dtype) for s in scratch)


def _ffn_kernel(*refs, layer, with_ple, with_final):
    h_ref, nw_ref, wgu_hbm, wd_hbm = refs[:4]
    k = 4
    if with_ple:
        p_ref, pn_ref, wgate_ref, wup_ref = refs[k:k + 4]
        k += 4
    if with_final:
        fn_ref = refs[k]
        k += 1
    o_ref, xn_ref, r_ref, a_ref, wg_scr, wu_scr, wd_scr, stg_g, stg_u, stg_d = refs[k:k + 10]
    k += 10
    sem = refs[-1]
    if with_ple:
        wgate_scr, wup_scr = refs[k:k + 2]
    s = pl.program_id(0)

    def normalize():
        x = h_ref[...]
        xn_ref[...] = (x * nw_ref[0]).astype(BF16)
        r_ref[...] = jnp.broadcast_to(_rms_scale(x), r_ref.shape)

    def up_chunk(c):
        sl = slice(c * FFN_FC, (c + 1) * FFN_FC)
        g = r_ref[...] * _dot(xn_ref[...], wg_scr[c])
        u = r_ref[...] * _dot(xn_ref[...], wu_scr[c])
        a_ref[:, sl] = (_silu(g) * u).astype(BF16)

    def finish():
        h1 = h_ref[...] + 0.5 * _dot(a_ref[...], wd_scr[...])
        if with_ple:
            xn_ref[...] = (h1 * pn_ref[0]).astype(BF16)
            gate = _sigmoid(_rms_scale(h1) * _dot(xn_ref[...], wgate_scr[...]))
            up = _dot(p_ref[0].astype(BF16), wup_scr[...])
            h1 = h1 + gate * up
        if with_final:
            h1 = _rmsnorm(h1, fn_ref[...])
        o_ref[...] = h1

    def stage_copies(c):
        slot = c % 2
        cols = pl.ds(c * FFN_FC, FFN_FC)
        ucols = pl.ds(D_FF + c * FFN_FC, FFN_FC)
        return (pltpu.make_async_copy(wgu_hbm.at[layer, pl.ds(0, D_MODEL), cols], stg_g.at[slot],
                                      sem.at[0, slot]),
                pltpu.make_async_copy(wgu_hbm.at[layer, pl.ds(0, D_MODEL), ucols], stg_u.at[slot],
                                      sem.at[1, slot]),
                pltpu.make_async_copy(wd_hbm.at[layer, cols, pl.ds(0, D_MODEL)], stg_d.at[slot],
                                      sem.at[2, slot]))

    @pl.when(s == 0)
    def _():
        for cp in stage_copies(0):
            cp.start()
        normalize()
        if with_ple:
            _cast_rows(wgate_scr, wgate_ref)
            _cast_rows(wup_scr, wup_ref)
        for c in range(FFN_NCH):
            if c + 1 < FFN_NCH:
                for cp in stage_copies(c + 1):
                    cp.start()
            for cp in stage_copies(c):
                cp.wait()
            slot = c % 2
            wg_scr[c] = stg_g[slot].astype(BF16)
            wu_scr[c] = stg_u[slot].astype(BF16)
            wd_scr[c * FFN_FC:(c + 1) * FFN_FC, :] = stg_d[slot].astype(BF16)
            up_chunk(c)
        finish()

    @pl.when(s > 0)
    def _():
        normalize()
        for c in range(FFN_NCH):
            up_chunk(c)
        finish()


def _ffn_tile(layer, with_ple):
    return FFN_TM_FIRST if (layer == 0 and not with_ple) else FFN_TM


def _ffn_call(h, layer, norm_w, w_gu, w_d, ple=None, final_norm=None):
    t = h.shape[0]
    tm = _ffn_tile(layer, ple is not None)
    assert t % tm == 0 and D_FF % FFN_FC == 0
    row_spec = pl.BlockSpec((tm, D_MODEL), lambda s: (s, 0))
    in_specs = [row_spec, _const_spec((1, 1, D_MODEL), (layer, 0, 0)),
                pl.BlockSpec(memory_space=pl.ANY), pl.BlockSpec(memory_space=pl.ANY)]
    args = [h, norm_w, w_gu, w_d]
    scratch = [pltpu.VMEM((tm, D_MODEL), BF16),
               pltpu.VMEM((tm, FFN_FC), F32),
               pltpu.VMEM((tm, D_FF), BF16),
               pltpu.VMEM((FFN_NCH, D_MODEL, FFN_FC), BF16),
               pltpu.VMEM((FFN_NCH, D_MODEL, FFN_FC), BF16),
               pltpu.VMEM((D_FF, D_MODEL), BF16),
               pltpu.VMEM((2, D_MODEL, FFN_FC), F32),
               pltpu.VMEM((2, D_MODEL, FFN_FC), F32),
               pltpu.VMEM((2, FFN_FC, D_MODEL), F32)]
    if ple is not None:
        p, pn, wgate, wup = ple
        in_specs += [pl.BlockSpec((1, tm, PLE_DIM), lambda s: (layer, s, 0)),
                     _const_spec((1, 1, D_MODEL), (layer, 0, 0)),
                     _const_spec((1, D_MODEL, D_MODEL), (layer, 0, 0)),
                     _const_spec((1, PLE_DIM, D_MODEL), (layer, 0, 0))]
        args += [p, pn, wgate, wup]
        scratch += [pltpu.VMEM((D_MODEL, D_MODEL), BF16), pltpu.VMEM((PLE_DIM, D_MODEL), BF16)]
    if final_norm is not None:
        in_specs.append(_const_spec((1, D_MODEL)))
        args.append(final_norm)
    return pl.pallas_call(
        functools.partial(_ffn_kernel, layer=layer, with_ple=ple is not None,
                          with_final=final_norm is not None),
        grid=(t // tm,),
        in_specs=in_specs,
        out_specs=row_spec,
        out_shape=jax.ShapeDtypeStruct((t, D_MODEL), F32),
        scratch_shapes=scratch + [pltpu.SemaphoreType.DMA((3, 2))],
        compiler_params=pltpu.CompilerParams(
            dimension_semantics=("arbitrary",),
            vmem_limit_bytes=_vmem_limit(args, in_specs, row_spec, F32, scratch)),
        name="ffn_ple" if ple is not None else "ffn",
    )(*args)


def _mix_even_kernel(h_ref, nw_ref, win_ref, lng_ref, lnb_ref, ws_ref, bs_ref, scw_ref, wout_ref,
                     o_ref, win_scr, wout_scr, wt_scr, hn_ref, u_ref, bg_ref, vn_ref, q_ref, y_ref):
    tm = MIX_TM
    i = pl.program_id(0)
    j = pl.program_id(1)

    @pl.when((i == 0) & (j == 0))
    def _():
        _cast_rows(win_scr, win_ref)
        _cast_rows(wout_scr, wout_ref)
        row = lax.broadcasted_iota(jnp.int32, (CHUNK, CHUNK), 0)
        col = lax.broadcasted_iota(jnp.int32, (CHUNK, CHUNK), 1)
        for g in range(N_GROUPS):
            wt_scr[g] = jnp.where(col <= row, ws_ref[0, g], 0.0).astype(BF16)

    @pl.when(j == 0)
    def _():
        q_ref[0:SC_HALO, :] = jnp.zeros((SC_HALO, HALF), F32)

    x = h_ref[0]
    hn_ref[...] = _rmsnorm(x, nw_ref[0]).astype(BF16)

    v = _gelu_tanh(_dot(hn_ref[...], win_scr[:, HALF:2 * HALF]))
    for g in range(N_GROUPS):
        gsl = slice(g * GROUP_W, (g + 1) * GROUP_W)
        vn_ref[:, gsl] = _layernorm(v[:, gsl], lng_ref[0][:, gsl], lnb_ref[0][:, gsl]).astype(BF16)
    u_ref[...] = _dot(hn_ref[...], win_scr[:, 0:HALF])
    q_ref[SC_HALO:SC_HALO + tm, :] = (_dot(hn_ref[...], win_scr[:, 3 * HALF:4 * HALF])
                                      * _dot(hn_ref[...], win_scr[:, 4 * HALF:5 * HALF]))
    bg_ref[...] = _dot(hn_ref[...], win_scr[:, 2 * HALF:3 * HALF])
    for g in range(N_GROUPS):
        gsl = slice(g * GROUP_W, (g + 1) * GROUP_W)
        for c in range(tm // CHUNK):
            rsl = slice(c * CHUNK, (c + 1) * CHUNK)
            mixed = _dot(wt_scr[g], vn_ref[rsl, gsl]) + bs_ref[:, gsl]
            y_ref[rsl, gsl] = (_gelu_tanh(u_ref[rsl, gsl]) * mixed).astype(BF16)
    o_ref[0] = x + _dot(y_ref[:, 0:HALF], wout_scr[0:HALF, :])

    conv = scw_ref[0, 2:3, :] * q_ref[SC_HALO:SC_HALO + tm, :]
    conv = conv + scw_ref[0, 1:2, :] * q_ref[SC_HALO - 1:SC_HALO - 1 + tm, :]
    conv = conv + scw_ref[0, 0:1, :] * q_ref[SC_HALO - 2:SC_HALO - 2 + tm, :]
    y_ref[:, HALF:2 * HALF] = (bg_ref[...] * conv).astype(BF16)
    q_ref[0:SC_HALO, :] = q_ref[tm:tm + SC_HALO, :]

    o_ref[0] = o_ref[0] + _dot(y_ref[:, HALF:2 * HALF], wout_scr[HALF:2 * HALF, :])


def _mix_even_call(h3, layer, j, norm_w, w_in, ln_g, ln_b, w_s, b_full, sc_w, w_out):
    b, s, _ = h3.shape
    tm = MIX_TM
    assert s % tm == 0 and tm % CHUNK == 0 and SC_HALO >= SC_WIDTH - 1
    row_spec = pl.BlockSpec((1, tm, D_MODEL), lambda i, t: (i, t, 0))
    in_specs = [row_spec, _const_spec((1, 1, D_MODEL), (layer, 0, 0)),
                _const_spec((1, D_MODEL, 5 * HALF), (j, 0, 0)),
                _const_spec((1, 1, HALF), (j, 0, 0)), _const_spec((1, 1, HALF), (j, 0, 0)),
                _const_spec((1, N_GROUPS, CHUNK, CHUNK), (j, 0, 0, 0)), _const_spec((CHUNK, HALF)),
                _const_spec((1, SC_WIDTH, HALF), (j, 0, 0)),
                _const_spec((1, D_MODEL, D_MODEL), (j, 0, 0))]
    args = [h3, norm_w, w_in, ln_g, ln_b, w_s, b_full, sc_w, w_out]
    scratch = [pltpu.VMEM((D_MODEL, 5 * HALF), BF16),
               pltpu.VMEM((D_MODEL, D_MODEL), BF16),
               pltpu.VMEM((N_GROUPS, CHUNK, CHUNK), BF16),
               pltpu.VMEM((tm, D_MODEL), BF16),
               pltpu.VMEM((tm, HALF), F32),
               pltpu.VMEM((tm, HALF), F32),
               pltpu.VMEM((tm, HALF), BF16),
               pltpu.VMEM((SC_HALO + tm, HALF), F32),
               pltpu.VMEM((tm, D_MODEL), BF16)]
    return pl.pallas_call(
        _mix_even_kernel,
        grid=(b, s // tm),
        in_specs=in_specs,
        out_specs=row_spec,
        out_shape=jax.ShapeDtypeStruct(h3.shape, F32),
        scratch_shapes=scratch,
        compiler_params=pltpu.CompilerParams(
            dimension_semantics=("arbitrary", "arbitrary"),
            vmem_limit_bytes=_vmem_limit(args, in_specs, row_spec, F32, scratch)),
        name="mix_even",
    )(*args)


class _RowView:
    def __init__(self, ref, base):
        self.ref, self.base = ref, base

    def _shift(self, idx):
        rows, lanes = idx
        if isinstance(rows, slice):
            rows = slice(rows.start + self.base, rows.stop + self.base)
        else:
            rows = pl.ds(rows.start + self.base, rows.size)
        return rows, lanes

    def __getitem__(self, idx):
        return self.ref[self._shift(idx)]

    def __setitem__(self, idx, val):
        self.ref[self._shift(idx)] = val


def _mix_odd_kernel(zs_ref, h_ref, nw_ref, win_ref, pw_ref, ps_ref, cvw_ref, cvb_ref, lng_ref,
                    lnb_ref, wout_ref, o_ref, win_scr, wout_scr, pw_scr, hn_ref, p_ref, sa_ref,
                    sb_ref, raw_ref, *rest):
    c_refs, y_ref = rest[:-1], rest[-1]
    n_lane = HALF // GLU_FC

    unit_rows = HALO + 2 * (MIX_TM_ODD // ODD_PARTS)

    def c_ref_of(part, c):
        u = part * n_lane + c
        return _RowView(c_refs[(u + 1) // 2], unit_rows * ((u + 1) % 2) if u > 0 else 0)

    tm = MIX_TM_ODD
    n = HALO + tm
    rp = tm // ODD_PARTS
    npart = HALO + rp
    i = pl.program_id(0)
    j = pl.program_id(1)

    @pl.when((i == 0) & (j == 0))
    def _():
        _cast_rows(win_scr, win_ref)
        _cast_rows(wout_scr, wout_ref)
        for g in range(N_GROUPS):
            pw_scr[g] = pw_ref[0, g].astype(BF16)

    @pl.when(j == 0)
    def _():
        p_ref[0:HALO, :] = jnp.zeros((HALO, HALF), F32)
        for c in range(n_lane):
            c_ref_of(0, c)[0:HALO, :] = jnp.zeros((HALO, GLU_FC), F32)

    zero = zs_ref[0]
    n_units = ODD_PARTS * n_lane
    anchor_count = {}

    def anchor(acc, u):
        if u < n_units:
            ref = c_refs[(u + 1) // 2]
            slot = anchor_count.get(u, 0)
            anchor_count[u] = slot + 1
            spare = ref.shape[0] - ANCHOR_SLOTS * SUBLANES + slot * SUBLANES
            ref[spare:spare + SUBLANES, :] = acc[acc.shape[0] - SUBLANES:, acc.shape[1] - GLU_FC:]

    def front(part):
        rows = slice(part * rp, (part + 1) * rp)
        for c in range(n_lane):
            a = _dot(hn_ref[rows, :], win_scr[:, HALF + c * GLU_FC:HALF + (c + 1) * GLU_FC])
            g = _dot(hn_ref[rows, :], win_scr[:, 2 * HALF + c * GLU_FC:2 * HALF + (c + 1) * GLU_FC])
            if part == 0:
                c_ref_of(part, c)[HALO:npart, :] = a * _sigmoid(g)
            else:
                raw_ref[c, :, 0:GLU_FC] = a
                raw_ref[c, :, GLU_FC:2 * GLU_FC] = g
        pin = _dot(hn_ref[rows, :], win_scr[:, 0:HALF])
        p_ref[HALO + part * rp:HALO + (part + 1) * rp, :] = pin

    def glu(part):
        for c in range(n_lane):
            if part > 0:
                c_ref_of(part, c)[0:HALO, :] = c_ref_of(part - 1, c)[rp:npart, :]
                c_ref_of(part, c)[HALO:npart, :] = (raw_ref[c, :, 0:GLU_FC]
                                                    * _sigmoid(raw_ref[c, :, GLU_FC:2 * GLU_FC]))

    def pool(part):
        rows = slice(part * rp, (part + 1) * rp)
        b0 = part * rp
        e = b0 + npart
        g1, g2, g3 = GROUP_W, 2 * GROUP_W, 3 * GROUP_W
        sa_ref[b0 + 8:e, :] = p_ref[b0 + 8:e, :] + p_ref[b0 + 7:e - 1, :]
        sb_ref[b0 + 16:e, g1:] = sa_ref[b0 + 16:e, g1:] + sa_ref[b0 + 14:e - 2, g1:]
        sa_ref[b0 + 24:e, g2:] = sb_ref[b0 + 24:e, g2:] + sb_ref[b0 + 20:e - 4, g2:]
        sb_ref[b0 + 32:e, g3:] = sa_ref[b0 + 32:e, g3:] + sa_ref[b0 + 24:e - 8, g3:]
        head = slice(HALO, HALO + max(POOL_WINDOWS))
        pos = lax.broadcasted_iota(jnp.int32, (max(POOL_WINDOWS), GROUP_W), 0)
        for gi, win in enumerate(POOL_WINDOWS):
            gsl = slice(gi * GROUP_W, (gi + 1) * GROUP_W)
            s_ref = sa_ref if gi % 2 == 0 else sb_ref
            if part == 0:
                count = jnp.minimum(pos + 1, win).astype(F32)
                s_ref[head, gsl] = s_ref[head, gsl] * jnp.where(j == 0, float(win) / count, 1.0)
            pooled = s_ref[b0 + HALO:e, gsl] * (1.0 / win) - p_ref[b0 + HALO:e, gsl]
            yc = _dot(pooled.astype(BF16), pw_scr[gi]) * ps_ref[0][:, gsl]
            y_ref[rows, gsl] = yc.astype(BF16)
        acc = _dot(y_ref[rows, 0:HALF], wout_scr[0:HALF, :])
        o_ref[0, rows, :] = h_ref[0, rows, :] + acc
        anchor(acc, (part + 1) * n_lane + n_lane - 1)

    def conv(part):
        for c in range(n_lane):
            c_ref = c_ref_of(part, c)
            for r in range(rp // CV_ROWS):
                r0 = pl.multiple_of(zero + r * CV_ROWS, CV_ROWS)
                for lt in range(GLU_FC // LANES):
                    lsl = slice(lt * LANES, (lt + 1) * LANES)
                    wsl = slice(c * GLU_FC + lt * LANES, c * GLU_FC + (lt + 1) * LANES)
                    chunk = c_ref[pl.ds(r0, HALO + CV_ROWS), lsl]
                    acc = None
                    for b in range(SUBLANES):
                        rolled = chunk if b == 0 else pltpu.roll(chunk, b, axis=0)
                        for a in range(HALO // SUBLANES):
                            d = SUBLANES * a + b
                            if d >= CV_WIDTH:
                                continue
                            lo = HALO - SUBLANES * a
                            src = rolled[lo:lo + CV_ROWS].reshape(CV_ROWS // SUBLANES, SUBLANES, LANES)
                            term = cvw_ref[CV_WIDTH - 1 - d, :, wsl][None] * src
                            acc = term if acc is None else acc + term
                    acc = acc + cvb_ref[0][:, wsl][None]
                    c_ref[pl.ds(r0 + npart, CV_ROWS), lsl] = acc.reshape(CV_ROWS, LANES)

    def back(part):
        rows = slice(part * rp, (part + 1) * rp)
        hc = jnp.concatenate([c_ref_of(part, c)[npart:npart + rp, :] for c in range(n_lane)], axis=-1)
        ln = _layernorm(hc, lng_ref[0], lnb_ref[0])
        y_ref[rows, HALF:2 * HALF] = _silu(ln).astype(BF16)
        acc = _dot(y_ref[rows, HALF:2 * HALF], wout_scr[HALF:2 * HALF, :])
        o_ref[0, rows, :] = o_ref[0, rows, :] + acc
        anchor(acc, (part + 1) * n_lane + n_lane - 1)

    hn_ref[...] = _rmsnorm(h_ref[0], nw_ref[0]).astype(BF16)
    front(0)
    anchor(hn_ref[tm - 2 * SUBLANES:tm, :].astype(F32), 0)
    for part in range(ODD_PARTS):
        if part + 1 < ODD_PARTS:
            front(part + 1)
        pool(part)
        glu(part)
        conv(part)
        back(part)
    p_ref[0:HALO, :] = p_ref[tm:n, :]
    for c in range(n_lane):
        c_ref_of(0, c)[0:HALO, :] = c_ref_of(ODD_PARTS - 1, c)[rp:npart, :]


def _mix_odd_call(h3, layer, j, norm_w, w_in, pool_w, pool_scale, cv_w8, cv_b, ln_g, ln_b, w_out):
    b, s, _ = h3.shape
    tm = MIX_TM_ODD
    tiles = s // tm
    assert s % tm == 0 and (tm // ODD_PARTS) % CV_ROWS == 0 and HALO >= CV_WIDTH - 1 and ODD_PARTS == 2
    assert HALO >= 2 * max(POOL_WINDOWS) - 1 and tm >= max(POOL_WINDOWS)
    row_spec = pl.BlockSpec((1, tm, D_MODEL), lambda i, t: (i, t, 0))
    vec = _const_spec((1, 1, HALF), (j, 0, 0))
    in_specs = [pl.BlockSpec(memory_space=pltpu.SMEM), row_spec,
                _const_spec((1, 1, D_MODEL), (layer, 0, 0)),
                _const_spec((1, D_MODEL, 3 * HALF), (j, 0, 0)),
                _const_spec((1, N_GROUPS, GROUP_W, GROUP_W), (j, 0, 0, 0)), vec,
                _const_spec((CV_WIDTH, SUBLANES, HALF)), vec, vec, vec,
                _const_spec((1, D_MODEL, D_MODEL), (j, 0, 0))]
    args = [jnp.zeros((1,), jnp.int32), h3, norm_w, w_in, pool_w, pool_scale, cv_w8, cv_b, ln_g, ln_b,
            w_out]
    scratch = [pltpu.VMEM((D_MODEL, 3 * HALF), BF16),
               pltpu.VMEM((D_MODEL, D_MODEL), BF16),
               pltpu.VMEM((N_GROUPS, GROUP_W, GROUP_W), BF16),
               pltpu.VMEM((tm, D_MODEL), BF16),
               pltpu.VMEM((HALO + tm, HALF), F32),
               pltpu.VMEM((HALO + tm, HALF), F32),
               pltpu.VMEM((HALO + tm, HALF), F32),
               pltpu.VMEM((HALF // GLU_FC, tm // ODD_PARTS, 2 * GLU_FC), F32),
               pltpu.VMEM((HALO + 2 * tm // ODD_PARTS + ANCHOR_SLOTS * SUBLANES, GLU_FC), F32),
               *[pltpu.VMEM((2 * (HALO + 2 * tm // ODD_PARTS) + ANCHOR_SLOTS * SUBLANES, GLU_FC), F32)
                 for _ in range(ODD_PARTS * (HALF // GLU_FC) // 2 - 1)],
               pltpu.VMEM((HALO + 2 * tm // ODD_PARTS + ANCHOR_SLOTS * SUBLANES, GLU_FC), F32),
               pltpu.VMEM((tm, D_MODEL), BF16)]
    return pl.pallas_call(
        _mix_odd_kernel,
        grid=(b, tiles),
        in_specs=in_specs,
        out_specs=row_spec,
        out_shape=jax.ShapeDtypeStruct(h3.shape, F32),
        scratch_shapes=scratch,
        compiler_params=pltpu.CompilerParams(
            dimension_semantics=("arbitrary", "arbitrary"),
            vmem_limit_bytes=_vmem_limit(args, in_specs, row_spec, F32, scratch)),
        name="mix_odd",
    )(*args)


def kernel(x, p, ffn1_norm, ffn1_w_gu, ffn1_w_down, mix_norm, ffn2_norm, ffn2_w_gu, ffn2_w_down,
           ple_norm, ple_w_gate, ple_w_up, ab_w_in, gm_ln_g, gm_ln_b, gm_w_s, gm_b_s, sc_w,
           ab_w_out, cd_w_in, pool_w, pool_scale, cv_w, cv_b, cv_ln_g, cv_ln_b, cd_w_out,
           final_norm):
    bsz, seq, d = x.shape
    depth = p.shape[0]
    t = bsz * seq
    vec3 = lambda v: v.reshape(v.shape[0], 1, v.shape[1])
    p3 = p.reshape(depth, t, PLE_DIM)
    h = x.reshape(t, d)
    for i in range(depth):
        j = i // 2
        h = _ffn_call(h, i, vec3(ffn1_norm), ffn1_w_gu, ffn1_w_down)
        h3 = h.reshape(bsz, seq, d)
        if i % 2 == 0:
            b_full = jnp.repeat(gm_b_s[j].T, GROUP_W, axis=1)
            h3 = _mix_even_call(h3, i, j, vec3(mix_norm), ab_w_in, vec3(gm_ln_g), vec3(gm_ln_b),
                                gm_w_s, b_full, sc_w, ab_w_out)
        else:
            cv_w8 = jnp.broadcast_to(cv_w[j][:, None, :], (CV_WIDTH, SUBLANES, HALF))
            h3 = _mix_odd_call(h3, i, j, vec3(mix_norm), cd_w_in, pool_w, vec3(pool_scale), cv_w8,
                               vec3(cv_b), vec3(cv_ln_g), vec3(cv_ln_b), cd_w_out)
        h = h3.reshape(t, d)
        h = _ffn_call(h, i, vec3(ffn2_norm), ffn2_w_gu, ffn2_w_down,
                      ple=(p3, vec3(ple_norm), ple_w_gate, ple_w_up),
                      final_norm=final_norm.reshape(1, d) if i == depth - 1 else None)
    return h.reshape(bsz, seq, d)
```

```python
import functools
import math

import jax
import jax.numpy as jnp
from jax import lax
from jax.experimental import pallas as pl
from jax.experimental.pallas import tpu as pltpu

F32 = jnp.float32
BF16 = jnp.bfloat16

D_MODEL = 1024
PLE_DIM = 256
D_FF = 2816
HALF = 512
LANES = 128
SUBLANES = 8
GROUP_W = 128
N_GROUPS = 4
CHUNK = 128
POOL_WINDOWS = (2, 4, 8, 16)
SC_WIDTH = 3
CV_WIDTH = 31
RMS_EPS = 1e-6
LN_EPS = 1e-5

FFN_TM = 512
FFN_TM_FIRST = 1024
FFN_FC = 256
FFN_NCH = D_FF // FFN_FC
MIX_TM = 1024
MIX_TM_ODD = 1024
ANCHOR_SLOTS = 4
ODD_PARTS = 2
GLU_FC = 256
CAST_ROWS = 256
SC_HALO = 8
HALO = 32
CV_ROWS = 128
SPILL_MARGIN = 6 * 1024 * 1024


def _rms_scale(x):
    return lax.rsqrt(jnp.mean(x * x, axis=-1, keepdims=True) + RMS_EPS)


def _rmsnorm(x, g):
    return x * _rms_scale(x) * g


def _layernorm(x, g, b):
    mu = jnp.mean(x, axis=-1, keepdims=True)
    xc = x - mu
    var = jnp.mean(xc * xc, axis=-1, keepdims=True)
    return xc * lax.rsqrt(var + LN_EPS) * g + b


def _sigmoid(x):
    return 0.5 + 0.5 * jnp.tanh(0.5 * x)


def _silu(x):
    hx = 0.5 * x
    return hx + hx * jnp.tanh(hx)


def _gelu_tanh(x):
    c = 0.7978845608028654
    return 0.5 * x * (1.0 + jnp.tanh(c * (x + 0.044715 * (x * x * x))))


def _dot(a, b):
    return jnp.dot(a, b, preferred_element_type=F32)


def _cast_rows(dst_ref, src_ref):
    rows = src_ref.shape[1]
    step = min(rows, CAST_ROWS)
    for r in range(0, rows, step):
        dst_ref[r:r + step, :] = src_ref[0, r:r + step, :].astype(BF16)


def _const_spec(shape, index=None):
    index = (0,) * len(shape) if index is None else index
    return pl.BlockSpec(shape, lambda *_: index, pipeline_mode=pl.Buffered(1))


def _vmem_limit(args, in_specs, out_spec, out_dtype, scratch):
    nbytes = lambda shape, dtype: math.prod(shape) * jnp.dtype(dtype).itemsize
    total = 2 * nbytes(out_spec.block_shape, out_dtype) + SPILL_MARGIN
    for arg, spec in zip(args, in_specs):
        if spec.block_shape is not None:
            total += (1 if spec.pipeline_mode is not None else 2) * nbytes(spec.block_shape, arg.dtype)
    return total + sum(nbytes(s.shape, s.dtype) for s in scratch)


def _ffn_kernel(*refs, layer, with_ple, with_final):
    h_ref, nw_ref, wgu_hbm, wd_hbm = refs[:4]
    k = 4
    if with_ple:
        p_ref, pn_ref, wgate_ref, wup_ref = refs[k:k + 4]
        k += 4
    if with_final:
        fn_ref = refs[k]
        k += 1
    o_ref, xn_ref, r_ref, a_ref, wg_scr, wu_scr, wd_scr, stg_g, stg_u, stg_d = refs[k:k + 10]
    k += 10
    sem = refs[-1]
    if with_ple:
        wgate_scr, wup_scr = refs[k:k + 2]
    s = pl.program_id(0)

    def normalize():
        x = h_ref[...]
        xn_ref[...] = (x * nw_ref[0]).astype(BF16)
        r_ref[...] = jnp.broadcast_to(_rms_scale(x), r_ref.shape)

    def up_chunk(c):
        sl = slice(c * FFN_FC, (c + 1) * FFN_FC)
        g = r_ref[...] * _dot(xn_ref[...], wg_scr[c])
        u = r_ref[...] * _dot(xn_ref[...], wu_scr[c])
        a_ref[:, sl] = (_silu(g) * u).astype(BF16)

    def finish():
        h1 = h_ref[...] + 0.5 * _dot(a_ref[...], wd_scr[...])
        if with_ple:
            xn_ref[...] = (h1 * pn_ref[0]).astype(BF16)
            gate = _sigmoid(_rms_scale(h1) * _dot(xn_ref[...], wgate_scr[...]))
            up = _dot(p_ref[0].astype(BF16), wup_scr[...])
            h1 = h1 + gate * up
        if with_final:
            h1 = _rmsnorm(h1, fn_ref[...])
        o_ref[...] = h1

    def stage_copies(c):
        slot = c % 2
        cols = pl.ds(c * FFN_FC, FFN_FC)
        ucols = pl.ds(D_FF + c * FFN_FC, FFN_FC)
        return (pltpu.make_async_copy(wgu_hbm.at[layer, pl.ds(0, D_MODEL), cols], stg_g.at[slot],
                                      sem.at[0, slot]),
                pltpu.make_async_copy(wgu_hbm.at[layer, pl.ds(0, D_MODEL), ucols], stg_u.at[slot],
                                      sem.at[1, slot]),
                pltpu.make_async_copy(wd_hbm.at[layer, cols, pl.ds(0, D_MODEL)], stg_d.at[slot],
                                      sem.at[2, slot]))

    @pl.when(s == 0)
    def _():
        for cp in stage_copies(0):
            cp.start()
        normalize()
        if with_ple:
            _cast_rows(wgate_scr, wgate_ref)
            _cast_rows(wup_scr, wup_ref)
        for c in range(FFN_NCH):
            if c + 1 < FFN_NCH:
                for cp in stage_copies(c + 1):
                    cp.start()
            for cp in stage_copies(c):
                cp.wait()
            slot = c % 2
            wg_scr[c] = stg_g[slot].astype(BF16)
            wu_scr[c] = stg_u[slot].astype(BF16)
            wd_scr[c * FFN_FC:(c + 1) * FFN_FC, :] = stg_d[slot].astype(BF16)
            up_chunk(c)
        finish()

    @pl.when(s > 0)
    def _():
        normalize()
        for c in range(FFN_NCH):
            up_chunk(c)
        finish()


def _ffn_tile(layer, with_ple):
    return FFN_TM_FIRST if not with_ple else FFN_TM


def _ffn_call(h, layer, norm_w, w_gu, w_d, ple=None, final_norm=None):
    t = h.shape[0]
    tm = _ffn_tile(layer, ple is not None)
    assert t % tm == 0 and D_FF % FFN_FC == 0
    row_spec = pl.BlockSpec((tm, D_MODEL), lambda s: (s, 0))
    in_specs = [row_spec, _const_spec((1, 1, D_MODEL), (layer, 0, 0)),
                pl.BlockSpec(memory_space=pl.ANY), pl.BlockSpec(memory_space=pl.ANY)]
    args = [h, norm_w, w_gu, w_d]
    scratch = [pltpu.VMEM((tm, D_MODEL), BF16),
               pltpu.VMEM((tm, FFN_FC), F32),
               pltpu.VMEM((tm, D_FF), BF16),
               pltpu.VMEM((FFN_NCH, D_MODEL, FFN_FC), BF16),
               pltpu.VMEM((FFN_NCH, D_MODEL, FFN_FC), BF16),
               pltpu.VMEM((D_FF, D_MODEL), BF16),
               pltpu.VMEM((2, D_MODEL, FFN_FC), F32),
               pltpu.VMEM((2, D_MODEL, FFN_FC), F32),
               pltpu.VMEM((2, FFN_FC, D_MODEL), F32)]
    if ple is not None:
        p, pn, wgate, wup = ple
        in_specs += [pl.BlockSpec((1, tm, PLE_DIM), lambda s: (layer, s, 0)),
                     _const_spec((1, 1, D_MODEL), (layer, 0, 0)),
                     _const_spec((1, D_MODEL, D_MODEL), (layer, 0, 0)),
                     _const_spec((1, PLE_DIM, D_MODEL), (layer, 0, 0))]
        args += [p, pn, wgate, wup]
        scratch += [pltpu.VMEM((D_MODEL, D_MODEL), BF16), pltpu.VMEM((PLE_DIM, D_MODEL), BF16)]
    if final_norm is not None:
        in_specs.append(_const_spec((1, D_MODEL)))
        args.append(final_norm)
    return pl.pallas_call(
        functools.partial(_ffn_kernel, layer=layer, with_ple=ple is not None,
                          with_final=final_norm is not None),
        grid=(t // tm,),
        in_specs=in_specs,
        out_specs=row_spec,
        out_shape=jax.ShapeDtypeStruct((t, D_MODEL), F32),
        scratch_shapes=scratch + [pltpu.SemaphoreType.DMA((3, 2))],
        compiler_params=pltpu.CompilerParams(
            dimension_semantics=("arbitrary",),
            vmem_limit_bytes=_vmem_limit(args, in_specs, row_spec, F32, scratch)),
        name="ffn_ple" if ple is not None else "ffn",
    )(*args)


def _mix_even_kernel(h_ref, nw_ref, win_ref, lng_ref, lnb_ref, ws_ref, bs_ref, scw_ref, wout_ref,
                     o_ref, win_scr, wout_scr, wt_scr, hn_ref, u_ref, bg_ref, vn_ref, q_ref, y_ref):
    tm = MIX_TM
    i = pl.program_id(0)
    j = pl.program_id(1)

    @pl.when((i == 0) & (j == 0))
    def _():
        _cast_rows(win_scr, win_ref)
        _cast_rows(wout_scr, wout_ref)
        row = lax.broadcasted_iota(jnp.int32, (CHUNK, CHUNK), 0)
        col = lax.broadcasted_iota(jnp.int32, (CHUNK, CHUNK), 1)
        for g in range(N_GROUPS):
            wt_scr[g] = jnp.where(col <= row, ws_ref[0, g], 0.0).astype(BF16)

    @pl.when(j == 0)
    def _():
        q_ref[0:SC_HALO, :] = jnp.zeros((SC_HALO, HALF), F32)

    x = h_ref[0]
    hn_ref[...] = _rmsnorm(x, nw_ref[0]).astype(BF16)

    v = _gelu_tanh(_dot(hn_ref[...], win_scr[:, HALF:2 * HALF]))
    for g in range(N_GROUPS):
        gsl = slice(g * GROUP_W, (g + 1) * GROUP_W)
        vn_ref[:, gsl] = _layernorm(v[:, gsl], lng_ref[0][:, gsl], lnb_ref[0][:, gsl]).astype(BF16)
    u_ref[...] = _dot(hn_ref[...], win_scr[:, 0:HALF])
    q_ref[SC_HALO:SC_HALO + tm, :] = (_dot(hn_ref[...], win_scr[:, 3 * HALF:4 * HALF])
                                      * _dot(hn_ref[...], win_scr[:, 4 * HALF:5 * HALF]))
    bg_ref[...] = _dot(hn_ref[...], win_scr[:, 2 * HALF:3 * HALF])
    for g in range(N_GROUPS):
        gsl = slice(g * GROUP_W, (g + 1) * GROUP_W)
        for c in range(tm // CHUNK):
            rsl = slice(c * CHUNK, (c + 1) * CHUNK)
            mixed = _dot(wt_scr[g], vn_ref[rsl, gsl]) + bs_ref[:, gsl]
            y_ref[rsl, gsl] = (_gelu_tanh(u_ref[rsl, gsl]) * mixed).astype(BF16)
    o_ref[0] = x + _dot(y_ref[:, 0:HALF], wout_scr[0:HALF, :])

    conv = scw_ref[0, 2:3, :] * q_ref[SC_HALO:SC_HALO + tm, :]
    conv = conv + scw_ref[0, 1:2, :] * q_ref[SC_HALO - 1:SC_HALO - 1 + tm, :]
    conv = conv + scw_ref[0, 0:1, :] * q_ref[SC_HALO - 2:SC_HALO - 2 + tm, :]
    y_ref[:, HALF:2 * HALF] = (bg_ref[...] * conv).astype(BF16)
    q_ref[0:SC_HALO, :] = q_ref[tm:tm + SC_HALO, :]

    o_ref[0] = o_ref[0] + _dot(y_ref[:, HALF:2 * HALF], wout_scr[HALF:2 * HALF, :])


def _mix_even_call(h3, layer, j, norm_w, w_in, ln_g, ln_b, w_s, b_full, sc_w, w_out):
    b, s, _ = h3.shape
    tm = MIX_TM
    assert s % tm == 0 and tm % CHUNK == 0 and SC_HALO >= SC_WIDTH - 1
    row_spec = pl.BlockSpec((1, tm, D_MODEL), lambda i, t: (i, t, 0))
    in_specs = [row_spec, _const_spec((1, 1, D_MODEL), (layer, 0, 0)),
                _const_spec((1, D_MODEL, 5 * HALF), (j, 0, 0)),
                _const_spec((1, 1, HALF), (j, 0, 0)), _const_spec((1, 1, HALF), (j, 0, 0)),
                _const_spec((1, N_GROUPS, CHUNK, CHUNK), (j, 0, 0, 0)), _const_spec((CHUNK, HALF)),
                _const_spec((1, SC_WIDTH, HALF), (j, 0, 0)),
                _const_spec((1, D_MODEL, D_MODEL), (j, 0, 0))]
    args = [h3, norm_w, w_in, ln_g, ln_b, w_s, b_full, sc_w, w_out]
    scratch = [pltpu.VMEM((D_MODEL, 5 * HALF), BF16),
               pltpu.VMEM((D_MODEL, D_MODEL), BF16),
               pltpu.VMEM((N_GROUPS, CHUNK, CHUNK), BF16),
               pltpu.VMEM((tm, D_MODEL), BF16),
               pltpu.VMEM((tm, HALF), F32),
               pltpu.VMEM((tm, HALF), F32),
               pltpu.VMEM((tm, HALF), BF16),
               pltpu.VMEM((SC_HALO + tm, HALF), F32),
               pltpu.VMEM((tm, D_MODEL), BF16)]
    return pl.pallas_call(
        _mix_even_kernel,
        grid=(b, s // tm),
        in_specs=in_specs,
        out_specs=row_spec,
        out_shape=jax.ShapeDtypeStruct(h3.shape, F32),
        scratch_shapes=scratch,
        compiler_params=pltpu.CompilerParams(
            dimension_semantics=("arbitrary", "arbitrary"),
            vmem_limit_bytes=_vmem_limit(args, in_specs, row_spec, F32, scratch)),
        name="mix_even",
    )(*args)


class _RowView:
    def __init__(self, ref, base):
        self.ref, self.base = ref, base

    def _shift(self, idx):
        rows, lanes = idx
        if isinstance(rows, slice):
            rows = slice(rows.start + self.base, rows.stop + self.base)
        else:
            rows = pl.ds(rows.start + self.base, rows.size)
        return rows, lanes

    def __getitem__(self, idx):
        return self.ref[self._shift(idx)]

    def __setitem__(self, idx, val):
        self.ref[self._shift(idx)] = val


def _mix_odd_kernel(zs_ref, h_ref, nw_ref, win_ref, pw_ref, ps_ref, cvw_ref, cvb_ref, lng_ref,
                    lnb_ref, wout_ref, o_ref, win_scr, wout_scr, pw_scr, hn_ref, p_ref, sa_ref,
                    sb_ref, raw_ref, *rest):
    c_refs, y_ref = rest[:-1], rest[-1]
    n_lane = HALF // GLU_FC

    unit_rows = HALO + 2 * (MIX_TM_ODD // ODD_PARTS)

    def c_ref_of(part, c):
        u = part * n_lane + c
        return _RowView(c_refs[(u + 1) // 2], unit_rows * ((u + 1) % 2) if u > 0 else 0)

    tm = MIX_TM_ODD
    n = HALO + tm
    rp = tm // ODD_PARTS
    npart = HALO + rp
    i = pl.program_id(0)
    j = pl.program_id(1)

    @pl.when((i == 0) & (j == 0))
    def _():
        _cast_rows(win_scr, win_ref)
        _cast_rows(wout_scr, wout_ref)
        for g in range(N_GROUPS):
            pw_scr[g] = pw_ref[0, g].astype(BF16)

    @pl.when(j == 0)
    def _():
        p_ref[0:HALO, :] = jnp.zeros((HALO, HALF), F32)
        for c in range(n_lane):
            c_ref_of(0, c)[0:HALO, :] = jnp.zeros((HALO, GLU_FC), F32)

    zero = zs_ref[0]
    n_units = ODD_PARTS * n_lane
    anchor_count = {}

    def anchor(acc, u):
        if u < n_units:
            ref = c_refs[(u + 1) // 2]
            slot = anchor_count.get(u, 0)
            anchor_count[u] = slot + 1
            spare = ref.shape[0] - ANCHOR_SLOTS * SUBLANES + slot * SUBLANES
            ref[spare:spare + SUBLANES, :] = acc[acc.shape[0] - SUBLANES:, acc.shape[1] - GLU_FC:]

    def front(part):
        rows = slice(part * rp, (part + 1) * rp)
        for c in range(n_lane):
            a = _dot(hn_ref[rows, :], win_scr[:, HALF + c * GLU_FC:HALF + (c + 1) * GLU_FC])
            g = _dot(hn_ref[rows, :], win_scr[:, 2 * HALF + c * GLU_FC:2 * HALF + (c + 1) * GLU_FC])
            if part == 0:
                c_ref_of(part, c)[HALO:npart, :] = a * _sigmoid(g)
            else:
                raw_ref[c, :, 0:GLU_FC] = a
                raw_ref[c, :, GLU_FC:2 * GLU_FC] = g
        pin = _dot(hn_ref[rows, :], win_scr[:, 0:HALF])
        p_ref[HALO + part * rp:HALO + (part + 1) * rp, :] = pin

    def glu(part):
        for c in range(n_lane):
            if part > 0:
                c_ref_of(part, c)[0:HALO, :] = c_ref_of(part - 1, c)[rp:npart, :]
                c_ref_of(part, c)[HALO:npart, :] = (raw_ref[c, :, 0:GLU_FC]
                                                    * _sigmoid(raw_ref[c, :, GLU_FC:2 * GLU_FC]))

    def pool(part):
        rows = slice(part * rp, (part + 1) * rp)
        b0 = part * rp
        e = b0 + npart
        g1, g2, g3 = GROUP_W, 2 * GROUP_W, 3 * GROUP_W
        sa_ref[b0 + 8:e, :] = p_ref[b0 + 8:e, :] + p_ref[b0 + 7:e - 1, :]
        sb_ref[b0 + 16:e, g1:] = sa_ref[b0 + 16:e, g1:] + sa_ref[b0 + 14:e - 2, g1:]
        sa_ref[b0 + 24:e, g2:] = sb_ref[b0 + 24:e, g2:] + sb_ref[b0 + 20:e - 4, g2:]
        sb_ref[b0 + 32:e, g3:] = sa_ref[b0 + 32:e, g3:] + sa_ref[b0 + 24:e - 8, g3:]
        head = slice(HALO, HALO + max(POOL_WINDOWS))
        pos = lax.broadcasted_iota(jnp.int32, (max(POOL_WINDOWS), GROUP_W), 0)
        for gi, win in enumerate(POOL_WINDOWS):
            gsl = slice(gi * GROUP_W, (gi + 1) * GROUP_W)
            s_ref = sa_ref if gi % 2 == 0 else sb_ref
            if part == 0:
                count = jnp.minimum(pos + 1, win).astype(F32)
                s_ref[head, gsl] = s_ref[head, gsl] * jnp.where(j == 0, float(win) / count, 1.0)
            pooled = s_ref[b0 + HALO:e, gsl] * (1.0 / win) - p_ref[b0 + HALO:e, gsl]
            yc = _dot(pooled.astype(BF16), pw_scr[gi]) * ps_ref[0][:, gsl]
            y_ref[rows, gsl] = yc.astype(BF16)
        acc = _dot(y_ref[rows, 0:HALF], wout_scr[0:HALF, :])
        o_ref[0, rows, :] = h_ref[0, rows, :] + acc
        anchor(acc, (part + 1) * n_lane + n_lane - 1)

    def conv(part):
        for c in range(n_lane):
            c_ref = c_ref_of(part, c)
            for r in range(rp // CV_ROWS):
                r0 = pl.multiple_of(zero + r * CV_ROWS, CV_ROWS)
                for lt in range(GLU_FC // LANES):
                    lsl = slice(lt * LANES, (lt + 1) * LANES)
                    wsl = slice(c * GLU_FC + lt * LANES, c * GLU_FC + (lt + 1) * LANES)
                    chunk = c_ref[pl.ds(r0, HALO + CV_ROWS), lsl]
                    acc = None
                    for b in range(SUBLANES):
                        rolled = chunk if b == 0 else pltpu.roll(chunk, b, axis=0)
                        for a in range(HALO // SUBLANES):
                            d = SUBLANES * a + b
                            if d >= CV_WIDTH:
                                continue
                            lo = HALO - SUBLANES * a
                            src = rolled[lo:lo + CV_ROWS].reshape(CV_ROWS // SUBLANES, SUBLANES, LANES)
                            term = cvw_ref[CV_WIDTH - 1 - d, :, wsl][None] * src
                            acc = term if acc is None else acc + term
                    acc = acc + cvb_ref[0][:, wsl][None]
                    c_ref[pl.ds(r0 + npart, CV_ROWS), lsl] = acc.reshape(CV_ROWS, LANES)

    def back(part):
        rows = slice(part * rp, (part + 1) * rp)
        hc = jnp.concatenate([c_ref_of(part, c)[npart:npart + rp, :] for c in range(n_lane)], axis=-1)
        ln = _layernorm(hc, lng_ref[0], lnb_ref[0])
        y_ref[rows, HALF:2 * HALF] = _silu(ln).astype(BF16)
        acc = _dot(y_ref[rows, HALF:2 * HALF], wout_scr[HALF:2 * HALF, :])
        o_ref[0, rows, :] = o_ref[0, rows, :] + acc
        anchor(acc, (part + 1) * n_lane + n_lane - 1)

    hn_ref[...] = _rmsnorm(h_ref[0], nw_ref[0]).astype(BF16)
    front(0)
    anchor(hn_ref[tm - 2 * SUBLANES:tm, :].astype(F32), 0)
    for part in range(ODD_PARTS):
        if part + 1 < ODD_PARTS:
            front(part + 1)
        pool(part)
        glu(part)
        conv(part)
        back(part)
    p_ref[0:HALO, :] = p_ref[tm:n, :]
    for c in range(n_lane):
        c_ref_of(0, c)[0:HALO, :] = c_ref_of(ODD_PARTS - 1, c)[rp:npart, :]


def _mix_odd_call(h3, layer, j, norm_w, w_in, pool_w, pool_scale, cv_w8, cv_b, ln_g, ln_b, w_out):
    b, s, _ = h3.shape
    tm = MIX_TM_ODD
    tiles = s // tm
    assert s % tm == 0 and (tm // ODD_PARTS) % CV_ROWS == 0 and HALO >= CV_WIDTH - 1 and ODD_PARTS == 2
    assert HALO >= 2 * max(POOL_WINDOWS) - 1 and tm >= max(POOL_WINDOWS)
    row_spec = pl.BlockSpec((1, tm, D_MODEL), lambda i, t: (i, t, 0))
    vec = _const_spec((1, 1, HALF), (j, 0, 0))
    in_specs = [pl.BlockSpec(memory_space=pltpu.SMEM), row_spec,
                _const_spec((1, 1, D_MODEL), (layer, 0, 0)),
                _const_spec((1, D_MODEL, 3 * HALF), (j, 0, 0)),
                _const_spec((1, N_GROUPS, GROUP_W, GROUP_W), (j, 0, 0, 0)), vec,
                _const_spec((CV_WIDTH, SUBLANES, HALF)), vec, vec, vec,
                _const_spec((1, D_MODEL, D_MODEL), (j, 0, 0))]
    args = [jnp.zeros((1,), jnp.int32), h3, norm_w, w_in, pool_w, pool_scale, cv_w8, cv_b, ln_g, ln_b,
            w_out]
    scratch = [pltpu.VMEM((D_MODEL, 3 * HALF), BF16),
               pltpu.VMEM((D_MODEL, D_MODEL), BF16),
               pltpu.VMEM((N_GROUPS, GROUP_W, GROUP_W), BF16),
               pltpu.VMEM((tm, D_MODEL), BF16),
               pltpu.VMEM((HALO + tm, HALF), F32),
               pltpu.VMEM((HALO + tm, HALF), F32),
               pltpu.VMEM((HALO + tm, HALF), F32),
               pltpu.VMEM((HALF // GLU_FC, tm // ODD_PARTS, 2 * GLU_FC), F32),
               pltpu.VMEM((HALO + 2 * tm // ODD_PARTS + ANCHOR_SLOTS * SUBLANES, GLU_FC), F32),
               *[pltpu.VMEM((2 * (HALO + 2 * tm // ODD_PARTS) + ANCHOR_SLOTS * SUBLANES, GLU_FC), F32)
                 for _ in range(ODD_PARTS * (HALF // GLU_FC) // 2 - 1)],
               pltpu.VMEM((HALO + 2 * tm // ODD_PARTS + ANCHOR_SLOTS * SUBLANES, GLU_FC), F32),
               pltpu.VMEM((tm, D_MODEL), BF16)]
    return pl.pallas_call(
        _mix_odd_kernel,
        grid=(b, tiles),
        in_specs=in_specs,
        out_specs=row_spec,
        out_shape=jax.ShapeDtypeStruct(h3.shape, F32),
        scratch_shapes=scratch,
        compiler_params=pltpu.CompilerParams(
            dimension_semantics=("arbitrary", "arbitrary"),
            vmem_limit_bytes=_vmem_limit(args, in_specs, row_spec, F32, scratch)),
        name="mix_odd",
    )(*args)


def kernel(x, p, ffn1_norm, ffn1_w_gu, ffn1_w_down, mix_norm, ffn2_norm, ffn2_w_gu, ffn2_w_down,
           ple_norm, ple_w_gate, ple_w_up, ab_w_in, gm_ln_g, gm_ln_b, gm_w_s, gm_b_s, sc_w,
           ab_w_out, cd_w_in, pool_w, pool_scale, cv_w, cv_b, cv_ln_g, cv_ln_b, cd_w_out,
           final_norm):
    bsz, seq, d = x.shape
    depth = p.shape[0]
    t = bsz * seq
    vec3 = lambda v: v.reshape(v.shape[0], 1, v.shape[1])
    p3 = p.reshape(depth, t, PLE_DIM)
    h = x.reshape(t, d)
    for i in range(depth):
        j = i // 2
        h = _ffn_call(h, i, vec3(ffn1_norm), ffn1_w_gu, ffn1_w_down)
        h3 = h.reshape(bsz, seq, d)
        if i % 2 == 0:
            b_full = jnp.repeat(gm_b_s[j].T, GROUP_W, axis=1)
            h3 = _mix_even_call(h3, i, j, vec3(mix_norm), ab_w_in, vec3(gm_ln_g), vec3(gm_ln_b),
                                gm_w_s, b_full, sc_w, ab_w_out)
        else:
            cv_w8 = jnp.broadcast_to(cv_w[j][:, None, :], (CV_WIDTH, SUBLANES, HALF))
            h3 = _mix_odd_call(h3, i, j, vec3(mix_norm), cd_w_in, pool_w, vec3(pool_scale), cv_w8,
                               vec3(cv_b), vec3(cv_ln_g), vec3(cv_ln_b), cd_w_out)
        h = h3.reshape(t, d)
        h = _ffn_call(h, i, vec3(ffn2_norm), ffn2_w_gu, ffn2_w_down,
                      ple=(p3, vec3(ple_norm), ple_w_gate, ple_w_up),
                      final_norm=final_norm.reshape(1, d) if i == depth - 1 else None)
    return h.reshape(bsz, seq, d)
```

```python
import functools
import math

import jax
import jax.numpy as jnp
from jax import lax
from jax.experimental import pallas as pl
from jax.experimental.pallas import tpu as pltpu

F32 = jnp.float32
BF16 = jnp.bfloat16

D_MODEL = 1024
PLE_DIM = 256
D_FF = 2816
HALF = 512
LANES = 128
SUBLANES = 8
GROUP_W = 128
N_GROUPS = 4
CHUNK = 128
POOL_WINDOWS = (2, 4, 8, 16)
SC_WIDTH = 3
CV_WIDTH = 31
RMS_EPS = 1e-6
LN_EPS = 1e-5

FFN_TM = 512
FFN_TM_FIRST = 1024
FFN_FC = 256
FFN_NCH = D_FF // FFN_FC
MIX_TM = 1024
MIX_TM_ODD = 1024
ANCHOR_SLOTS = 4
ODD_PARTS = 2
GLU_FC = 256
CAST_ROWS = 256
SC_HALO = 8
HALO = 32
CV_ROWS = 128
SPILL_MARGIN = 6 * 1024 * 1024


def _rms_scale(x):
    return lax.rsqrt(jnp.mean(x * x, axis=-1, keepdims=True) + RMS_EPS)


def _rmsnorm(x, g):
    return x * _rms_scale(x) * g


def _layernorm(x, g, b):
    mu = jnp.mean(x, axis=-1, keepdims=True)
    xc = x - mu
    var = jnp.mean(xc * xc, axis=-1, keepdims=True)
    return xc * lax.rsqrt(var + LN_EPS) * g + b


def _sigmoid(x):
    return 0.5 + 0.5 * jnp.tanh(0.5 * x)


def _silu(x):
    hx = 0.5 * x
    return hx + hx * jnp.tanh(hx)


def _gelu_tanh(x):
    c = 0.7978845608028654
    return 0.5 * x * (1.0 + jnp.tanh(c * (x + 0.044715 * (x * x * x))))


def _dot(a, b):
    return jnp.dot(a, b, preferred_element_type=F32)


def _cast_rows(dst_ref, src_ref):
    rows = src_ref.shape[1]
    step = min(rows, CAST_ROWS)
    for r in range(0, rows, step):
        dst_ref[r:r + step, :] = src_ref[0, r:r + step, :].astype(BF16)


def _load_once(pairs, sem):
    copies = [pltpu.make_async_copy(src, dst, sem.at[k]) for k, (src, dst) in enumerate(pairs)]
    for cp in copies:
        cp.start()
    for cp in copies:
        cp.wait()


def _const_spec(shape, index=None):
    index = (0,) * len(shape) if index is None else index
    return pl.BlockSpec(shape, lambda *_: index, pipeline_mode=pl.Buffered(1))


def _vmem_limit(args, in_specs, out_spec, out_dtype, scratch):
    nbytes = lambda shape, dtype: math.prod(shape) * jnp.dtype(dtype).itemsize
    total = 2 * nbytes(out_spec.block_shape, out_dtype) + SPILL_MARGIN
    for arg, spec in zip(args, in_specs):
        if spec.block_shape is not None:
            total += (1 if spec.pipeline_mode is not None else 2) * nbytes(spec.block_shape, arg.dtype)
    return total + sum(nbytes(s.shape, s.dtype) for s in scratch)


def _ffn_kernel(*refs, layer, with_ple, with_final):
    h_ref, nw_ref, wgu_hbm, wd_hbm = refs[:4]
    k = 4
    if with_ple:
        p_ref, pn_ref, wgate_ref, wup_ref = refs[k:k + 4]
        k += 4
    if with_final:
        fn_ref = refs[k]
        k += 1
    o_ref, xn_ref, r_ref, a_ref, wg_scr, wu_scr, wd_scr, stg_g, stg_u, stg_d = refs[k:k + 10]
    k += 10
    sem = refs[-1]
    if with_ple:
        wgate_scr, wup_scr = refs[k:k + 2]
    s = pl.program_id(0)

    def normalize():
        x = h_ref[...]
        xn_ref[...] = (x * nw_ref[0]).astype(BF16)
        r_ref[...] = jnp.broadcast_to(_rms_scale(x), r_ref.shape)

    def up_chunk(c):
        sl = slice(c * FFN_FC, (c + 1) * FFN_FC)
        g = r_ref[...] * _dot(xn_ref[...], wg_scr[c])
        u = r_ref[...] * _dot(xn_ref[...], wu_scr[c])
        a_ref[:, sl] = (_silu(g) * u).astype(BF16)

    def finish():
        h1 = h_ref[...] + 0.5 * _dot(a_ref[...], wd_scr[...])
        if with_ple:
            xn_ref[...] = (h1 * pn_ref[0]).astype(BF16)
            gate = _sigmoid(_rms_scale(h1) * _dot(xn_ref[...], wgate_scr[...]))
            up = _dot(p_ref[0].astype(BF16), wup_scr[...])
            h1 = h1 + gate * up
        if with_final:
            h1 = _rmsnorm(h1, fn_ref[...])
        o_ref[...] = h1

    def stage_copies(c):
        slot = c % 2
        cols = pl.ds(c * FFN_FC, FFN_FC)
        ucols = pl.ds(D_FF + c * FFN_FC, FFN_FC)
        return (pltpu.make_async_copy(wgu_hbm.at[layer, pl.ds(0, D_MODEL), cols], stg_g.at[slot],
                                      sem.at[0, slot]),
                pltpu.make_async_copy(wgu_hbm.at[layer, pl.ds(0, D_MODEL), ucols], stg_u.at[slot],
                                      sem.at[1, slot]),
                pltpu.make_async_copy(wd_hbm.at[layer, cols, pl.ds(0, D_MODEL)], stg_d.at[slot],
                                      sem.at[2, slot]))

    @pl.when(s == 0)
    def _():
        for cp in stage_copies(0):
            cp.start()
        normalize()
        if with_ple:
            _cast_rows(wgate_scr, wgate_ref)
            _cast_rows(wup_scr, wup_ref)
        for c in range(FFN_NCH):
            if c + 1 < FFN_NCH:
                for cp in stage_copies(c + 1):
                    cp.start()
            for cp in stage_copies(c):
                cp.wait()
            slot = c % 2
            wg_scr[c] = stg_g[slot].astype(BF16)
            wu_scr[c] = stg_u[slot].astype(BF16)
            wd_scr[c * FFN_FC:(c + 1) * FFN_FC, :] = stg_d[slot].astype(BF16)
            up_chunk(c)
        finish()

    @pl.when(s > 0)
    def _():
        normalize()
        for c in range(FFN_NCH):
            up_chunk(c)
        finish()


def _ffn_tile(layer, with_ple):
    return FFN_TM_FIRST if (layer == 0 and not with_ple) else FFN_TM


def _ffn_call(h, layer, norm_w, w_gu, w_d, ple=None, final_norm=None):
    t = h.shape[0]
    tm = _ffn_tile(layer, ple is not None)
    assert t % tm == 0 and D_FF % FFN_FC == 0
    row_spec = pl.BlockSpec((tm, D_MODEL), lambda s: (s, 0))
    in_specs = [row_spec, _const_spec((1, 1, D_MODEL), (layer, 0, 0)),
                pl.BlockSpec(memory_space=pl.ANY), pl.BlockSpec(memory_space=pl.ANY)]
    args = [h, norm_w, w_gu, w_d]
    scratch = [pltpu.VMEM((tm, D_MODEL), BF16),
               pltpu.VMEM((tm, FFN_FC), F32),
               pltpu.VMEM((tm, D_FF), BF16),
               pltpu.VMEM((FFN_NCH, D_MODEL, FFN_FC), BF16),
               pltpu.VMEM((FFN_NCH, D_MODEL, FFN_FC), BF16),
               pltpu.VMEM((D_FF, D_MODEL), BF16),
               pltpu.VMEM((2, D_MODEL, FFN_FC), F32),
               pltpu.VMEM((2, D_MODEL, FFN_FC), F32),
               pltpu.VMEM((2, FFN_FC, D_MODEL), F32)]
    if ple is not None:
        p, pn, wgate, wup = ple
        in_specs += [pl.BlockSpec((1, tm, PLE_DIM), lambda s: (layer, s, 0)),
                     _const_spec((1, 1, D_MODEL), (layer, 0, 0)),
                     _const_spec((1, D_MODEL, D_MODEL), (layer, 0, 0)),
                     _const_spec((1, PLE_DIM, D_MODEL), (layer, 0, 0))]
        args += [p, pn, wgate, wup]
        scratch += [pltpu.VMEM((D_MODEL, D_MODEL), BF16), pltpu.VMEM((PLE_DIM, D_MODEL), BF16)]
    if final_norm is not None:
        in_specs.append(_const_spec((1, D_MODEL)))
        args.append(final_norm)
    return pl.pallas_call(
        functools.partial(_ffn_kernel, layer=layer, with_ple=ple is not None,
                          with_final=final_norm is not None),
        grid=(t // tm,),
        in_specs=in_specs,
        out_specs=row_spec,
        out_shape=jax.ShapeDtypeStruct((t, D_MODEL), F32),
        scratch_shapes=scratch + [pltpu.SemaphoreType.DMA((3, 2))],
        compiler_params=pltpu.CompilerParams(
            dimension_semantics=("arbitrary",),
            vmem_limit_bytes=_vmem_limit(args, in_specs, row_spec, F32, scratch)),
        name="ffn_ple" if ple is not None else "ffn",
    )(*args)


def _mix_even_kernel(h_ref, nw_hbm, win_hbm, lng_hbm, lnb_hbm, ws_hbm, bs_hbm, scw_hbm, wout_hbm,
                     o_ref, nw_ref, win_ref, lng_ref, lnb_ref, ws_ref, bs_ref, scw_ref, wout_ref,
                     win_scr, wout_scr, wt_scr, hn_ref, u_ref, bg_ref, vn_ref, q_ref, y_ref, sem,
                     *, layer, grp):
    tm = MIX_TM
    i = pl.program_id(0)
    j = pl.program_id(1)

    @pl.when((i == 0) & (j == 0))
    def _():
        _load_once([(nw_hbm.at[pl.ds(layer, 1)], nw_ref), (win_hbm.at[pl.ds(grp, 1)], win_ref),
                    (lng_hbm.at[pl.ds(grp, 1)], lng_ref), (lnb_hbm.at[pl.ds(grp, 1)], lnb_ref),
                    (ws_hbm.at[pl.ds(grp, 1)], ws_ref), (bs_hbm, bs_ref),
                    (scw_hbm.at[pl.ds(grp, 1)], scw_ref), (wout_hbm.at[pl.ds(grp, 1)], wout_ref)], sem)
        _cast_rows(win_scr, win_ref)
        _cast_rows(wout_scr, wout_ref)
        row = lax.broadcasted_iota(jnp.int32, (CHUNK, CHUNK), 0)
        col = lax.broadcasted_iota(jnp.int32, (CHUNK, CHUNK), 1)
        for g in range(N_GROUPS):
            wt_scr[g] = jnp.where(col <= row, ws_ref[0, g], 0.0).astype(BF16)

    @pl.when(j == 0)
    def _():
        q_ref[0:SC_HALO, :] = jnp.zeros((SC_HALO, HALF), F32)

    x = h_ref[0]
    hn_ref[...] = _rmsnorm(x, nw_ref[0]).astype(BF16)

    v = _gelu_tanh(_dot(hn_ref[...], win_scr[:, HALF:2 * HALF]))
    for g in range(N_GROUPS):
        gsl = slice(g * GROUP_W, (g + 1) * GROUP_W)
        vn_ref[:, gsl] = _layernorm(v[:, gsl], lng_ref[0][:, gsl], lnb_ref[0][:, gsl]).astype(BF16)
    u_ref[...] = _dot(hn_ref[...], win_scr[:, 0:HALF])
    q_ref[SC_HALO:SC_HALO + tm, :] = (_dot(hn_ref[...], win_scr[:, 3 * HALF:4 * HALF])
                                      * _dot(hn_ref[...], win_scr[:, 4 * HALF:5 * HALF]))
    bg_ref[...] = _dot(hn_ref[...], win_scr[:, 2 * HALF:3 * HALF])
    for g in range(N_GROUPS):
        gsl = slice(g * GROUP_W, (g + 1) * GROUP_W)
        for c in range(tm // CHUNK):
            rsl = slice(c * CHUNK, (c + 1) * CHUNK)
            mixed = _dot(wt_scr[g], vn_ref[rsl, gsl]) + bs_ref[:, gsl]
            y_ref[rsl, gsl] = (_gelu_tanh(u_ref[rsl, gsl]) * mixed).astype(BF16)
    o_ref[0] = x + _dot(y_ref[:, 0:HALF], wout_scr[0:HALF, :])

    conv = scw_ref[0, 2:3, :] * q_ref[SC_HALO:SC_HALO + tm, :]
    conv = conv + scw_ref[0, 1:2, :] * q_ref[SC_HALO - 1:SC_HALO - 1 + tm, :]
    conv = conv + scw_ref[0, 0:1, :] * q_ref[SC_HALO - 2:SC_HALO - 2 + tm, :]
    y_ref[:, HALF:2 * HALF] = (bg_ref[...] * conv).astype(BF16)
    q_ref[0:SC_HALO, :] = q_ref[tm:tm + SC_HALO, :]

    o_ref[0] = o_ref[0] + _dot(y_ref[:, HALF:2 * HALF], wout_scr[HALF:2 * HALF, :])


def _mix_even_call(h3, layer, j, norm_w, w_in, ln_g, ln_b, w_s, b_full, sc_w, w_out):
    b, s, _ = h3.shape
    tm = MIX_TM
    assert s % tm == 0 and tm % CHUNK == 0 and SC_HALO >= SC_WIDTH - 1
    row_spec = pl.BlockSpec((1, tm, D_MODEL), lambda i, t: (i, t, 0))
    args = [h3, norm_w, w_in, ln_g, ln_b, w_s, b_full, sc_w, w_out]
    in_specs = [row_spec] + [pl.BlockSpec(memory_space=pl.ANY)] * (len(args) - 1)
    scratch = [pltpu.VMEM((1, 1, D_MODEL), F32),
               pltpu.VMEM((1, D_MODEL, 5 * HALF), F32),
               pltpu.VMEM((1, 1, HALF), F32),
               pltpu.VMEM((1, 1, HALF), F32),
               pltpu.VMEM((1, N_GROUPS, CHUNK, CHUNK), F32),
               pltpu.VMEM((CHUNK, HALF), F32),
               pltpu.VMEM((1, SC_WIDTH, HALF), F32),
               pltpu.VMEM((1, D_MODEL, D_MODEL), F32),
               pltpu.VMEM((D_MODEL, 5 * HALF), BF16),
               pltpu.VMEM((D_MODEL, D_MODEL), BF16),
               pltpu.VMEM((N_GROUPS, CHUNK, CHUNK), BF16),
               pltpu.VMEM((tm, D_MODEL), BF16),
               pltpu.VMEM((tm, HALF), F32),
               pltpu.VMEM((tm, HALF), F32),
               pltpu.VMEM((tm, HALF), BF16),
               pltpu.VMEM((SC_HALO + tm, HALF), F32),
               pltpu.VMEM((tm, D_MODEL), BF16)]
    return pl.pallas_call(
        functools.partial(_mix_even_kernel, layer=layer, grp=j),
        grid=(b, s // tm),
        in_specs=in_specs,
        out_specs=row_spec,
        out_shape=jax.ShapeDtypeStruct(h3.shape, F32),
        scratch_shapes=scratch + [pltpu.SemaphoreType.DMA((len(args) - 1,))],
        compiler_params=pltpu.CompilerParams(
            dimension_semantics=("arbitrary", "arbitrary"),
            vmem_limit_bytes=_vmem_limit(args, in_specs, row_spec, F32, scratch)),
        name="mix_even",
    )(*args)


class _RowView:
    def __init__(self, ref, base):
        self.ref, self.base = ref, base

    def _shift(self, idx):
        rows, lanes = idx
        if isinstance(rows, slice):
            rows = slice(rows.start + self.base, rows.stop + self.base)
        else:
            rows = pl.ds(rows.start + self.base, rows.size)
        return rows, lanes

    def __getitem__(self, idx):
        return self.ref[self._shift(idx)]

    def __setitem__(self, idx, val):
        self.ref[self._shift(idx)] = val


def _mix_odd_kernel(zs_ref, h_ref, nw_ref, win_ref, pw_ref, ps_ref, cvw_ref, cvb_ref, lng_ref,
                    lnb_ref, wout_ref, o_ref, win_scr, wout_scr, pw_scr, hn_ref, p_ref, sa_ref,
                    sb_ref, raw_ref, *rest):
    c_refs, y_ref = rest[:-1], rest[-1]
    n_lane = HALF // GLU_FC

    unit_rows = HALO + 2 * (MIX_TM_ODD // ODD_PARTS)

    def c_ref_of(part, c):
        u = part * n_lane + c
        return _RowView(c_refs[(u + 1) // 2], unit_rows * ((u + 1) % 2) if u > 0 else 0)

    tm = MIX_TM_ODD
    n = HALO + tm
    rp = tm // ODD_PARTS
    npart = HALO + rp
    i = pl.program_id(0)
    j = pl.program_id(1)

    @pl.when((i == 0) & (j == 0))
    def _():
        _cast_rows(win_scr, win_ref)
        _cast_rows(wout_scr, wout_ref)
        for g in range(N_GROUPS):
            pw_scr[g] = pw_ref[0, g].astype(BF16)

    @pl.when(j == 0)
    def _():
        p_ref[0:HALO, :] = jnp.zeros((HALO, HALF), F32)
        for c in range(n_lane):
            c_ref_of(0, c)[0:HALO, :] = jnp.zeros((HALO, GLU_FC), F32)

    zero = zs_ref[0]
    n_units = ODD_PARTS * n_lane
    anchor_count = {}

    def anchor(acc, u):
        if u < n_units:
            ref = c_refs[(u + 1) // 2]
            slot = anchor_count.get(u, 0)
            anchor_count[u] = slot + 1
            spare = ref.shape[0] - ANCHOR_SLOTS * SUBLANES + slot * SUBLANES
            ref[spare:spare + SUBLANES, :] = acc[acc.shape[0] - SUBLANES:, acc.shape[1] - GLU_FC:]

    def front(part):
        rows = slice(part * rp, (part + 1) * rp)
        for c in range(n_lane):
            a = _dot(hn_ref[rows, :], win_scr[:, HALF + c * GLU_FC:HALF + (c + 1) * GLU_FC])
            g = _dot(hn_ref[rows, :], win_scr[:, 2 * HALF + c * GLU_FC:2 * HALF + (c + 1) * GLU_FC])
            if part == 0:
                c_ref_of(part, c)[HALO:npart, :] = a * _sigmoid(g)
            else:
                raw_ref[c, :, 0:GLU_FC] = a
                raw_ref[c, :, GLU_FC:2 * GLU_FC] = g
        pin = _dot(hn_ref[rows, :], win_scr[:, 0:HALF])
        p_ref[HALO + part * rp:HALO + (part + 1) * rp, :] = pin

    def glu(part):
        for c in range(n_lane):
            if part > 0:
                c_ref_of(part, c)[0:HALO, :] = c_ref_of(part - 1, c)[rp:npart, :]
                c_ref_of(part, c)[HALO:npart, :] = (raw_ref[c, :, 0:GLU_FC]
                                                    * _sigmoid(raw_ref[c, :, GLU_FC:2 * GLU_FC]))

    def pool(part):
        rows = slice(part * rp, (part + 1) * rp)
        b0 = part * rp
        e = b0 + npart
        g1, g2, g3 = GROUP_W, 2 * GROUP_W, 3 * GROUP_W
        sa_ref[b0 + 8:e, :] = p_ref[b0 + 8:e, :] + p_ref[b0 + 7:e - 1, :]
        sb_ref[b0 + 16:e, g1:] = sa_ref[b0 + 16:e, g1:] + sa_ref[b0 + 14:e - 2, g1:]
        sa_ref[b0 + 24:e, g2:] = sb_ref[b0 + 24:e, g2:] + sb_ref[b0 + 20:e - 4, g2:]
        sb_ref[b0 + 32:e, g3:] = sa_ref[b0 + 32:e, g3:] + sa_ref[b0 + 24:e - 8, g3:]
        head = slice(HALO, HALO + max(POOL_WINDOWS))
        pos = lax.broadcasted_iota(jnp.int32, (max(POOL_WINDOWS), GROUP_W), 0)
        for gi, win in enumerate(POOL_WINDOWS):
            gsl = slice(gi * GROUP_W, (gi + 1) * GROUP_W)
            s_ref = sa_ref if gi % 2 == 0 else sb_ref
            if part == 0:
                count = jnp.minimum(pos + 1, win).astype(F32)
                s_ref[head, gsl] = s_ref[head, gsl] * jnp.where(j == 0, float(win) / count, 1.0)
            pooled = s_ref[b0 + HALO:e, gsl] * (1.0 / win) - p_ref[b0 + HALO:e, gsl]
            yc = _dot(pooled.astype(BF16), pw_scr[gi]) * ps_ref[0][:, gsl]
            y_ref[rows, gsl] = yc.astype(BF16)
        acc = _dot(y_ref[rows, 0:HALF], wout_scr[0:HALF, :])
        o_ref[0, rows, :] = h_ref[0, rows, :] + acc
        anchor(acc, (part + 1) * n_lane + n_lane - 1)

    def conv(part):
        for c in range(n_lane):
            c_ref = c_ref_of(part, c)
            for r in range(rp // CV_ROWS):
                r0 = pl.multiple_of(zero + r * CV_ROWS, CV_ROWS)
                for lt in range(GLU_FC // LANES):
                    lsl = slice(lt * LANES, (lt + 1) * LANES)
                    wsl = slice(c * GLU_FC + lt * LANES, c * GLU_FC + (lt + 1) * LANES)
                    chunk = c_ref[pl.ds(r0, HALO + CV_ROWS), lsl]
                    acc = None
                    for b in range(SUBLANES):
                        rolled = chunk if b == 0 else pltpu.roll(chunk, b, axis=0)
                        for a in range(HALO // SUBLANES):
                            d = SUBLANES * a + b
                            if d >= CV_WIDTH:
                                continue
                            lo = HALO - SUBLANES * a
                            src = rolled[lo:lo + CV_ROWS].reshape(CV_ROWS // SUBLANES, SUBLANES, LANES)
                            term = cvw_ref[CV_WIDTH - 1 - d, :, wsl][None] * src
                            acc = term if acc is None else acc + term
                    acc = acc + cvb_ref[0][:, wsl][None]
                    c_ref[pl.ds(r0 + npart, CV_ROWS), lsl] = acc.reshape(CV_ROWS, LANES)

    def back(part):
        rows = slice(part * rp, (part + 1) * rp)
        hc = jnp.concatenate([c_ref_of(part, c)[npart:npart + rp, :] for c in range(n_lane)], axis=-1)
        ln = _layernorm(hc, lng_ref[0], lnb_ref[0])
        y_ref[rows, HALF:2 * HALF] = _silu(ln).astype(BF16)
        acc = _dot(y_ref[rows, HALF:2 * HALF], wout_scr[HALF:2 * HALF, :])
        o_ref[0, rows, :] = o_ref[0, rows, :] + acc
        anchor(acc, (part + 1) * n_lane + n_lane - 1)

    hn_ref[...] = _rmsnorm(h_ref[0], nw_ref[0]).astype(BF16)
    front(0)
    anchor(hn_ref[tm - 2 * SUBLANES:tm, :].astype(F32), 0)
    for part in range(ODD_PARTS):
        if part + 1 < ODD_PARTS:
            front(part + 1)
        pool(part)
        glu(part)
        conv(part)
        back(part)
    p_ref[0:HALO, :] = p_ref[tm:n, :]
    for c in range(n_lane):
        c_ref_of(0, c)[0:HALO, :] = c_ref_of(ODD_PARTS - 1, c)[rp:npart, :]


def _mix_odd_call(h3, layer, j, norm_w, w_in, pool_w, pool_scale, cv_w8, cv_b, ln_g, ln_b, w_out):
    b, s, _ = h3.shape
    tm = MIX_TM_ODD
    tiles = s // tm
    assert s % tm == 0 and (tm // ODD_PARTS) % CV_ROWS == 0 and HALO >= CV_WIDTH - 1 and ODD_PARTS == 2
    assert HALO >= 2 * max(POOL_WINDOWS) - 1 and tm >= max(POOL_WINDOWS)
    row_spec = pl.BlockSpec((1, tm, D_MODEL), lambda i, t: (i, t, 0))
    vec = _const_spec((1, 1, HALF), (j, 0, 0))
    in_specs = [pl.BlockSpec(memory_space=pltpu.SMEM), row_spec,
                _const_spec((1, 1, D_MODEL), (layer, 0, 0)),
                _const_spec((1, D_MODEL, 3 * HALF), (j, 0, 0)),
                _const_spec((1, N_GROUPS, GROUP_W, GROUP_W), (j, 0, 0, 0)), vec,
                _const_spec((CV_WIDTH, SUBLANES, HALF)), vec, vec, vec,
                _const_spec((1, D_MODEL, D_MODEL), (j, 0, 0))]
    args = [jnp.zeros((1,), jnp.int32), h3, norm_w, w_in, pool_w, pool_scale, cv_w8, cv_b, ln_g, ln_b,
            w_out]
    scratch = [pltpu.VMEM((D_MODEL, 3 * HALF), BF16),
               pltpu.VMEM((D_MODEL, D_MODEL), BF16),
               pltpu.VMEM((N_GROUPS, GROUP_W, GROUP_W), BF16),
               pltpu.VMEM((tm, D_MODEL), BF16),
               pltpu.VMEM((HALO + tm, HALF), F32),
               pltpu.VMEM((HALO + tm, HALF), F32),
               pltpu.VMEM((HALO + tm, HALF), F32),
               pltpu.VMEM((HALF // GLU_FC, tm // ODD_PARTS, 2 * GLU_FC), F32),
               pltpu.VMEM((HALO + 2 * tm // ODD_PARTS + ANCHOR_SLOTS * SUBLANES, GLU_FC), F32),
               *[pltpu.VMEM((2 * (HALO + 2 * tm // ODD_PARTS) + ANCHOR_SLOTS * SUBLANES, GLU_FC), F32)
                 for _ in range(ODD_PARTS * (HALF // GLU_FC) // 2 - 1)],
               pltpu.VMEM((HALO + 2 * tm // ODD_PARTS + ANCHOR_SLOTS * SUBLANES, GLU_FC), F32),
               pltpu.VMEM((tm, D_MODEL), BF16)]
    return pl.pallas_call(
        _mix_odd_kernel,
        grid=(b, tiles),
        in_specs=in_specs,
        out_specs=row_spec,
        out_shape=jax.ShapeDtypeStruct(h3.shape, F32),
        scratch_shapes=scratch,
        compiler_params=pltpu.CompilerParams(
            dimension_semantics=("arbitrary", "arbitrary"),
            vmem_limit_bytes=_vmem_limit(args, in_specs, row_spec, F32, scratch)),
        name="mix_odd",
    )(*args)


def kernel(x, p, ffn1_norm, ffn1_w_gu, ffn1_w_down, mix_norm, ffn2_norm, ffn2_w_gu, ffn2_w_down,
           ple_norm, ple_w_gate, ple_w_up, ab_w_in, gm_ln_g, gm_ln_b, gm_w_s, gm_b_s, sc_w,
           ab_w_out, cd_w_in, pool_w, pool_scale, cv_w, cv_b, cv_ln_g, cv_ln_b, cd_w_out,
           final_norm):
    bsz, seq, d = x.shape
    depth = p.shape[0]
    t = bsz * seq
    vec3 = lambda v: v.reshape(v.shape[0], 1, v.shape[1])
    p3 = p.reshape(depth, t, PLE_DIM)
    h = x.reshape(t, d)
    for i in range(depth):
        j = i // 2
        h = _ffn_call(h, i, vec3(ffn1_norm), ffn1_w_gu, ffn1_w_down)
        h3 = h.reshape(bsz, seq, d)
        if i % 2 == 0:
            b_full = jnp.repeat(gm_b_s[j].T, GROUP_W, axis=1)
            h3 = _mix_even_call(h3, i, j, vec3(mix_norm), ab_w_in, vec3(gm_ln_g), vec3(gm_ln_b),
                                gm_w_s, b_full, sc_w, ab_w_out)
        else:
            cv_w8 = jnp.broadcast_to(cv_w[j][:, None, :], (CV_WIDTH, SUBLANES, HALF))
            h3 = _mix_odd_call(h3, i, j, vec3(mix_norm), cd_w_in, pool_w, vec3(pool_scale), cv_w8,
                               vec3(cv_b), vec3(cv_ln_g), vec3(cv_ln_b), cd_w_out)
        h = h3.reshape(t, d)
        h = _ffn_call(h, i, vec3(ffn2_norm), ffn2_w_gu, ffn2_w_down,
                      ple=(p3, vec3(ple_norm), ple_w_gate, ple_w_up),
                      final_norm=final_norm.reshape(1, d) if i == depth - 1 else None)
    return h.reshape(bsz, seq, d)
```

```python
import functools
import math

import jax
import jax.numpy as jnp
from jax import lax
from jax.experimental import pallas as pl
from jax.experimental.pallas import tpu as pltpu

F32 = jnp.float32
BF16 = jnp.bfloat16

D_MODEL = 1024
PLE_DIM = 256
D_FF = 2816
HALF = 512
LANES = 128
SUBLANES = 8
GROUP_W = 128
N_GROUPS = 4
CHUNK = 128
POOL_WINDOWS = (2, 4, 8, 16)
SC_WIDTH = 3
CV_WIDTH = 31
RMS_EPS = 1e-6
LN_EPS = 1e-5

FFN_TM = 512
FFN_TM_FIRST = 1024
FFN_FC = 256
FFN_NCH = D_FF // FFN_FC
MIX_TM = 1024
MIX_TM_ODD = 1024
ANCHOR_SLOTS = 4
ODD_PARTS = 2
GLU_FC = 256
CAST_ROWS = 256
SC_HALO = 8
HALO = 32
CV_ROWS = 128
SPILL_MARGIN = 6 * 1024 * 1024


def _rms_scale(x):
    return lax.rsqrt(jnp.mean(x * x, axis=-1, keepdims=True) + RMS_EPS)


def _rmsnorm(x, g):
    return x * _rms_scale(x) * g


def _layernorm(x, g, b):
    mu = jnp.mean(x, axis=-1, keepdims=True)
    xc = x - mu
    var = jnp.mean(xc * xc, axis=-1, keepdims=True)
    return xc * lax.rsqrt(var + LN_EPS) * g + b


def _sigmoid(x):
    return 0.5 + 0.5 * jnp.tanh(0.5 * x)


def _silu(x):
    hx = 0.5 * x
    return hx + hx * jnp.tanh(hx)


def _gelu_tanh(x):
    c = 0.7978845608028654
    return 0.5 * x * (1.0 + jnp.tanh(c * (x + 0.044715 * (x * x * x))))


def _dot(a, b):
    return jnp.dot(a, b, preferred_element_type=F32)


def _cast_rows(dst_ref, src_ref):
    rows = src_ref.shape[1]
    step = min(rows, CAST_ROWS)
    for r in range(0, rows, step):
        dst_ref[r:r + step, :] = src_ref[0, r:r + step, :].astype(BF16)


def _const_spec(shape, index=None):
    index = (0,) * len(shape) if index is None else index
    return pl.BlockSpec(shape, lambda *_: index, pipeline_mode=pl.Buffered(1))


def _vmem_limit(args, in_specs, out_spec, out_dtype, scratch):
    nbytes = lambda shape, dtype: math.prod(shape) * jnp.dtype(dtype).itemsize
    total = 2 * nbytes(out_spec.block_shape, out_dtype) + SPILL_MARGIN
    for arg, spec in zip(args, in_specs):
        if spec.block_shape is not None:
            total += (1 if spec.pipeline_mode is not None else 2) * nbytes(spec.block_shape, arg.dtype)
    return total + sum(nbytes(s.shape, s.dtype) for s in scratch)


def _ffn_kernel(*refs, layer, with_ple, with_final):
    h_ref, nw_ref, wgu_hbm, wd_hbm = refs[:4]
    k = 4
    if with_ple:
        p_ref, pn_ref, wgate_hbm, wup_hbm = refs[k:k + 4]
        k += 4
    if with_final:
        fn_ref = refs[k]
        k += 1
    o_ref, xn_ref, r_ref, a_ref, wg_scr, wu_scr, wd_scr, stg_g, stg_u, stg_d = refs[k:k + 10]
    k += 10
    sem = refs[-1]
    if with_ple:
        wgate_scr, wup_scr, wgate_f32, wup_f32 = refs[k:k + 4]
    s = pl.program_id(0)

    def normalize():
        x = h_ref[...]
        xn_ref[...] = (x * nw_ref[0]).astype(BF16)
        r_ref[...] = jnp.broadcast_to(_rms_scale(x), r_ref.shape)

    def up_chunk(c):
        sl = slice(c * FFN_FC, (c + 1) * FFN_FC)
        g = r_ref[...] * _dot(xn_ref[...], wg_scr[c])
        u = r_ref[...] * _dot(xn_ref[...], wu_scr[c])
        a_ref[:, sl] = (_silu(g) * u).astype(BF16)

    def finish():
        h1 = h_ref[...] + 0.5 * _dot(a_ref[...], wd_scr[...])
        if with_ple:
            xn_ref[...] = (h1 * pn_ref[0]).astype(BF16)
            gate = _sigmoid(_rms_scale(h1) * _dot(xn_ref[...], wgate_scr[...]))
            up = _dot(p_ref[0].astype(BF16), wup_scr[...])
            h1 = h1 + gate * up
        if with_final:
            h1 = _rmsnorm(h1, fn_ref[...])
        o_ref[...] = h1

    def stage_copies(c):
        slot = c % 2
        cols = pl.ds(c * FFN_FC, FFN_FC)
        ucols = pl.ds(D_FF + c * FFN_FC, FFN_FC)
        return (pltpu.make_async_copy(wgu_hbm.at[layer, pl.ds(0, D_MODEL), cols], stg_g.at[slot],
                                      sem.at[0, slot]),
                pltpu.make_async_copy(wgu_hbm.at[layer, pl.ds(0, D_MODEL), ucols], stg_u.at[slot],
                                      sem.at[1, slot]),
                pltpu.make_async_copy(wd_hbm.at[layer, cols, pl.ds(0, D_MODEL)], stg_d.at[slot],
                                      sem.at[2, slot]))

    @pl.when(s == 0)
    def _():
        for cp in stage_copies(0):
            cp.start()
        if with_ple:
            ple_copies = (pltpu.make_async_copy(wgate_hbm.at[pl.ds(layer, 1)], wgate_f32, sem.at[3, 0]),
                          pltpu.make_async_copy(wup_hbm.at[pl.ds(layer, 1)], wup_f32, sem.at[3, 1]))
            for cp in ple_copies:
                cp.start()
        normalize()
        for c in range(FFN_NCH):
            if c + 1 < FFN_NCH:
                for cp in stage_copies(c + 1):
                    cp.start()
            for cp in stage_copies(c):
                cp.wait()
            slot = c % 2
            wg_scr[c] = stg_g[slot].astype(BF16)
            wu_scr[c] = stg_u[slot].astype(BF16)
            wd_scr[c * FFN_FC:(c + 1) * FFN_FC, :] = stg_d[slot].astype(BF16)
            up_chunk(c)
        if with_ple:
            for cp in ple_copies:
                cp.wait()
            _cast_rows(wgate_scr, wgate_f32)
            _cast_rows(wup_scr, wup_f32)
        finish()

    @pl.when(s > 0)
    def _():
        normalize()
        for c in range(FFN_NCH):
            up_chunk(c)
        finish()


def _ffn_tile(layer, with_ple):
    return FFN_TM_FIRST if (layer == 0 and not with_ple) else FFN_TM


def _ffn_call(h, layer, norm_w, w_gu, w_d, ple=None, final_norm=None):
    t = h.shape[0]
    tm = _ffn_tile(layer, ple is not None)
    assert t % tm == 0 and D_FF % FFN_FC == 0
    row_spec = pl.BlockSpec((tm, D_MODEL), lambda s: (s, 0))
    in_specs = [row_spec, _const_spec((1, 1, D_MODEL), (layer, 0, 0)),
                pl.BlockSpec(memory_space=pl.ANY), pl.BlockSpec(memory_space=pl.ANY)]
    args = [h, norm_w, w_gu, w_d]
    scratch = [pltpu.VMEM((tm, D_MODEL), BF16),
               pltpu.VMEM((tm, FFN_FC), F32),
               pltpu.VMEM((tm, D_FF), BF16),
               pltpu.VMEM((FFN_NCH, D_MODEL, FFN_FC), BF16),
               pltpu.VMEM((FFN_NCH, D_MODEL, FFN_FC), BF16),
               pltpu.VMEM((D_FF, D_MODEL), BF16),
               pltpu.VMEM((2, D_MODEL, FFN_FC), F32),
               pltpu.VMEM((2, D_MODEL, FFN_FC), F32),
               pltpu.VMEM((2, FFN_FC, D_MODEL), F32)]
    if ple is not None:
        p, pn, wgate, wup = ple
        in_specs += [pl.BlockSpec((1, tm, PLE_DIM), lambda s: (layer, s, 0)),
                     _const_spec((1, 1, D_MODEL), (layer, 0, 0)),
                     pl.BlockSpec(memory_space=pl.ANY), pl.BlockSpec(memory_space=pl.ANY)]
        args += [p, pn, wgate, wup]
        scratch += [pltpu.VMEM((D_MODEL, D_MODEL), BF16), pltpu.VMEM((PLE_DIM, D_MODEL), BF16),
                    pltpu.VMEM((1, D_MODEL, D_MODEL), F32),
                    pltpu.VMEM((1, PLE_DIM, D_MODEL), F32)]
    if final_norm is not None:
        in_specs.append(_const_spec((1, D_MODEL)))
        args.append(final_norm)
    return pl.pallas_call(
        functools.partial(_ffn_kernel, layer=layer, with_ple=ple is not None,
                          with_final=final_norm is not None),
        grid=(t // tm,),
        in_specs=in_specs,
        out_specs=row_spec,
        out_shape=jax.ShapeDtypeStruct((t, D_MODEL), F32),
        scratch_shapes=scratch + [pltpu.SemaphoreType.DMA((4, 2))],
        compiler_params=pltpu.CompilerParams(
            dimension_semantics=("arbitrary",),
            vmem_limit_bytes=_vmem_limit(args, in_specs, row_spec, F32, scratch)),
        name="ffn_ple" if ple is not None else "ffn",
    )(*args)


def _mix_even_kernel(h_ref, nw_ref, win_ref, lng_ref, lnb_ref, ws_ref, bs_ref, scw_ref, wout_ref,
                     o_ref, win_scr, wout_scr, wt_scr, hn_ref, u_ref, bg_ref, vn_ref, q_ref, y_ref):
    tm = MIX_TM
    i = pl.program_id(0)
    j = pl.program_id(1)

    @pl.when((i == 0) & (j == 0))
    def _():
        _cast_rows(win_scr, win_ref)
        _cast_rows(wout_scr, wout_ref)
        row = lax.broadcasted_iota(jnp.int32, (CHUNK, CHUNK), 0)
        col = lax.broadcasted_iota(jnp.int32, (CHUNK, CHUNK), 1)
        for g in range(N_GROUPS):
            wt_scr[g] = jnp.where(col <= row, ws_ref[0, g], 0.0).astype(BF16)

    @pl.when(j == 0)
    def _():
        q_ref[0:SC_HALO, :] = jnp.zeros((SC_HALO, HALF), F32)

    x = h_ref[0]
    hn_ref[...] = _rmsnorm(x, nw_ref[0]).astype(BF16)

    v = _gelu_tanh(_dot(hn_ref[...], win_scr[:, HALF:2 * HALF]))
    for g in range(N_GROUPS):
        gsl = slice(g * GROUP_W, (g + 1) * GROUP_W)
        vn_ref[:, gsl] = _layernorm(v[:, gsl], lng_ref[0][:, gsl], lnb_ref[0][:, gsl]).astype(BF16)
    u_ref[...] = _dot(hn_ref[...], win_scr[:, 0:HALF])
    q_ref[SC_HALO:SC_HALO + tm, :] = (_dot(hn_ref[...], win_scr[:, 3 * HALF:4 * HALF])
                                      * _dot(hn_ref[...], win_scr[:, 4 * HALF:5 * HALF]))
    bg_ref[...] = _dot(hn_ref[...], win_scr[:, 2 * HALF:3 * HALF])
    for g in range(N_GROUPS):
        gsl = slice(g * GROUP_W, (g + 1) * GROUP_W)
        for c in range(tm // CHUNK):
            rsl = slice(c * CHUNK, (c + 1) * CHUNK)
            mixed = _dot(wt_scr[g], vn_ref[rsl, gsl]) + bs_ref[:, gsl]
            y_ref[rsl, gsl] = (_gelu_tanh(u_ref[rsl, gsl]) * mixed).astype(BF16)
    o_ref[0] = x + _dot(y_ref[:, 0:HALF], wout_scr[0:HALF, :])

    conv = scw_ref[0, 2:3, :] * q_ref[SC_HALO:SC_HALO + tm, :]
    conv = conv + scw_ref[0, 1:2, :] * q_ref[SC_HALO - 1:SC_HALO - 1 + tm, :]
    conv = conv + scw_ref[0, 0:1, :] * q_ref[SC_HALO - 2:SC_HALO - 2 + tm, :]
    y_ref[:, HALF:2 * HALF] = (bg_ref[...] * conv).astype(BF16)
    q_ref[0:SC_HALO, :] = q_ref[tm:tm + SC_HALO, :]

    o_ref[0] = o_ref[0] + _dot(y_ref[:, HALF:2 * HALF], wout_scr[HALF:2 * HALF, :])


def _mix_even_call(h3, layer, j, norm_w, w_in, ln_g, ln_b, w_s, b_full, sc_w, w_out):
    b, s, _ = h3.shape
    tm = MIX_TM
    assert s % tm == 0 and tm % CHUNK == 0 and SC_HALO >= SC_WIDTH - 1
    row_spec = pl.BlockSpec((1, tm, D_MODEL), lambda i, t: (i, t, 0))
    in_specs = [row_spec, _const_spec((1, 1, D_MODEL), (layer, 0, 0)),
                _const_spec((1, D_MODEL, 5 * HALF), (j, 0, 0)),
                _const_spec((1, 1, HALF), (j, 0, 0)), _const_spec((1, 1, HALF), (j, 0, 0)),
                _const_spec((1, N_GROUPS, CHUNK, CHUNK), (j, 0, 0, 0)), _const_spec((CHUNK, HALF)),
                _const_spec((1, SC_WIDTH, HALF), (j, 0, 0)),
                _const_spec((1, D_MODEL, D_MODEL), (j, 0, 0))]
    args = [h3, norm_w, w_in, ln_g, ln_b, w_s, b_full, sc_w, w_out]
    scratch = [pltpu.VMEM((D_MODEL, 5 * HALF), BF16),
               pltpu.VMEM((D_MODEL, D_MODEL), BF16),
               pltpu.VMEM((N_GROUPS, CHUNK, CHUNK), BF16),
               pltpu.VMEM((tm, D_MODEL), BF16),
               pltpu.VMEM((tm, HALF), F32),
               pltpu.VMEM((tm, HALF), F32),
               pltpu.VMEM((tm, HALF), BF16),
               pltpu.VMEM((SC_HALO + tm, HALF), F32),
               pltpu.VMEM((tm, D_MODEL), BF16)]
    return pl.pallas_call(
        _mix_even_kernel,
        grid=(b, s // tm),
        in_specs=in_specs,
        out_specs=row_spec,
        out_shape=jax.ShapeDtypeStruct(h3.shape, F32),
        scratch_shapes=scratch,
        compiler_params=pltpu.CompilerParams(
            dimension_semantics=("arbitrary", "arbitrary"),
            vmem_limit_bytes=_vmem_limit(args, in_specs, row_spec, F32, scratch)),
        name="mix_even",
    )(*args)


class _RowView:
    def __init__(self, ref, base):
        self.ref, self.base = ref, base

    def _shift(self, idx):
        rows, lanes = idx
        if isinstance(rows, slice):
            rows = slice(rows.start + self.base, rows.stop + self.base)
        else:
            rows = pl.ds(rows.start + self.base, rows.size)
        return rows, lanes

    def __getitem__(self, idx):
        return self.ref[self._shift(idx)]

    def __setitem__(self, idx, val):
        self.ref[self._shift(idx)] = val


def _mix_odd_kernel(zs_ref, h_ref, nw_ref, win_ref, pw_ref, ps_ref, cvw_ref, cvb_ref, lng_ref,
                    lnb_ref, wout_ref, o_ref, win_scr, wout_scr, pw_scr, hn_ref, p_ref, sa_ref,
                    sb_ref, raw_ref, *rest):
    c_refs, y_ref = rest[:-1], rest[-1]
    n_lane = HALF // GLU_FC

    unit_rows = HALO + 2 * (MIX_TM_ODD // ODD_PARTS)

    def c_ref_of(part, c):
        u = part * n_lane + c
        return _RowView(c_refs[(u + 1) // 2], unit_rows * ((u + 1) % 2) if u > 0 else 0)

    tm = MIX_TM_ODD
    n = HALO + tm
    rp = tm // ODD_PARTS
    npart = HALO + rp
    i = pl.program_id(0)
    j = pl.program_id(1)

    @pl.when((i == 0) & (j == 0))
    def _():
        _cast_rows(win_scr, win_ref)
        _cast_rows(wout_scr, wout_ref)
        for g in range(N_GROUPS):
            pw_scr[g] = pw_ref[0, g].astype(BF16)

    @pl.when(j == 0)
    def _():
        p_ref[0:HALO, :] = jnp.zeros((HALO, HALF), F32)
        for c in range(n_lane):
            c_ref_of(0, c)[0:HALO, :] = jnp.zeros((HALO, GLU_FC), F32)

    zero = zs_ref[0]
    n_units = ODD_PARTS * n_lane
    anchor_count = {}

    def anchor(acc, u):
        if u < n_units:
            ref = c_refs[(u + 1) // 2]
            slot = anchor_count.get(u, 0)
            anchor_count[u] = slot + 1
            spare = ref.shape[0] - ANCHOR_SLOTS * SUBLANES + slot * SUBLANES
            ref[spare:spare + SUBLANES, :] = acc[acc.shape[0] - SUBLANES:, acc.shape[1] - GLU_FC:]

    def front(part):
        rows = slice(part * rp, (part + 1) * rp)
        for c in range(n_lane):
            a = _dot(hn_ref[rows, :], win_scr[:, HALF + c * GLU_FC:HALF + (c + 1) * GLU_FC])
            g = _dot(hn_ref[rows, :], win_scr[:, 2 * HALF + c * GLU_FC:2 * HALF + (c + 1) * GLU_FC])
            if part == 0:
                c_ref_of(part, c)[HALO:npart, :] = a * _sigmoid(g)
            else:
                raw_ref[c, :, 0:GLU_FC] = a
                raw_ref[c, :, GLU_FC:2 * GLU_FC] = g
        pin = _dot(hn_ref[rows, :], win_scr[:, 0:HALF])
        p_ref[HALO + part * rp:HALO + (part + 1) * rp, :] = pin

    def glu(part):
        for c in range(n_lane):
            if part > 0:
                c_ref_of(part, c)[0:HALO, :] = c_ref_of(part - 1, c)[rp:npart, :]
                c_ref_of(part, c)[HALO:npart, :] = (raw_ref[c, :, 0:GLU_FC]
                                                    * _sigmoid(raw_ref[c, :, GLU_FC:2 * GLU_FC]))

    def pool(part):
        rows = slice(part * rp, (part + 1) * rp)
        b0 = part * rp
        e = b0 + npart
        g1, g2, g3 = GROUP_W, 2 * GROUP_W, 3 * GROUP_W
        sa_ref[b0 + 8:e, :] = p_ref[b0 + 8:e, :] + p_ref[b0 + 7:e - 1, :]
        sb_ref[b0 + 16:e, g1:] = sa_ref[b0 + 16:e, g1:] + sa_ref[b0 + 14:e - 2, g1:]
        sa_ref[b0 + 24:e, g2:] = sb_ref[b0 + 24:e, g2:] + sb_ref[b0 + 20:e - 4, g2:]
        sb_ref[b0 + 32:e, g3:] = sa_ref[b0 + 32:e, g3:] + sa_ref[b0 + 24:e - 8, g3:]
        head = slice(HALO, HALO + max(POOL_WINDOWS))
        pos = lax.broadcasted_iota(jnp.int32, (max(POOL_WINDOWS), GROUP_W), 0)
        for gi, win in enumerate(POOL_WINDOWS):
            gsl = slice(gi * GROUP_W, (gi + 1) * GROUP_W)
            s_ref = sa_ref if gi % 2 == 0 else sb_ref
            if part == 0:
                count = jnp.minimum(pos + 1, win).astype(F32)
                s_ref[head, gsl] = s_ref[head, gsl] * jnp.where(j == 0, float(win) / count, 1.0)
            pooled = s_ref[b0 + HALO:e, gsl] * (1.0 / win) - p_ref[b0 + HALO:e, gsl]
            yc = _dot(pooled.astype(BF16), pw_scr[gi]) * ps_ref[0][:, gsl]
            y_ref[rows, gsl] = yc.astype(BF16)
        acc = _dot(y_ref[rows, 0:HALF], wout_scr[0:HALF, :])
        o_ref[0, rows, :] = h_ref[0, rows, :] + acc
        anchor(acc, (part + 1) * n_lane + n_lane - 1)

    def conv(part):
        for c in range(n_lane):
            c_ref = c_ref_of(part, c)
            for r in range(rp // CV_ROWS):
                r0 = pl.multiple_of(zero + r * CV_ROWS, CV_ROWS)
                for lt in range(GLU_FC // LANES):
                    lsl = slice(lt * LANES, (lt + 1) * LANES)
                    wsl = slice(c * GLU_FC + lt * LANES, c * GLU_FC + (lt + 1) * LANES)
                    chunk = c_ref[pl.ds(r0, HALO + CV_ROWS), lsl]
                    acc = None
                    for b in range(SUBLANES):
                        rolled = chunk if b == 0 else pltpu.roll(chunk, b, axis=0)
                        for a in range(HALO // SUBLANES):
                            d = SUBLANES * a + b
                            if d >= CV_WIDTH:
                                continue
                            lo = HALO - SUBLANES * a
                            src = rolled[lo:lo + CV_ROWS].reshape(CV_ROWS // SUBLANES, SUBLANES, LANES)
                            term = cvw_ref[CV_WIDTH - 1 - d, :, wsl][None] * src
                            acc = term if acc is None else acc + term
                    acc = acc + cvb_ref[0][:, wsl][None]
                    c_ref[pl.ds(r0 + npart, CV_ROWS), lsl] = acc.reshape(CV_ROWS, LANES)

    def back(part):
        rows = slice(part * rp, (part + 1) * rp)
        hc = jnp.concatenate([c_ref_of(part, c)[npart:npart + rp, :] for c in range(n_lane)], axis=-1)
        ln = _layernorm(hc, lng_ref[0], lnb_ref[0])
        y_ref[rows, HALF:2 * HALF] = _silu(ln).astype(BF16)
        acc = _dot(y_ref[rows, HALF:2 * HALF], wout_scr[HALF:2 * HALF, :])
        o_ref[0, rows, :] = o_ref[0, rows, :] + acc
        anchor(acc, (part + 1) * n_lane + n_lane - 1)

    hn_ref[...] = _rmsnorm(h_ref[0], nw_ref[0]).astype(BF16)
    front(0)
    anchor(hn_ref[tm - 2 * SUBLANES:tm, :].astype(F32), 0)
    for part in range(ODD_PARTS):
        if part + 1 < ODD_PARTS:
            front(part + 1)
        pool(part)
        glu(part)
        conv(part)
        back(part)
    p_ref[0:HALO, :] = p_ref[tm:n, :]
    for c in range(n_lane):
        c_ref_of(0, c)[0:HALO, :] = c_ref_of(ODD_PARTS - 1, c)[rp:npart, :]


def _mix_odd_call(h3, layer, j, norm_w, w_in, pool_w, pool_scale, cv_w8, cv_b, ln_g, ln_b, w_out):
    b, s, _ = h3.shape
    tm = MIX_TM_ODD
    tiles = s // tm
    assert s % tm == 0 and (tm // ODD_PARTS) % CV_ROWS == 0 and HALO >= CV_WIDTH - 1 and ODD_PARTS == 2
    assert HALO >= 2 * max(POOL_WINDOWS) - 1 and tm >= max(POOL_WINDOWS)
    row_spec = pl.BlockSpec((1, tm, D_MODEL), lambda i, t: (i, t, 0))
    vec = _const_spec((1, 1, HALF), (j, 0, 0))
    in_specs = [pl.BlockSpec(memory_space=pltpu.SMEM), row_spec,
                _const_spec((1, 1, D_MODEL), (layer, 0, 0)),
                _const_spec((1, D_MODEL, 3 * HALF), (j, 0, 0)),
                _const_spec((1, N_GROUPS, GROUP_W, GROUP_W), (j, 0, 0, 0)), vec,
                _const_spec((CV_WIDTH, SUBLANES, HALF)), vec, vec, vec,
                _const_spec((1, D_MODEL, D_MODEL), (j, 0, 0))]
    args = [jnp.zeros((1,), jnp.int32), h3, norm_w, w_in, pool_w, pool_scale, cv_w8, cv_b, ln_g, ln_b,
            w_out]
    scratch = [pltpu.VMEM((D_MODEL, 3 * HALF), BF16),
               pltpu.VMEM((D_MODEL, D_MODEL), BF16),
               pltpu.VMEM((N_GROUPS, GROUP_W, GROUP_W), BF16),
               pltpu.VMEM((tm, D_MODEL), BF16),
               pltpu.VMEM((HALO + tm, HALF), F32),
               pltpu.VMEM((HALO + tm, HALF), F32),
               pltpu.VMEM((HALO + tm, HALF), F32),
               pltpu.VMEM((HALF // GLU_FC, tm // ODD_PARTS, 2 * GLU_FC), F32),
               pltpu.VMEM((HALO + 2 * tm // ODD_PARTS + ANCHOR_SLOTS * SUBLANES, GLU_FC), F32),
               *[pltpu.VMEM((2 * (HALO + 2 * tm // ODD_PARTS) + ANCHOR_SLOTS * SUBLANES, GLU_FC), F32)
                 for _ in range(ODD_PARTS * (HALF // GLU_FC) // 2 - 1)],
               pltpu.VMEM((HALO + 2 * tm // ODD_PARTS + ANCHOR_SLOTS * SUBLANES, GLU_FC), F32),
               pltpu.VMEM((tm, D_MODEL), BF16)]
    return pl.pallas_call(
        _mix_odd_kernel,
        grid=(b, tiles),
        in_specs=in_specs,
        out_specs=row_spec,
        out_shape=jax.ShapeDtypeStruct(h3.shape, F32),
        scratch_shapes=scratch,
        compiler_params=pltpu.CompilerParams(
            dimension_semantics=("arbitrary", "arbitrary"),
            vmem_limit_bytes=_vmem_limit(args, in_specs, row_spec, F32, scratch)),
        name="mix_odd",
    )(*args)


def kernel(x, p, ffn1_norm, ffn1_w_gu, ffn1_w_down, mix_norm, ffn2_norm, ffn2_w_gu, ffn2_w_down,
           ple_norm, ple_w_gate, ple_w_up, ab_w_in, gm_ln_g, gm_ln_b, gm_w_s, gm_b_s, sc_w,
           ab_w_out, cd_w_in, pool_w, pool_scale, cv_w, cv_b, cv_ln_g, cv_ln_b, cd_w_out,
           final_norm):
    bsz, seq, d = x.shape
    depth = p.shape[0]
    t = bsz * seq
    vec3 = lambda v: v.reshape(v.shape[0], 1, v.shape[1])
    p3 = p.reshape(depth, t, PLE_DIM)
    h = x.reshape(t, d)
    for i in range(depth):
        j = i // 2
        h = _ffn_call(h, i, vec3(ffn1_norm), ffn1_w_gu, ffn1_w_down)
        h3 = h.reshape(bsz, seq, d)
        if i % 2 == 0:
            b_full = jnp.repeat(gm_b_s[j].T, GROUP_W, axis=1)
            h3 = _mix_even_call(h3, i, j, vec3(mix_norm), ab_w_in, vec3(gm_ln_g), vec3(gm_ln_b),
                                gm_w_s, b_full, sc_w, ab_w_out)
        else:
            cv_w8 = jnp.broadcast_to(cv_w[j][:, None, :], (CV_WIDTH, SUBLANES, HALF))
            h3 = _mix_odd_call(h3, i, j, vec3(mix_norm), cd_w_in, pool_w, vec3(pool_scale), cv_w8,
                               vec3(cv_b), vec3(cv_ln_g), vec3(cv_ln_b), cd_w_out)
        h = h3.reshape(t, d)
        h = _ffn_call(h, i, vec3(ffn2_norm), ffn2_w_gu, ffn2_w_down,
                      ple=(p3, vec3(ple_norm), ple_w_gate, ple_w_up),
                      final_norm=final_norm.reshape(1, d) if i == depth - 1 else None)
    return h.reshape(bsz, seq, d)
```

```python
import functools
import math

import jax
import jax.numpy as jnp
from jax import lax
from jax.experimental import pallas as pl
from jax.experimental.pallas import tpu as pltpu

F32 = jnp.float32
BF16 = jnp.bfloat16

D_MODEL = 1024
PLE_DIM = 256
D_FF = 2816
HALF = 512
LANES = 128
SUBLANES = 8
GROUP_W = 128
N_GROUPS = 4
CHUNK = 128
POOL_WINDOWS = (2, 4, 8, 16)
SC_WIDTH = 3
CV_WIDTH = 31
RMS_EPS = 1e-6
LN_EPS = 1e-5

FFN_TM = 512
FFN_TM_FIRST = 1024
FFN_FC = 256
FFN_NCH = D_FF // FFN_FC
MIX_TM = 1024
MIX_TM_ODD = 1024
ANCHOR_SLOTS = 4
ODD_PARTS = 2
GLU_FC = 256
CAST_ROWS = 256
SC_HALO = 8
HALO = 32
CV_ROWS = 128
SPILL_MARGIN = 6 * 1024 * 1024


def _rms_scale(x):
    return lax.rsqrt(jnp.mean(x * x, axis=-1, keepdims=True) + RMS_EPS)


def _rmsnorm(x, g):
    return x * _rms_scale(x) * g


def _layernorm(x, g, b):
    mu = jnp.mean(x, axis=-1, keepdims=True)
    xc = x - mu
    var = jnp.mean(xc * xc, axis=-1, keepdims=True)
    return xc * lax.rsqrt(var + LN_EPS) * g + b


def _sigmoid(x):
    return 0.5 + 0.5 * jnp.tanh(0.5 * x)


def _silu(x):
    hx = 0.5 * x
    return hx + hx * jnp.tanh(hx)


def _gelu_tanh(x):
    c = 0.7978845608028654
    return 0.5 * x * (1.0 + jnp.tanh(c * (x + 0.044715 * (x * x * x))))


def _dot(a, b):
    return jnp.dot(a, b, preferred_element_type=F32)


def _cast_rows(dst_ref, src_ref):
    rows = src_ref.shape[1]
    step = min(rows, CAST_ROWS)
    for r in range(0, rows, step):
        dst_ref[r:r + step, :] = src_ref[0, r:r + step, :].astype(BF16)


def _const_spec(shape, index=None):
    index = (0,) * len(shape) if index is None else index
    return pl.BlockSpec(shape, lambda *_: index, pipeline_mode=pl.Buffered(1))


def _vmem_limit(args, in_specs, out_spec, out_dtype, scratch):
    nbytes = lambda shape, dtype: math.prod(shape) * jnp.dtype(dtype).itemsize
    total = 2 * nbytes(out_spec.block_shape, out_dtype) + SPILL_MARGIN
    for arg, spec in zip(args, in_specs):
        if spec.block_shape is not None:
            total += (1 if spec.pipeline_mode is not None else 2) * nbytes(spec.block_shape, arg.dtype)
    return total + sum(nbytes(s.shape, s.dtype) for s in scratch)


def _ffn_kernel(*refs, layer, with_ple, with_final):
    h_ref, nw_ref, wgu_hbm, wd_hbm = refs[:4]
    k = 4
    if with_ple:
        p_ref, pn_ref, wgate_ref, wup_ref = refs[k:k + 4]
        k += 4
    if with_final:
        fn_ref = refs[k]
        k += 1
    o_ref, xn_ref, r_ref, a_ref, wg_scr, wu_scr, wd_scr, stg_g, stg_u, stg_d = refs[k:k + 10]
    k += 10
    sem = refs[-1]
    if with_ple:
        wgate_scr, wup_scr = refs[k:k + 2]
    s = pl.program_id(0)

    def normalize():
        x = h_ref[...]
        xn_ref[...] = (x * nw_ref[0]).astype(BF16)
        r_ref[...] = jnp.broadcast_to(_rms_scale(x), r_ref.shape)

    def up_chunk(c):
        sl = slice(c * FFN_FC, (c + 1) * FFN_FC)
        g = r_ref[...] * _dot(xn_ref[...], wg_scr[c])
        u = r_ref[...] * _dot(xn_ref[...], wu_scr[c])
        a_ref[:, sl] = (_silu(g) * u).astype(BF16)

    def finish():
        h1 = h_ref[...] + 0.5 * _dot(a_ref[...], wd_scr[...])
        if with_ple:
            xn_ref[...] = (h1 * pn_ref[0]).astype(BF16)
            gate = _sigmoid(_rms_scale(h1) * _dot(xn_ref[...], wgate_scr[...]))
            up = _dot(p_ref[0].astype(BF16), wup_scr[...])
            h1 = h1 + gate * up
        if with_final:
            h1 = _rmsnorm(h1, fn_ref[...])
        o_ref[...] = h1

    def stage_copies(c):
        slot = c % 2
        cols = pl.ds(c * FFN_FC, FFN_FC)
        ucols = pl.ds(D_FF + c * FFN_FC, FFN_FC)
        return (pltpu.make_async_copy(wgu_hbm.at[layer, pl.ds(0, D_MODEL), cols], stg_g.at[slot],
                                      sem.at[0, slot]),
                pltpu.make_async_copy(wgu_hbm.at[layer, pl.ds(0, D_MODEL), ucols], stg_u.at[slot],
                                      sem.at[1, slot]),
                pltpu.make_async_copy(wd_hbm.at[layer, cols, pl.ds(0, D_MODEL)], stg_d.at[slot],
                                      sem.at[2, slot]))

    @pl.when(s == 0)
    def _():
        for k_cp, cp in enumerate(stage_copies(0)):
            cp.start(priority=k_cp % 2)
        normalize()
        if with_ple:
            _cast_rows(wgate_scr, wgate_ref)
            _cast_rows(wup_scr, wup_ref)
        for c in range(FFN_NCH):
            if c + 1 < FFN_NCH:
                for k_cp, cp in enumerate(stage_copies(c + 1)):
                    cp.start(priority=k_cp % 2)
            for cp in stage_copies(c):
                cp.wait()
            slot = c % 2
            wg_scr[c] = stg_g[slot].astype(BF16)
            wu_scr[c] = stg_u[slot].astype(BF16)
            wd_scr[c * FFN_FC:(c + 1) * FFN_FC, :] = stg_d[slot].astype(BF16)
            up_chunk(c)
        finish()

    @pl.when(s > 0)
    def _():
        normalize()
        for c in range(FFN_NCH):
            up_chunk(c)
        finish()


def _ffn_tile(layer, with_ple):
    return FFN_TM_FIRST if (layer == 0 and not with_ple) else FFN_TM


def _ffn_call(h, layer, norm_w, w_gu, w_d, ple=None, final_norm=None):
    t = h.shape[0]
    tm = _ffn_tile(layer, ple is not None)
    assert t % tm == 0 and D_FF % FFN_FC == 0
    row_spec = pl.BlockSpec((tm, D_MODEL), lambda s: (s, 0))
    in_specs = [row_spec, _const_spec((1, 1, D_MODEL), (layer, 0, 0)),
                pl.BlockSpec(memory_space=pl.ANY), pl.BlockSpec(memory_space=pl.ANY)]
    args = [h, norm_w, w_gu, w_d]
    scratch = [pltpu.VMEM((tm, D_MODEL), BF16),
               pltpu.VMEM((tm, FFN_FC), F32),
               pltpu.VMEM((tm, D_FF), BF16),
               pltpu.VMEM((FFN_NCH, D_MODEL, FFN_FC), BF16),
               pltpu.VMEM((FFN_NCH, D_MODEL, FFN_FC), BF16),
               pltpu.VMEM((D_FF, D_MODEL), BF16),
               pltpu.VMEM((2, D_MODEL, FFN_FC), F32),
               pltpu.VMEM((2, D_MODEL, FFN_FC), F32),
               pltpu.VMEM((2, FFN_FC, D_MODEL), F32)]
    if ple is not None:
        p, pn, wgate, wup = ple
        in_specs += [pl.BlockSpec((1, tm, PLE_DIM), lambda s: (layer, s, 0)),
                     _const_spec((1, 1, D_MODEL), (layer, 0, 0)),
                     _const_spec((1, D_MODEL, D_MODEL), (layer, 0, 0)),
                     _const_spec((1, PLE_DIM, D_MODEL), (layer, 0, 0))]
        args += [p, pn, wgate, wup]
        scratch += [pltpu.VMEM((D_MODEL, D_MODEL), BF16), pltpu.VMEM((PLE_DIM, D_MODEL), BF16)]
    if final_norm is not None:
        in_specs.append(_const_spec((1, D_MODEL)))
        args.append(final_norm)
    return pl.pallas_call(
        functools.partial(_ffn_kernel, layer=layer, with_ple=ple is not None,
                          with_final=final_norm is not None),
        grid=(t // tm,),
        in_specs=in_specs,
        out_specs=row_spec,
        out_shape=jax.ShapeDtypeStruct((t, D_MODEL), F32),
        scratch_shapes=scratch + [pltpu.SemaphoreType.DMA((3, 2))],
        compiler_params=pltpu.CompilerParams(
            dimension_semantics=("arbitrary",),
            vmem_limit_bytes=_vmem_limit(args, in_specs, row_spec, F32, scratch)),
        name="ffn_ple" if ple is not None else "ffn",
    )(*args)


def _mix_even_kernel(h_ref, nw_ref, win_ref, lng_ref, lnb_ref, ws_ref, bs_ref, scw_ref, wout_ref,
                     o_ref, win_scr, wout_scr, wt_scr, hn_ref, u_ref, bg_ref, vn_ref, q_ref, y_ref):
    tm = MIX_TM
    i = pl.program_id(0)
    j = pl.program_id(1)

    @pl.when((i == 0) & (j == 0))
    def _():
        _cast_rows(win_scr, win_ref)
        _cast_rows(wout_scr, wout_ref)
        row = lax.broadcasted_iota(jnp.int32, (CHUNK, CHUNK), 0)
        col = lax.broadcasted_iota(jnp.int32, (CHUNK, CHUNK), 1)
        for g in range(N_GROUPS):
            wt_scr[g] = jnp.where(col <= row, ws_ref[0, g], 0.0).astype(BF16)

    @pl.when(j == 0)
    def _():
        q_ref[0:SC_HALO, :] = jnp.zeros((SC_HALO, HALF), F32)

    x = h_ref[0]
    hn_ref[...] = _rmsnorm(x, nw_ref[0]).astype(BF16)

    v = _gelu_tanh(_dot(hn_ref[...], win_scr[:, HALF:2 * HALF]))
    for g in range(N_GROUPS):
        gsl = slice(g * GROUP_W, (g + 1) * GROUP_W)
        vn_ref[:, gsl] = _layernorm(v[:, gsl], lng_ref[0][:, gsl], lnb_ref[0][:, gsl]).astype(BF16)
    u_ref[...] = _dot(hn_ref[...], win_scr[:, 0:HALF])
    q_ref[SC_HALO:SC_HALO + tm, :] = (_dot(hn_ref[...], win_scr[:, 3 * HALF:4 * HALF])
                                      * _dot(hn_ref[...], win_scr[:, 4 * HALF:5 * HALF]))
    bg_ref[...] = _dot(hn_ref[...], win_scr[:, 2 * HALF:3 * HALF])
    for g in range(N_GROUPS):
        gsl = slice(g * GROUP_W, (g + 1) * GROUP_W)
        for c in range(tm // CHUNK):
            rsl = slice(c * CHUNK, (c + 1) * CHUNK)
            mixed = _dot(wt_scr[g], vn_ref[rsl, gsl]) + bs_ref[:, gsl]
            y_ref[rsl, gsl] = (_gelu_tanh(u_ref[rsl, gsl]) * mixed).astype(BF16)
    o_ref[0] = x + _dot(y_ref[:, 0:HALF], wout_scr[0:HALF, :])

    conv = scw_ref[0, 2:3, :] * q_ref[SC_HALO:SC_HALO + tm, :]
    conv = conv + scw_ref[0, 1:2, :] * q_ref[SC_HALO - 1:SC_HALO - 1 + tm, :]
    conv = conv + scw_ref[0, 0:1, :] * q_ref[SC_HALO - 2:SC_HALO - 2 + tm, :]
    y_ref[:, HALF:2 * HALF] = (bg_ref[...] * conv).astype(BF16)
    q_ref[0:SC_HALO, :] = q_ref[tm:tm + SC_HALO, :]

    o_ref[0] = o_ref[0] + _dot(y_ref[:, HALF:2 * HALF], wout_scr[HALF:2 * HALF, :])


def _mix_even_call(h3, layer, j, norm_w, w_in, ln_g, ln_b, w_s, b_full, sc_w, w_out):
    b, s, _ = h3.shape
    tm = MIX_TM
    assert s % tm == 0 and tm % CHUNK == 0 and SC_HALO >= SC_WIDTH - 1
    row_spec = pl.BlockSpec((1, tm, D_MODEL), lambda i, t: (i, t, 0))
    in_specs = [row_spec, _const_spec((1, 1, D_MODEL), (layer, 0, 0)),
                _const_spec((1, D_MODEL, 5 * HALF), (j, 0, 0)),
                _const_spec((1, 1, HALF), (j, 0, 0)), _const_spec((1, 1, HALF), (j, 0, 0)),
                _const_spec((1, N_GROUPS, CHUNK, CHUNK), (j, 0, 0, 0)), _const_spec((CHUNK, HALF)),
                _const_spec((1, SC_WIDTH, HALF), (j, 0, 0)),
                _const_spec((1, D_MODEL, D_MODEL), (j, 0, 0))]
    args = [h3, norm_w, w_in, ln_g, ln_b, w_s, b_full, sc_w, w_out]
    scratch = [pltpu.VMEM((D_MODEL, 5 * HALF), BF16),
               pltpu.VMEM((D_MODEL, D_MODEL), BF16),
               pltpu.VMEM((N_GROUPS, CHUNK, CHUNK), BF16),
               pltpu.VMEM((tm, D_MODEL), BF16),
               pltpu.VMEM((tm, HALF), F32),
               pltpu.VMEM((tm, HALF), F32),
               pltpu.VMEM((tm, HALF), BF16),
               pltpu.VMEM((SC_HALO + tm, HALF), F32),
               pltpu.VMEM((tm, D_MODEL), BF16)]
    return pl.pallas_call(
        _mix_even_kernel,
        grid=(b, s // tm),
        in_specs=in_specs,
        out_specs=row_spec,
        out_shape=jax.ShapeDtypeStruct(h3.shape, F32),
        scratch_shapes=scratch,
        compiler_params=pltpu.CompilerParams(
            dimension_semantics=("arbitrary", "arbitrary"),
            vmem_limit_bytes=_vmem_limit(args, in_specs, row_spec, F32, scratch)),
        name="mix_even",
    )(*args)


class _RowView:
    def __init__(self, ref, base):
        self.ref, self.base = ref, base

    def _shift(self, idx):
        rows, lanes = idx
        if isinstance(rows, slice):
            rows = slice(rows.start + self.base, rows.stop + self.base)
        else:
            rows = pl.ds(rows.start + self.base, rows.size)
        return rows, lanes

    def __getitem__(self, idx):
        return self.ref[self._shift(idx)]

    def __setitem__(self, idx, val):
        self.ref[self._shift(idx)] = val


def _mix_odd_kernel(zs_ref, h_ref, nw_ref, win_ref, pw_ref, ps_ref, cvw_ref, cvb_ref, lng_ref,
                    lnb_ref, wout_ref, o_ref, win_scr, wout_scr, pw_scr, hn_ref, p_ref, sa_ref,
                    sb_ref, raw_ref, *rest):
    c_refs, y_ref = rest[:-1], rest[-1]
    n_lane = HALF // GLU_FC

    unit_rows = HALO + 2 * (MIX_TM_ODD // ODD_PARTS)

    def c_ref_of(part, c):
        u = part * n_lane + c
        return _RowView(c_refs[(u + 1) // 2], unit_rows * ((u + 1) % 2) if u > 0 else 0)

    tm = MIX_TM_ODD
    n = HALO + tm
    rp = tm // ODD_PARTS
    npart = HALO + rp
    i = pl.program_id(0)
    j = pl.program_id(1)

    @pl.when((i == 0) & (j == 0))
    def _():
        _cast_rows(win_scr, win_ref)
        _cast_rows(wout_scr, wout_ref)
        for g in range(N_GROUPS):
            pw_scr[g] = pw_ref[0, g].astype(BF16)

    @pl.when(j == 0)
    def _():
        p_ref[0:HALO, :] = jnp.zeros((HALO, HALF), F32)
        for c in range(n_lane):
            c_ref_of(0, c)[0:HALO, :] = jnp.zeros((HALO, GLU_FC), F32)

    zero = zs_ref[0]
    n_units = ODD_PARTS * n_lane
    anchor_count = {}

    def anchor(acc, u):
        if u < n_units:
            ref = c_refs[(u + 1) // 2]
            slot = anchor_count.get(u, 0)
            anchor_count[u] = slot + 1
            spare = ref.shape[0] - ANCHOR_SLOTS * SUBLANES + slot * SUBLANES
            ref[spare:spare + SUBLANES, :] = acc[acc.shape[0] - SUBLANES:, acc.shape[1] - GLU_FC:]

    def front(part):
        rows = slice(part * rp, (part + 1) * rp)
        for c in range(n_lane):
            a = _dot(hn_ref[rows, :], win_scr[:, HALF + c * GLU_FC:HALF + (c + 1) * GLU_FC])
            g = _dot(hn_ref[rows, :], win_scr[:, 2 * HALF + c * GLU_FC:2 * HALF + (c + 1) * GLU_FC])
            if part == 0:
                c_ref_of(part, c)[HALO:npart, :] = a * _sigmoid(g)
            else:
                raw_ref[c, :, 0:GLU_FC] = a
                raw_ref[c, :, GLU_FC:2 * GLU_FC] = g
        pin = _dot(hn_ref[rows, :], win_scr[:, 0:HALF])
        p_ref[HALO + part * rp:HALO + (part + 1) * rp, :] = pin

    def glu(part):
        for c in range(n_lane):
            if part > 0:
                c_ref_of(part, c)[0:HALO, :] = c_ref_of(part - 1, c)[rp:npart, :]
                c_ref_of(part, c)[HALO:npart, :] = (raw_ref[c, :, 0:GLU_FC]
                                                    * _sigmoid(raw_ref[c, :, GLU_FC:2 * GLU_FC]))

    def pool(part):
        rows = slice(part * rp, (part + 1) * rp)
        b0 = part * rp
        e = b0 + npart
        g1, g2, g3 = GROUP_W, 2 * GROUP_W, 3 * GROUP_W
        sa_ref[b0 + 8:e, :] = p_ref[b0 + 8:e, :] + p_ref[b0 + 7:e - 1, :]
        sb_ref[b0 + 16:e, g1:] = sa_ref[b0 + 16:e, g1:] + sa_ref[b0 + 14:e - 2, g1:]
        sa_ref[b0 + 24:e, g2:] = sb_ref[b0 + 24:e, g2:] + sb_ref[b0 + 20:e - 4, g2:]
        sb_ref[b0 + 32:e, g3:] = sa_ref[b0 + 32:e, g3:] + sa_ref[b0 + 24:e - 8, g3:]
        head = slice(HALO, HALO + max(POOL_WINDOWS))
        pos = lax.broadcasted_iota(jnp.int32, (max(POOL_WINDOWS), GROUP_W), 0)
        for gi, win in enumerate(POOL_WINDOWS):
            gsl = slice(gi * GROUP_W, (gi + 1) * GROUP_W)
            s_ref = sa_ref if gi % 2 == 0 else sb_ref
            if part == 0:
                count = jnp.minimum(pos + 1, win).astype(F32)
                s_ref[head, gsl] = s_ref[head, gsl] * jnp.where(j == 0, float(win) / count, 1.0)
            pooled = s_ref[b0 + HALO:e, gsl] * (1.0 / win) - p_ref[b0 + HALO:e, gsl]
            yc = _dot(pooled.astype(BF16), pw_scr[gi]) * ps_ref[0][:, gsl]
            y_ref[rows, gsl] = yc.astype(BF16)
        acc = _dot(y_ref[rows, 0:HALF], wout_scr[0:HALF, :])
        o_ref[0, rows, :] = h_ref[0, rows, :] + acc
        anchor(acc, (part + 1) * n_lane + n_lane - 1)

    def conv(part):
        for c in range(n_lane):
            c_ref = c_ref_of(part, c)
            for r in range(rp // CV_ROWS):
                r0 = pl.multiple_of(zero + r * CV_ROWS, CV_ROWS)
                for lt in range(GLU_FC // LANES):
                    lsl = slice(lt * LANES, (lt + 1) * LANES)
                    wsl = slice(c * GLU_FC + lt * LANES, c * GLU_FC + (lt + 1) * LANES)
                    chunk = c_ref[pl.ds(r0, HALO + CV_ROWS), lsl]
                    acc = None
                    for b in range(SUBLANES):
                        rolled = chunk if b == 0 else pltpu.roll(chunk, b, axis=0)
                        for a in range(HALO // SUBLANES):
                            d = SUBLANES * a + b
                            if d >= CV_WIDTH:
                                continue
                            lo = HALO - SUBLANES * a
                            src = rolled[lo:lo + CV_ROWS].reshape(CV_ROWS // SUBLANES, SUBLANES, LANES)
                            term = cvw_ref[CV_WIDTH - 1 - d, :, wsl][None] * src
                            acc = term if acc is None else acc + term
                    acc = acc + cvb_ref[0][:, wsl][None]
                    c_ref[pl.ds(r0 + npart, CV_ROWS), lsl] = acc.reshape(CV_ROWS, LANES)

    def back(part):
        rows = slice(part * rp, (part + 1) * rp)
        hc = jnp.concatenate([c_ref_of(part, c)[npart:npart + rp, :] for c in range(n_lane)], axis=-1)
        ln = _layernorm(hc, lng_ref[0], lnb_ref[0])
        y_ref[rows, HALF:2 * HALF] = _silu(ln).astype(BF16)
        acc = _dot(y_ref[rows, HALF:2 * HALF], wout_scr[HALF:2 * HALF, :])
        o_ref[0, rows, :] = o_ref[0, rows, :] + acc
        anchor(acc, (part + 1) * n_lane + n_lane - 1)

    hn_ref[...] = _rmsnorm(h_ref[0], nw_ref[0]).astype(BF16)
    front(0)
    anchor(hn_ref[tm - 2 * SUBLANES:tm, :].astype(F32), 0)
    for part in range(ODD_PARTS):
        if part + 1 < ODD_PARTS:
            front(part + 1)
        pool(part)
        glu(part)
        conv(part)
        back(part)
    p_ref[0:HALO, :] = p_ref[tm:n, :]
    for c in range(n_lane):
        c_ref_of(0, c)[0:HALO, :] = c_ref_of(ODD_PARTS - 1, c)[rp:npart, :]


def _mix_odd_call(h3, layer, j, norm_w, w_in, pool_w, pool_scale, cv_w8, cv_b, ln_g, ln_b, w_out):
    b, s, _ = h3.shape
    tm = MIX_TM_ODD
    tiles = s // tm
    assert s % tm == 0 and (tm // ODD_PARTS) % CV_ROWS == 0 and HALO >= CV_WIDTH - 1 and ODD_PARTS == 2
    assert HALO >= 2 * max(POOL_WINDOWS) - 1 and tm >= max(POOL_WINDOWS)
    row_spec = pl.BlockSpec((1, tm, D_MODEL), lambda i, t: (i, t, 0))
    vec = _const_spec((1, 1, HALF), (j, 0, 0))
    in_specs = [pl.BlockSpec(memory_space=pltpu.SMEM), row_spec,
                _const_spec((1, 1, D_MODEL), (layer, 0, 0)),
                _const_spec((1, D_MODEL, 3 * HALF), (j, 0, 0)),
                _const_spec((1, N_GROUPS, GROUP_W, GROUP_W), (j, 0, 0, 0)), vec,
                _const_spec((CV_WIDTH, SUBLANES, HALF)), vec, vec, vec,
                _const_spec((1, D_MODEL, D_MODEL), (j, 0, 0))]
    args = [jnp.zeros((1,), jnp.int32), h3, norm_w, w_in, pool_w, pool_scale, cv_w8, cv_b, ln_g, ln_b,
            w_out]
    scratch = [pltpu.VMEM((D_MODEL, 3 * HALF), BF16),
               pltpu.VMEM((D_MODEL, D_MODEL), BF16),
               pltpu.VMEM((N_GROUPS, GROUP_W, GROUP_W), BF16),
               pltpu.VMEM((tm, D_MODEL), BF16),
               pltpu.VMEM((HALO + tm, HALF), F32),
               pltpu.VMEM((HALO + tm, HALF), F32),
               pltpu.VMEM((HALO + tm, HALF), F32),
               pltpu.VMEM((HALF // GLU_FC, tm // ODD_PARTS, 2 * GLU_FC), F32),
               pltpu.VMEM((HALO + 2 * tm // ODD_PARTS + ANCHOR_SLOTS * SUBLANES, GLU_FC), F32),
               *[pltpu.VMEM((2 * (HALO + 2 * tm // ODD_PARTS) + ANCHOR_SLOTS * SUBLANES, GLU_FC), F32)
                 for _ in range(ODD_PARTS * (HALF // GLU_FC) // 2 - 1)],
               pltpu.VMEM((HALO + 2 * tm // ODD_PARTS + ANCHOR_SLOTS * SUBLANES, GLU_FC), F32),
               pltpu.VMEM((tm, D_MODEL), BF16)]
    return pl.pallas_call(
        _mix_odd_kernel,
        grid=(b, tiles),
        in_specs=in_specs,
        out_specs=row_spec,
        out_shape=jax.ShapeDtypeStruct(h3.shape, F32),
        scratch_shapes=scratch,
        compiler_params=pltpu.CompilerParams(
            dimension_semantics=("arbitrary", "arbitrary"),
            vmem_limit_bytes=_vmem_limit(args, in_specs, row_spec, F32, scratch)),
        name="mix_odd",
    )(*args)


def kernel(x, p, ffn1_norm, ffn1_w_gu, ffn1_w_down, mix_norm, ffn2_norm, ffn2_w_gu, ffn2_w_down,
           ple_norm, ple_w_gate, ple_w_up, ab_w_in, gm_ln_g, gm_ln_b, gm_w_s, gm_b_s, sc_w,
           ab_w_out, cd_w_in, pool_w, pool_scale, cv_w, cv_b, cv_ln_g, cv_ln_b, cd_w_out,
           final_norm):
    bsz, seq, d = x.shape
    depth = p.shape[0]
    t = bsz * seq
    vec3 = lambda v: v.reshape(v.shape[0], 1, v.shape[1])
    p3 = p.reshape(depth, t, PLE_DIM)
    h = x.reshape(t, d)
    for i in range(depth):
        j = i // 2
        h = _ffn_call(h, i, vec3(ffn1_norm), ffn1_w_gu, ffn1_w_down)
        h3 = h.reshape(bsz, seq, d)
        if i % 2 == 0:
            b_full = jnp.repeat(gm_b_s[j].T, GROUP_W, axis=1)
            h3 = _mix_even_call(h3, i, j, vec3(mix_norm), ab_w_in, vec3(gm_ln_g), vec3(gm_ln_b),
                                gm_w_s, b_full, sc_w, ab_w_out)
        else:
            cv_w8 = jnp.broadcast_to(cv_w[j][:, None, :], (CV_WIDTH, SUBLANES, HALF))
            h3 = _mix_odd_call(h3, i, j, vec3(mix_norm), cd_w_in, pool_w, vec3(pool_scale), cv_w8,
                               vec3(cv_b), vec3(cv_ln_g), vec3(cv_ln_b), cd_w_out)
        h = h3.reshape(t, d)
        h = _ffn_call(h, i, vec3(ffn2_norm), ffn2_w_gu, ffn2_w_down,
                      ple=(p3, vec3(ple_norm), ple_w_gate, ple_w_up),
                      final_norm=final_norm.reshape(1, d) if i == depth - 1 else None)
    return h.reshape(bsz, seq, d)
```

```python
import functools
import math

import jax
import jax.numpy as jnp
from jax import lax
from jax.experimental import pallas as pl
from jax.experimental.pallas import tpu as pltpu

F32 = jnp.float32
BF16 = jnp.bfloat16

D_MODEL = 1024
PLE_DIM = 256
D_FF = 2816
HALF = 512
LANES = 128
SUBLANES = 8
GROUP_W = 128
N_GROUPS = 4
CHUNK = 128
POOL_WINDOWS = (2, 4, 8, 16)
SC_WIDTH = 3
CV_WIDTH = 31
RMS_EPS = 1e-6
LN_EPS = 1e-5

FFN_TM = 512
FFN_TM_FIRST = 1024
FFN_FC = 256
FFN_NCH = D_FF // FFN_FC
MIX_TM = 1024
MIX_TM_ODD = 1024
ANCHOR_SLOTS = 4
ODD_PARTS = 2
GLU_FC = 256
CAST_ROWS = 256
SC_HALO = 8
HALO = 32
CV_ROWS = 128
SPILL_MARGIN = 6 * 1024 * 1024


def _rms_scale(x):
    return lax.rsqrt(jnp.mean(x * x, axis=-1, keepdims=True) + RMS_EPS)


def _rmsnorm(x, g):
    return x * _rms_scale(x) * g


def _layernorm(x, g, b):
    mu = jnp.mean(x, axis=-1, keepdims=True)
    xc = x - mu
    var = jnp.mean(xc * xc, axis=-1, keepdims=True)
    return xc * lax.rsqrt(var + LN_EPS) * g + b


def _sigmoid(x):
    return 0.5 + 0.5 * jnp.tanh(0.5 * x)


def _silu(x):
    hx = 0.5 * x
    return hx + hx * jnp.tanh(hx)


def _gelu_tanh(x):
    c = 0.7978845608028654
    return 0.5 * x * (1.0 + jnp.tanh(c * (x + 0.044715 * (x * x * x))))


def _dot(a, b):
    return jnp.dot(a, b, preferred_element_type=F32)


def _cast_rows(dst_ref, src_ref):
    rows = src_ref.shape[1]
    step = min(rows, CAST_ROWS)
    for r in range(0, rows, step):
        dst_ref[r:r + step, :] = src_ref[0, r:r + step, :].astype(BF16)


def _const_spec(shape, index=None):
    index = (0,) * len(shape) if index is None else index
    return pl.BlockSpec(shape, lambda *_: index, pipeline_mode=pl.Buffered(1))


def _vmem_limit(args, in_specs, out_spec, out_dtype, scratch):
    nbytes = lambda shape, dtype: math.prod(shape) * jnp.dtype(dtype).itemsize
    total = 2 * nbytes(out_spec.block_shape, out_dtype) + SPILL_MARGIN
    for arg, spec in zip(args, in_specs):
        if spec.block_shape is not None:
            total += (1 if spec.pipeline_mode is not None else 2) * nbytes(spec.block_shape, arg.dtype)
    return total + sum(nbytes(s.shape, s.dtype) for s in scratch)


def _ffn_kernel(*refs, layer, with_ple, with_final):
    h_ref, nw_ref, wgu_hbm, wd_hbm = refs[:4]
    k = 4
    if with_ple:
        p_ref, pn_ref, wgate_ref, wup_ref = refs[k:k + 4]
        k += 4
    if with_final:
        fn_ref = refs[k]
        k += 1
    o_ref, xn_ref, r_ref, a_ref, wg_scr, wu_scr, wd_scr, stg_g, stg_u, stg_d = refs[k:k + 10]
    k += 10
    sem = refs[-1]
    if with_ple:
        wgate_scr, wup_scr = refs[k:k + 2]
    s = pl.program_id(0)

    def normalize():
        x = h_ref[...]
        xn_ref[...] = (x * nw_ref[0]).astype(BF16)
        r_ref[...] = jnp.broadcast_to(_rms_scale(x), r_ref.shape)

    def up_chunk(c):
        sl = slice(c * FFN_FC, (c + 1) * FFN_FC)
        g = r_ref[...] * _dot(xn_ref[...], wg_scr[c])
        u = r_ref[...] * _dot(xn_ref[...], wu_scr[c])
        a_ref[:, sl] = (_silu(g) * u).astype(BF16)

    def finish():
        h1 = h_ref[...] + 0.5 * _dot(a_ref[...], wd_scr[...])
        if with_ple:
            xn_ref[...] = (h1 * pn_ref[0]).astype(BF16)
            gate = _sigmoid(_rms_scale(h1) * _dot(xn_ref[...], wgate_scr[...]))
            up = _dot(p_ref[0].astype(BF16), wup_scr[...])
            h1 = h1 + gate * up
        if with_final:
            h1 = _rmsnorm(h1, fn_ref[...])
        o_ref[...] = h1

    def stage_copies(c):
        slot = c % 2
        cols = pl.ds(c * FFN_FC, FFN_FC)
        ucols = pl.ds(D_FF + c * FFN_FC, FFN_FC)
        return (pltpu.make_async_copy(wgu_hbm.at[layer, pl.ds(0, D_MODEL), cols], stg_g.at[slot],
                                      sem.at[0, slot]),
                pltpu.make_async_copy(wgu_hbm.at[layer, pl.ds(0, D_MODEL), ucols], stg_u.at[slot],
                                      sem.at[1, slot]),
                pltpu.make_async_copy(wd_hbm.at[layer, cols, pl.ds(0, D_MODEL)], stg_d.at[slot],
                                      sem.at[2, slot]))

    @pl.when(s == 0)
    def _():
        def land(c):
            for cp in stage_copies(c):
                cp.wait()
            slot = c % 2
            wg_scr[c] = stg_g[slot].astype(BF16)
            wu_scr[c] = stg_u[slot].astype(BF16)
            wd_scr[c * FFN_FC:(c + 1) * FFN_FC, :] = stg_d[slot].astype(BF16)

        for c in range(min(2, FFN_NCH)):
            for cp in stage_copies(c):
                cp.start()
        normalize()
        if with_ple:
            _cast_rows(wgate_scr, wgate_ref)
            _cast_rows(wup_scr, wup_ref)
        land(0)
        for c in range(FFN_NCH):
            if c + 2 < FFN_NCH:
                for cp in stage_copies(c + 2):
                    cp.start()
            if c + 1 < FFN_NCH:
                land(c + 1)
            up_chunk(c)
        finish()

    @pl.when(s > 0)
    def _():
        normalize()
        for c in range(FFN_NCH):
            up_chunk(c)
        finish()


def _ffn_tile(layer, with_ple):
    return FFN_TM_FIRST if (layer == 0 and not with_ple) else FFN_TM


def _ffn_call(h, layer, norm_w, w_gu, w_d, ple=None, final_norm=None):
    t = h.shape[0]
    tm = _ffn_tile(layer, ple is not None)
    assert t % tm == 0 and D_FF % FFN_FC == 0
    row_spec = pl.BlockSpec((tm, D_MODEL), lambda s: (s, 0))
    in_specs = [row_spec, _const_spec((1, 1, D_MODEL), (layer, 0, 0)),
                pl.BlockSpec(memory_space=pl.ANY), pl.BlockSpec(memory_space=pl.ANY)]
    args = [h, norm_w, w_gu, w_d]
    scratch = [pltpu.VMEM((tm, D_MODEL), BF16),
               pltpu.VMEM((tm, FFN_FC), F32),
               pltpu.VMEM((tm, D_FF), BF16),
               pltpu.VMEM((FFN_NCH, D_MODEL, FFN_FC), BF16),
               pltpu.VMEM((FFN_NCH, D_MODEL, FFN_FC), BF16),
               pltpu.VMEM((D_FF, D_MODEL), BF16),
               pltpu.VMEM((2, D_MODEL, FFN_FC), F32),
               pltpu.VMEM((2, D_MODEL, FFN_FC), F32),
               pltpu.VMEM((2, FFN_FC, D_MODEL), F32)]
    if ple is not None:
        p, pn, wgate, wup = ple
        in_specs += [pl.BlockSpec((1, tm, PLE_DIM), lambda s: (layer, s, 0)),
                     _const_spec((1, 1, D_MODEL), (layer, 0, 0)),
                     _const_spec((1, D_MODEL, D_MODEL), (layer, 0, 0)),
                     _const_spec((1, PLE_DIM, D_MODEL), (layer, 0, 0))]
        args += [p, pn, wgate, wup]
        scratch += [pltpu.VMEM((D_MODEL, D_MODEL), BF16), pltpu.VMEM((PLE_DIM, D_MODEL), BF16)]
    if final_norm is not None:
        in_specs.append(_const_spec((1, D_MODEL)))
        args.append(final_norm)
    return pl.pallas_call(
        functools.partial(_ffn_kernel, layer=layer, with_ple=ple is not None,
                          with_final=final_norm is not None),
        grid=(t // tm,),
        in_specs=in_specs,
        out_specs=row_spec,
        out_shape=jax.ShapeDtypeStruct((t, D_MODEL), F32),
        scratch_shapes=scratch + [pltpu.SemaphoreType.DMA((3, 2))],
        compiler_params=pltpu.CompilerParams(
            dimension_semantics=("arbitrary",),
            vmem_limit_bytes=_vmem_limit(args, in_specs, row_spec, F32, scratch)),
        name="ffn_ple" if ple is not None else "ffn",
    )(*args)


def _mix_even_kernel(h_ref, nw_ref, win_ref, lng_ref, lnb_ref, ws_ref, bs_ref, scw_ref, wout_ref,
                     o_ref, win_scr, wout_scr, wt_scr, hn_ref, u_ref, bg_ref, vn_ref, q_ref, y_ref):
    tm = MIX_TM
    i = pl.program_id(0)
    j = pl.program_id(1)

    @pl.when((i == 0) & (j == 0))
    def _():
        _cast_rows(win_scr, win_ref)
        _cast_rows(wout_scr, wout_ref)
        row = lax.broadcasted_iota(jnp.int32, (CHUNK, CHUNK), 0)
        col = lax.broadcasted_iota(jnp.int32, (CHUNK, CHUNK), 1)
        for g in range(N_GROUPS):
            wt_scr[g] = jnp.where(col <= row, ws_ref[0, g], 0.0).astype(BF16)

    @pl.when(j == 0)
    def _():
        q_ref[0:SC_HALO, :] = jnp.zeros((SC_HALO, HALF), F32)

    x = h_ref[0]
    hn_ref[...] = _rmsnorm(x, nw_ref[0]).astype(BF16)

    v = _gelu_tanh(_dot(hn_ref[...], win_scr[:, HALF:2 * HALF]))
    for g in range(N_GROUPS):
        gsl = slice(g * GROUP_W, (g + 1) * GROUP_W)
        vn_ref[:, gsl] = _layernorm(v[:, gsl], lng_ref[0][:, gsl], lnb_ref[0][:, gsl]).astype(BF16)
    u_ref[...] = _dot(hn_ref[...], win_scr[:, 0:HALF])
    q_ref[SC_HALO:SC_HALO + tm, :] = (_dot(hn_ref[...], win_scr[:, 3 * HALF:4 * HALF])
                                      * _dot(hn_ref[...], win_scr[:, 4 * HALF:5 * HALF]))
    bg_ref[...] = _dot(hn_ref[...], win_scr[:, 2 * HALF:3 * HALF])
    for g in range(N_GROUPS):
        gsl = slice(g * GROUP_W, (g + 1) * GROUP_W)
        for c in range(tm // CHUNK):
            rsl = slice(c * CHUNK, (c + 1) * CHUNK)
            mixed = _dot(wt_scr[g], vn_ref[rsl, gsl]) + bs_ref[:, gsl]
            y_ref[rsl, gsl] = (_gelu_tanh(u_ref[rsl, gsl]) * mixed).astype(BF16)
    o_ref[0] = x + _dot(y_ref[:, 0:HALF], wout_scr[0:HALF, :])

    conv = scw_ref[0, 2:3, :] * q_ref[SC_HALO:SC_HALO + tm, :]
    conv = conv + scw_ref[0, 1:2, :] * q_ref[SC_HALO - 1:SC_HALO - 1 + tm, :]
    conv = conv + scw_ref[0, 0:1, :] * q_ref[SC_HALO - 2:SC_HALO - 2 + tm, :]
    y_ref[:, HALF:2 * HALF] = (bg_ref[...] * conv).astype(BF16)
    q_ref[0:SC_HALO, :] = q_ref[tm:tm + SC_HALO, :]

    o_ref[0] = o_ref[0] + _dot(y_ref[:, HALF:2 * HALF], wout_scr[HALF:2 * HALF, :])


def _mix_even_call(h3, layer, j, norm_w, w_in, ln_g, ln_b, w_s, b_full, sc_w, w_out):
    b, s, _ = h3.shape
    tm = MIX_TM
    assert s % tm == 0 and tm % CHUNK == 0 and SC_HALO >= SC_WIDTH - 1
    row_spec = pl.BlockSpec((1, tm, D_MODEL), lambda i, t: (i, t, 0))
    in_specs = [row_spec, _const_spec((1, 1, D_MODEL), (layer, 0, 0)),
                _const_spec((1, D_MODEL, 5 * HALF), (j, 0, 0)),
                _const_spec((1, 1, HALF), (j, 0, 0)), _const_spec((1, 1, HALF), (j, 0, 0)),
                _const_spec((1, N_GROUPS, CHUNK, CHUNK), (j, 0, 0, 0)), _const_spec((CHUNK, HALF)),
                _const_spec((1, SC_WIDTH, HALF), (j, 0, 0)),
                _const_spec((1, D_MODEL, D_MODEL), (j, 0, 0))]
    args = [h3, norm_w, w_in, ln_g, ln_b, w_s, b_full, sc_w, w_out]
    scratch = [pltpu.VMEM((D_MODEL, 5 * HALF), BF16),
               pltpu.VMEM((D_MODEL, D_MODEL), BF16),
               pltpu.VMEM((N_GROUPS, CHUNK, CHUNK), BF16),
               pltpu.VMEM((tm, D_MODEL), BF16),
               pltpu.VMEM((tm, HALF), F32),
               pltpu.VMEM((tm, HALF), F32),
               pltpu.VMEM((tm, HALF), BF16),
               pltpu.VMEM((SC_HALO + tm, HALF), F32),
               pltpu.VMEM((tm, D_MODEL), BF16)]
    return pl.pallas_call(
        _mix_even_kernel,
        grid=(b, s // tm),
        in_specs=in_specs,
        out_specs=row_spec,
        out_shape=jax.ShapeDtypeStruct(h3.shape, F32),
        scratch_shapes=scratch,
        compiler_params=pltpu.CompilerParams(
            dimension_semantics=("arbitrary", "arbitrary"),
            vmem_limit_bytes=_vmem_limit(args, in_specs, row_spec, F32, scratch)),
        name="mix_even",
    )(*args)


class _RowView:
    def __init__(self, ref, base):
        self.ref, self.base = ref, base

    def _shift(self, idx):
        rows, lanes = idx
        if isinstance(rows, slice):
            rows = slice(rows.start + self.base, rows.stop + self.base)
        else:
            rows = pl.ds(rows.start + self.base, rows.size)
        return rows, lanes

    def __getitem__(self, idx):
        return self.ref[self._shift(idx)]

    def __setitem__(self, idx, val):
        self.ref[self._shift(idx)] = val


def _mix_odd_kernel(zs_ref, h_ref, nw_ref, win_ref, pw_ref, ps_ref, cvw_ref, cvb_ref, lng_ref,
                    lnb_ref, wout_ref, o_ref, win_scr, wout_scr, pw_scr, hn_ref, p_ref, sa_ref,
                    sb_ref, raw_ref, *rest):
    c_refs, y_ref = rest[:-1], rest[-1]
    n_lane = HALF // GLU_FC

    unit_rows = HALO + 2 * (MIX_TM_ODD // ODD_PARTS)

    def c_ref_of(part, c):
        u = part * n_lane + c
        return _RowView(c_refs[(u + 1) // 2], unit_rows * ((u + 1) % 2) if u > 0 else 0)

    tm = MIX_TM_ODD
    n = HALO + tm
    rp = tm // ODD_PARTS
    npart = HALO + rp
    i = pl.program_id(0)
    j = pl.program_id(1)

    @pl.when((i == 0) & (j == 0))
    def _():
        _cast_rows(win_scr, win_ref)
        _cast_rows(wout_scr, wout_ref)
        for g in range(N_GROUPS):
            pw_scr[g] = pw_ref[0, g].astype(BF16)

    @pl.when(j == 0)
    def _():
        p_ref[0:HALO, :] = jnp.zeros((HALO, HALF), F32)
        for c in range(n_lane):
            c_ref_of(0, c)[0:HALO, :] = jnp.zeros((HALO, GLU_FC), F32)

    zero = zs_ref[0]
    n_units = ODD_PARTS * n_lane
    anchor_count = {}

    def anchor(acc, u):
        if u < n_units:
            ref = c_refs[(u + 1) // 2]
            slot = anchor_count.get(u, 0)
            anchor_count[u] = slot + 1
            spare = ref.shape[0] - ANCHOR_SLOTS * SUBLANES + slot * SUBLANES
            ref[spare:spare + SUBLANES, :] = acc[acc.shape[0] - SUBLANES:, acc.shape[1] - GLU_FC:]

    def front(part):
        rows = slice(part * rp, (part + 1) * rp)
        for c in range(n_lane):
            a = _dot(hn_ref[rows, :], win_scr[:, HALF + c * GLU_FC:HALF + (c + 1) * GLU_FC])
            g = _dot(hn_ref[rows, :], win_scr[:, 2 * HALF + c * GLU_FC:2 * HALF + (c + 1) * GLU_FC])
            if part == 0:
                c_ref_of(part, c)[HALO:npart, :] = a * _sigmoid(g)
            else:
                raw_ref[c, :, 0:GLU_FC] = a
                raw_ref[c, :, GLU_FC:2 * GLU_FC] = g
        pin = _dot(hn_ref[rows, :], win_scr[:, 0:HALF])
        p_ref[HALO + part * rp:HALO + (part + 1) * rp, :] = pin

    def glu(part):
        for c in range(n_lane):
            if part > 0:
                c_ref_of(part, c)[0:HALO, :] = c_ref_of(part - 1, c)[rp:npart, :]
                c_ref_of(part, c)[HALO:npart, :] = (raw_ref[c, :, 0:GLU_FC]
                                                    * _sigmoid(raw_ref[c, :, GLU_FC:2 * GLU_FC]))

    def pool(part):
        rows = slice(part * rp, (part + 1) * rp)
        b0 = part * rp
        e = b0 + npart
        g1, g2, g3 = GROUP_W, 2 * GROUP_W, 3 * GROUP_W
        sa_ref[b0 + 8:e, :] = p_ref[b0 + 8:e, :] + p_ref[b0 + 7:e - 1, :]
        sb_ref[b0 + 16:e, g1:] = sa_ref[b0 + 16:e, g1:] + sa_ref[b0 + 14:e - 2, g1:]
        sa_ref[b0 + 24:e, g2:] = sb_ref[b0 + 24:e, g2:] + sb_ref[b0 + 20:e - 4, g2:]
        sb_ref[b0 + 32:e, g3:] = sa_ref[b0 + 32:e, g3:] + sa_ref[b0 + 24:e - 8, g3:]
        head = slice(HALO, HALO + max(POOL_WINDOWS))
        pos = lax.broadcasted_iota(jnp.int32, (max(POOL_WINDOWS), GROUP_W), 0)
        for gi, win in enumerate(POOL_WINDOWS):
            gsl = slice(gi * GROUP_W, (gi + 1) * GROUP_W)
            s_ref = sa_ref if gi % 2 == 0 else sb_ref
            if part == 0:
                count = jnp.minimum(pos + 1, win).astype(F32)
                s_ref[head, gsl] = s_ref[head, gsl] * jnp.where(j == 0, float(win) / count, 1.0)
            pooled = s_ref[b0 + HALO:e, gsl] * (1.0 / win) - p_ref[b0 + HALO:e, gsl]
            yc = _dot(pooled.astype(BF16), pw_scr[gi]) * ps_ref[0][:, gsl]
            y_ref[rows, gsl] = yc.astype(BF16)
        acc = _dot(y_ref[rows, 0:HALF], wout_scr[0:HALF, :])
        o_ref[0, rows, :] = h_ref[0, rows, :] + acc
        anchor(acc, (part + 1) * n_lane + n_lane - 1)

    def conv(part):
        for c in range(n_lane):
            c_ref = c_ref_of(part, c)
            for r in range(rp // CV_ROWS):
                r0 = pl.multiple_of(zero + r * CV_ROWS, CV_ROWS)
                for lt in range(GLU_FC // LANES):
                    lsl = slice(lt * LANES, (lt + 1) * LANES)
                    wsl = slice(c * GLU_FC + lt * LANES, c * GLU_FC + (lt + 1) * LANES)
                    chunk = c_ref[pl.ds(r0, HALO + CV_ROWS), lsl]
                    acc = None
                    for b in range(SUBLANES):
                        rolled = chunk if b == 0 else pltpu.roll(chunk, b, axis=0)
                        for a in range(HALO // SUBLANES):
                            d = SUBLANES * a + b
                            if d >= CV_WIDTH:
                                continue
                            lo = HALO - SUBLANES * a
                            src = rolled[lo:lo + CV_ROWS].reshape(CV_ROWS // SUBLANES, SUBLANES, LANES)
                            term = cvw_ref[CV_WIDTH - 1 - d, :, wsl][None] * src
                            acc = term if acc is None else acc + term
                    acc = acc + cvb_ref[0][:, wsl][None]
                    c_ref[pl.ds(r0 + npart, CV_ROWS), lsl] = acc.reshape(CV_ROWS, LANES)

    def back(part):
        rows = slice(part * rp, (part + 1) * rp)
        hc = jnp.concatenate([c_ref_of(part, c)[npart:npart + rp, :] for c in range(n_lane)], axis=-1)
        ln = _layernorm(hc, lng_ref[0], lnb_ref[0])
        y_ref[rows, HALF:2 * HALF] = _silu(ln).astype(BF16)
        acc = _dot(y_ref[rows, HALF:2 * HALF], wout_scr[HALF:2 * HALF, :])
        o_ref[0, rows, :] = o_ref[0, rows, :] + acc
        anchor(acc, (part + 1) * n_lane + n_lane - 1)

    hn_ref[...] = _rmsnorm(h_ref[0], nw_ref[0]).astype(BF16)
    front(0)
    anchor(hn_ref[tm - 2 * SUBLANES:tm, :].astype(F32), 0)
    for part in range(ODD_PARTS):
        if part + 1 < ODD_PARTS:
            front(part + 1)
        pool(part)
        glu(part)
        conv(part)
        back(part)
    p_ref[0:HALO, :] = p_ref[tm:n, :]
    for c in range(n_lane):
        c_ref_of(0, c)[0:HALO, :] = c_ref_of(ODD_PARTS - 1, c)[rp:npart, :]


def _mix_odd_call(h3, layer, j, norm_w, w_in, pool_w, pool_scale, cv_w8, cv_b, ln_g, ln_b, w_out):
    b, s, _ = h3.shape
    tm = MIX_TM_ODD
    tiles = s // tm
    assert s % tm == 0 and (tm // ODD_PARTS) % CV_ROWS == 0 and HALO >= CV_WIDTH - 1 and ODD_PARTS == 2
    assert HALO >= 2 * max(POOL_WINDOWS) - 1 and tm >= max(POOL_WINDOWS)
    row_spec = pl.BlockSpec((1, tm, D_MODEL), lambda i, t: (i, t, 0))
    vec = _const_spec((1, 1, HALF), (j, 0, 0))
    in_specs = [pl.BlockSpec(memory_space=pltpu.SMEM), row_spec,
                _const_spec((1, 1, D_MODEL), (layer, 0, 0)),
                _const_spec((1, D_MODEL, 3 * HALF), (j, 0, 0)),
                _const_spec((1, N_GROUPS, GROUP_W, GROUP_W), (j, 0, 0, 0)), vec,
                _const_spec((CV_WIDTH, SUBLANES, HALF)), vec, vec, vec,
                _const_spec((1, D_MODEL, D_MODEL), (j, 0, 0))]
    args = [jnp.zeros((1,), jnp.int32), h3, norm_w, w_in, pool_w, pool_scale, cv_w8, cv_b, ln_g, ln_b,
            w_out]
    scratch = [pltpu.VMEM((D_MODEL, 3 * HALF), BF16),
               pltpu.VMEM((D_MODEL, D_MODEL), BF16),
               pltpu.VMEM((N_GROUPS, GROUP_W, GROUP_W), BF16),
               pltpu.VMEM((tm, D_MODEL), BF16),
               pltpu.VMEM((HALO + tm, HALF), F32),
               pltpu.VMEM((HALO + tm, HALF), F32),
               pltpu.VMEM((HALO + tm, HALF), F32),
               pltpu.VMEM((HALF // GLU_FC, tm // ODD_PARTS, 2 * GLU_FC), F32),
               pltpu.VMEM((HALO + 2 * tm // ODD_PARTS + ANCHOR_SLOTS * SUBLANES, GLU_FC), F32),
               *[pltpu.VMEM((2 * (HALO + 2 * tm // ODD_PARTS) + ANCHOR_SLOTS * SUBLANES, GLU_FC), F32)
                 for _ in range(ODD_PARTS * (HALF // GLU_FC) // 2 - 1)],
               pltpu.VMEM((HALO + 2 * tm // ODD_PARTS + ANCHOR_SLOTS * SUBLANES, GLU_FC), F32),
               pltpu.VMEM((tm, D_MODEL), BF16)]
    return pl.pallas_call(
        _mix_odd_kernel,
        grid=(b, tiles),
        in_specs=in_specs,
        out_specs=row_spec,
        out_shape=jax.ShapeDtypeStruct(h3.shape, F32),
        scratch_shapes=scratch,
        compiler_params=pltpu.CompilerParams(
            dimension_semantics=("arbitrary", "arbitrary"),
            vmem_limit_bytes=_vmem_limit(args, in_specs, row_spec, F32, scratch)),
        name="mix_odd",
    )(*args)


def kernel(x, p, ffn1_norm, ffn1_w_gu, ffn1_w_down, mix_norm, ffn2_norm, ffn2_w_gu, ffn2_w_down,
           ple_norm, ple_w_gate, ple_w_up, ab_w_in, gm_ln_g, gm_ln_b, gm_w_s, gm_b_s, sc_w,
           ab_w_out, cd_w_in, pool_w, pool_scale, cv_w, cv_b, cv_ln_g, cv_ln_b, cd_w_out,
           final_norm):
    bsz, seq, d = x.shape
    depth = p.shape[0]
    t = bsz * seq
    vec3 = lambda v: v.reshape(v.shape[0], 1, v.shape[1])
    p3 = p.reshape(depth, t, PLE_DIM)
    h = x.reshape(t, d)
    for i in range(depth):
        j = i // 2
        h = _ffn_call(h, i, vec3(ffn1_norm), ffn1_w_gu, ffn1_w_down)
        h3 = h.reshape(bsz, seq, d)
        if i % 2 == 0:
            b_full = jnp.repeat(gm_b_s[j].T, GROUP_W, axis=1)
            h3 = _mix_even_call(h3, i, j, vec3(mix_norm), ab_w_in, vec3(gm_ln_g), vec3(gm_ln_b),
                                gm_w_s, b_full, sc_w, ab_w_out)
        else:
            cv_w8 = jnp.broadcast_to(cv_w[j][:, None, :], (CV_WIDTH, SUBLANES, HALF))
            h3 = _mix_odd_call(h3, i, j, vec3(mix_norm), cd_w_in, pool_w, vec3(pool_scale), cv_w8,
                               vec3(cv_b), vec3(cv_ln_g), vec3(cv_ln_b), cd_w_out)
        h = h3.reshape(t, d)
        h = _ffn_call(h, i, vec3(ffn2_norm), ffn2_w_gu, ffn2_w_down,
                      ple=(p3, vec3(ple_norm), ple_w_gate, ple_w_up),
                      final_norm=final_norm.reshape(1, d) if i == depth - 1 else None)
    return h.reshape(bsz, seq, d)
```
